```python
import math, functools
import jax
import jax.numpy as jnp
from jax import lax
import numpy as np

D_MODEL = 2048
BATCH = 8
SEQ = 4096
DEPTH = 4

N_MIXERS = 4
RMS_EPS = 1e-6
Q_BLOCK = 128

REL_BUCKETS = 32
REL_MAX_DIST = 128
ATTN_HEAD_DIM = 128
ATTN_HEADS = D_MODEL // ATTN_HEAD_DIM

GLA_HEADS = 4
GLA_DK = D_MODEL // (2 * GLA_HEADS)
GLA_DV = D_MODEL // GLA_HEADS
GLA_GATE_RANK = 16
GLA_TAU = 16.0
GLA_CHUNK = 64
GLA_QK_DIM = GLA_HEADS * GLA_DK
GLA_V_DIM = GLA_HEADS * GLA_DV
GLA_IN = 2 * GLA_QK_DIM + 2 * GLA_V_DIM + 2 * GLA_GATE_RANK

GDN_HEAD_DIM = 128
GDN_QK_HEADS = D_MODEL // GDN_HEAD_DIM
GDN_V_HEADS = 2 * GDN_QK_HEADS
GDN_CONV = 5
GDN_CHUNK = 64
GDN_QK_DIM = GDN_QK_HEADS * GDN_HEAD_DIM
GDN_V_DIM = GDN_V_HEADS * GDN_HEAD_DIM
GDN_CONV_DIM = 2 * GDN_QK_DIM + GDN_V_DIM
GDN_IN = GDN_CONV_DIM + GDN_V_DIM + 4 * GDN_V_HEADS

DIFF_HEADS = ATTN_HEADS
DIFF_DQK = D_MODEL // (2 * DIFF_HEADS)
DIFF_DV = 2 * DIFF_DQK
DIFF_IN = 3 * D_MODEL

SWA_HEADS = ATTN_HEADS
SWA_KV_HEADS = 4
SWA_GROUP = SWA_HEADS // SWA_KV_HEADS
SWA_WINDOW = 128
SWA_IN = (SWA_HEADS + 2 * SWA_KV_HEADS) * ATTN_HEAD_DIM

N_EXPERTS = 16
EXPERT_DFF = D_MODEL // 2
EC_CAPACITY_FACTOR = 2

N_GLA = (DEPTH + 3) // N_MIXERS
N_GDN = (DEPTH + 2) // N_MIXERS
N_DIFF = (DEPTH + 1) // N_MIXERS
N_SWA = DEPTH // N_MIXERS

kernel_name = "hybrid_bidir_gla_gdn_diff_swa_ecmoe"


def rms_norm(x, gain):
    xf = x.astype(jnp.float32)
    y = xf * lax.rsqrt(jnp.mean(xf * xf, axis=-1, keepdims=True) + RMS_EPS)
    return (y * gain.astype(jnp.float32)).astype(x.dtype)


def l2_norm(x):
    xf = x.astype(jnp.float32)
    return xf * lax.rsqrt(jnp.sum(xf * xf, axis=-1, keepdims=True) + RMS_EPS)


def t5_bucket(rel):
    half = REL_BUCKETS // 2
    max_exact = half // 2
    n = jnp.abs(rel)
    log_ratio = jnp.log(jnp.maximum(n, 1).astype(jnp.float32) / max_exact) / math.log(REL_MAX_DIST / max_exact)
    large = jnp.minimum(max_exact + (log_ratio * (half - max_exact)).astype(jnp.int32), half - 1)
    return jnp.where(rel > 0, half, 0) + jnp.where(n < max_exact, n, large)


def rel_bias_heads(table, rel):
    return jnp.moveaxis(table[t5_bucket(rel)].astype(jnp.float32), -1, 0)


def to_chunks(t, c):
    return t.reshape(t.shape[:2] + (t.shape[2] // c, c) + t.shape[3:])


def from_chunks(t):
    return t.reshape(t.shape[:2] + (t.shape[2] * t.shape[3],) + t.shape[4:])


def flip_seq(t):
    return jnp.flip(t, axis=2)


def gla_chunk_scan(q, k, v, log_a):
    q, k, v, log_a = (to_chunks(t, GLA_CHUNK) for t in (q, k, v, log_a))
    b = jnp.cumsum(log_a, axis=3)
    b_last = b[:, :, :, -1:, :]
    q_d = q * jnp.exp(b)
    lower = jnp.tril(jnp.ones((GLA_CHUNK, GLA_CHUNK), jnp.float32))
    scores = jnp.einsum('bhnik,bhnjk->bhnij', q_d, k * jnp.exp(-b)) * lower
    o_intra = jnp.einsum('bhnij,bhnjv->bhniv', scores, v)
    k_state = k * jnp.exp(b_last - b)
    decay = jnp.exp(b_last[:, :, :, 0, :])

    def step(state, inp):
        qd_c, ks_c, v_c, dec_c = inp
        o = jnp.einsum('bhik,bhkv->bhiv', qd_c, state)
        state = state * dec_c[..., None] + jnp.einsum('bhjk,bhjv->bhkv', ks_c, v_c)
        return state, o

    B, H = q.shape[:2]
    state0 = jnp.zeros((B, H, q.shape[-1], v.shape[-1]), jnp.float32)
    xs = tuple(jnp.moveaxis(t, 2, 0) for t in (q_d, k_state, v, decay))
    _, o_inter = lax.scan(step, state0, xs)
    return from_chunks(o_intra + jnp.moveaxis(o_inter, 0, 2))


def gla_mixer(h, w_in, w_gate_up, b_gate, head_norm, w_out):
    B, S, _ = h.shape
    splits = [GLA_QK_DIM, 2 * GLA_QK_DIM, 2 * GLA_QK_DIM + GLA_V_DIM, 2 * GLA_QK_DIM + 2 * GLA_V_DIM]
    q, k, v, r, g_lo = jnp.split(h @ w_in, splits, axis=-1)
    gate = jnp.einsum('bsdr,dre->bsde', g_lo.reshape(B, S, 2, GLA_GATE_RANK), w_gate_up) + b_gate
    log_a = jax.nn.log_sigmoid(gate.astype(jnp.float32)) / GLA_TAU

    def heads(t, d):
        return t.astype(jnp.float32).reshape(B, S, GLA_HEADS, d).transpose(0, 2, 1, 3)

    qh = heads(q, GLA_DK) * GLA_DK ** -0.5
    kh, vh = heads(k, GLA_DK), heads(v, GLA_DV)
    la_f, la_b = heads(log_a[:, :, 0], GLA_DK), heads(log_a[:, :, 1], GLA_DK)
    o = gla_chunk_scan(qh, kh, vh, la_f) + flip_seq(
        gla_chunk_scan(flip_seq(qh), flip_seq(kh), flip_seq(vh), flip_seq(la_b)))
    o = rms_norm(o, head_norm).transpose(0, 2, 1, 3).reshape(B, S, GLA_V_DIM).astype(h.dtype)
    return (o * jax.nn.silu(r)) @ w_out


def centred_depthwise_conv(x, w):
    K, C = w.shape
    return lax.conv_general_dilated(x, w[:, None, :], window_strides=(1,), padding=[(K // 2, K // 2)],
                                    dimension_numbers=('NWC', 'WIO', 'NWC'), feature_group_count=C)


def gdn_chunk_scan(q, k, v, g, beta):
    C = GDN_CHUNK
    q, k, v, g, beta = (to_chunks(t, C) for t in (q, k, v, g, beta))
    gc = jnp.cumsum(g, axis=-1)
    lower_incl = jnp.tril(jnp.ones((C, C), bool))
    lower_strict = jnp.tril(jnp.ones((C, C), bool), -1)
    gamma = jnp.exp(jnp.where(lower_incl, gc[..., :, None] - gc[..., None, :], -jnp.inf))
    k_beta = k * beta[..., None]
    tri = jnp.where(lower_strict, jnp.einsum('bhnik,bhnjk->bhnij', k_beta, k) * gamma, 0.0) \
        + jnp.eye(C, dtype=jnp.float32)
    solve = functools.partial(lax.linalg.triangular_solve, left_side=True, lower=True, unit_diagonal=True)
    u = solve(tri, v * beta[..., None])
    w = solve(tri, k_beta * jnp.exp(gc)[..., None])
    qk = jnp.einsum('bhnik,bhnjk->bhnij', q, k) * gamma
    q_d = q * jnp.exp(gc)[..., None]
    k_state = k * jnp.exp(gc[..., -1:] - gc)[..., None]
    decay = jnp.exp(gc[..., -1])

    def step(state, inp):
        qd_c, ks_c, u_c, w_c, qk_c, dec_c = inp
        v_new = u_c - jnp.einsum('bhik,bhkv->bhiv', w_c, state)
        o = jnp.einsum('bhik,bhkv->bhiv', qd_c, state) + jnp.einsum('bhij,bhjv->bhiv', qk_c, v_new)
        state = state * dec_c[..., None, None] + jnp.einsum('bhjk,bhjv->bhkv', ks_c, v_new)
        return state, o

    B, H = q.shape[:2]
    state0 = jnp.zeros((B, H, q.shape[-1], v.shape[-1]), jnp.float32)
    xs = tuple(jnp.moveaxis(t, 2, 0) for t in (q_d, k_state, u, w, qk, decay))
    _, o = lax.scan(step, state0, xs)
    return from_chunks(jnp.moveaxis(o, 0, 2))


def gdn_mixer(h, w_in, conv_w, a_log, dt_bias, head_norm, w_out):
    B, S, _ = h.shape
    qkv, z, ab = jnp.split(h @ w_in, [GDN_CONV_DIM, GDN_CONV_DIM + GDN_V_DIM], axis=-1)
    qkv = jax.nn.silu(centred_depthwise_conv(qkv, conv_w))
    q, k, v = jnp.split(qkv, [GDN_QK_DIM, 2 * GDN_QK_DIM], axis=-1)
    rep = GDN_V_HEADS // GDN_QK_HEADS

    def qk_heads(t):
        t = l2_norm(t.reshape(B, S, GDN_QK_HEADS, GDN_HEAD_DIM))
        return jnp.repeat(t, rep, axis=2).transpose(0, 2, 1, 3)

    qh = qk_heads(q) * GDN_HEAD_DIM ** -0.5
    kh = qk_heads(k)
    vh = v.astype(jnp.float32).reshape(B, S, GDN_V_HEADS, GDN_HEAD_DIM).transpose(0, 2, 1, 3)
    ab = ab.astype(jnp.float32).reshape(B, S, 2, 2, GDN_V_HEADS)
    g = -jnp.exp(a_log.astype(jnp.float32)) * jax.nn.softplus(ab[:, :, 0] + dt_bias.astype(jnp.float32))
    beta = jax.nn.sigmoid(ab[:, :, 1])
    g = g.transpose(2, 0, 3, 1)
    beta = beta.transpose(2, 0, 3, 1)
    o = gdn_chunk_scan(qh, kh, vh, g[0], beta[0]) + flip_seq(
        gdn_chunk_scan(flip_seq(qh), flip_seq(kh), flip_seq(vh), flip_seq(g[1]), flip_seq(beta[1])))
    o = rms_norm(o, head_norm).transpose(0, 2, 1, 3).astype(h.dtype)
    o = o * jax.nn.silu(z.reshape(B, S, GDN_V_HEADS, GDN_HEAD_DIM))
    return o.reshape(B, S, GDN_V_DIM) @ w_out


def diff_mixer(h, w_in, q_norm, k_norm, lam, subln, w_out, rel_table, layer_idx):
    B, S, _ = h.shape
    nb = S // Q_BLOCK
    q, k, v = jnp.split(h @ w_in, 3, axis=-1)
    q = rms_norm(q.reshape(B, S, DIFF_HEADS, 2, DIFF_DQK), q_norm) * DIFF_DQK ** -0.5
    k = rms_norm(k.reshape(B, S, DIFF_HEADS, 2, DIFF_DQK), k_norm)
    v = v.reshape(B, S, DIFF_HEADS, DIFF_DV)
    lambda_init = 0.8 - 0.6 * math.exp(-0.3 * layer_idx)
    lamf = lam.astype(jnp.float32)
    lam_full = jnp.exp(jnp.sum(lamf[0] * lamf[1])) - jnp.exp(jnp.sum(lamf[2] * lamf[3])) + lambda_init
    k_t = k.transpose(0, 2, 3, 1, 4)
    q_blocks = jnp.moveaxis(q.reshape(B, nb, Q_BLOCK, DIFF_HEADS, 2, DIFF_DQK), 1, 0)
    k_pos = jnp.arange(S, dtype=jnp.int32)

    def block(args):
        qb, n = args
        s = jnp.einsum('bqhmd,bhmkd->bhmqk', qb, k_t).astype(jnp.float32)
        q_pos = n * Q_BLOCK + jnp.arange(Q_BLOCK, dtype=jnp.int32)
        s = s + rel_bias_heads(rel_table, k_pos[None, :] - q_pos[:, None])[None, :, None]
        p = jax.nn.softmax(s, axis=-1)
        attn = (p[:, :, 0] - lam_full * p[:, :, 1]).astype(v.dtype)
        return jnp.einsum('bhqk,bkhv->bqhv', attn, v)

    o = lax.map(block, (q_blocks, jnp.arange(nb, dtype=jnp.int32)))
    o = jnp.moveaxis(o, 0, 1).reshape(B, S, DIFF_HEADS, DIFF_DV)
    o = rms_norm(o, subln) * (1.0 - lambda_init)
    return o.reshape(B, S, DIFF_HEADS * DIFF_DV).astype(h.dtype) @ w_out


def swa_mixer(h, w_in, q_norm, k_norm, sink, w_out, rel_table):
    B, S, _ = h.shape
    nb = S // Q_BLOCK
    span = Q_BLOCK + 2 * SWA_WINDOW
    q, k, v = jnp.split(h @ w_in, [SWA_HEADS * ATTN_HEAD_DIM, (SWA_HEADS + SWA_KV_HEADS) * ATTN_HEAD_DIM], axis=-1)
    q = rms_norm(q.reshape(B, S, SWA_HEADS, ATTN_HEAD_DIM), q_norm) * ATTN_HEAD_DIM ** -0.5
    k = rms_norm(k.reshape(B, S, SWA_KV_HEADS, ATTN_HEAD_DIM), k_norm)
    v = v.reshape(B, S, SWA_KV_HEADS, ATTN_HEAD_DIM)
    pad = ((0, 0), (SWA_WINDOW, SWA_WINDOW), (0, 0), (0, 0))
    k_pad, v_pad = jnp.pad(k, pad), jnp.pad(v, pad)
    q_blocks = jnp.moveaxis(q.reshape(B, nb, Q_BLOCK, SWA_KV_HEADS, SWA_GROUP, ATTN_HEAD_DIM), 1, 0)
    rel = jnp.arange(span, dtype=jnp.int32)[None, :] - SWA_WINDOW - jnp.arange(Q_BLOCK, dtype=jnp.int32)[:, None]
    bias = rel_bias_heads(rel_table, rel).reshape(SWA_KV_HEADS, SWA_GROUP, Q_BLOCK, span)
    in_window = jnp.abs(rel) <= SWA_WINDOW
    sink_l = sink.astype(jnp.float32).reshape(1, SWA_KV_HEADS, SWA_GROUP, 1, 1)

    def block(args):
        qb, n = args
        start = n * Q_BLOCK
        kb = lax.dynamic_slice_in_dim(k_pad, start, span, axis=1)
        vb = lax.dynamic_slice_in_dim(v_pad, start, span, axis=1)
        key_pos = start - SWA_WINDOW + jnp.arange(span, dtype=jnp.int32)
        valid = in_window & (key_pos >= 0)[None, :] & (key_pos < S)[None, :]
        s = jnp.einsum('bqkgd,bjkd->bkgqj', qb, kb).astype(jnp.float32) + bias[None]
        s = jnp.where(valid, s, -jnp.inf)
        m = jnp.maximum(jnp.max(s, axis=-1, keepdims=True), sink_l)
        p = jnp.exp(s - m)
        p = p / (jnp.sum(p, axis=-1, keepdims=True) + jnp.exp(sink_l - m))
        return jnp.einsum('bkgqj,bjkd->bqkgd', p.astype(vb.dtype), vb)

    o = lax.map(block, (q_blocks, jnp.arange(nb, dtype=jnp.int32)))
    o = jnp.moveaxis(o, 0, 1).reshape(B, S, SWA_HEADS * ATTN_HEAD_DIM)
    return o @ w_out


def ec_moe(h, router, w1, w3, w2):
    B, S, D = h.shape
    cap = EC_CAPACITY_FACTOR * S // N_EXPERTS
    aff = jax.nn.softmax((h @ router).astype(jnp.float32), axis=-1)
    gate, idx = lax.top_k(jnp.swapaxes(aff, 1, 2), cap)
    xs = jax.vmap(lambda hb, ib: hb[ib])(h, idx)
    a = jnp.einsum('becd,edf->becf', xs, w1)
    b = jnp.einsum('becd,edf->becf', xs, w3)
    y = jnp.einsum('becf,efd->becd', jax.nn.silu(a) * b, w2) * gate[..., None].astype(h.dtype)
    out = jax.vmap(lambda yb, ib: jnp.zeros((S, D), yb.dtype).at[ib.reshape(-1)].add(yb.reshape(-1, D)))(y, idx)
    return out.astype(h.dtype)


def setup_inputs(seed: int = 0) -> dict:
    key = jax.random.key(seed)
    keys = jax.random.split(key, 32)
    kit = (keys[i] for i in range(32))
    D = D_MODEL

    def nrm(shape, scale):
        return jax.random.normal(next(kit), shape, jnp.float32) * scale

    def gain(shape):
        return 1.0 + 0.02 * jax.random.normal(next(kit), shape, jnp.float32)

    x = nrm((BATCH, SEQ, D), 1.0)
    rel_bias = nrm((REL_BUCKETS, ATTN_HEADS), 0.5)
    norm_mix = gain((DEPTH, D))
    norm_ffn = gain((DEPTH, D))

    gla_w_in = nrm((N_GLA, D, GLA_IN), D ** -0.5)
    gla_w_gate_up = nrm((N_GLA, 2, GLA_GATE_RANK, GLA_QK_DIM), GLA_GATE_RANK ** -0.5)
    gla_b_gate = nrm((N_GLA, 2, GLA_QK_DIM), 0.1)
    gla_head_norm = gain((N_GLA, GLA_DV))
    gla_w_out = nrm((N_GLA, GLA_V_DIM, D), GLA_V_DIM ** -0.5)

    gdn_w_in = nrm((N_GDN, D, GDN_IN), D ** -0.5)
    gdn_conv = nrm((N_GDN, GDN_CONV, GDN_CONV_DIM), GDN_CONV ** -0.5)
    gdn_a_log = jnp.log(jax.random.uniform(next(kit), (N_GDN, 2, GDN_V_HEADS), jnp.float32, 1.0, 16.0))
    dt = jnp.exp(jax.random.uniform(next(kit), (N_GDN, 2, GDN_V_HEADS), jnp.float32,
                                    math.log(1e-3), math.log(1e-1)))
    gdn_dt_bias = dt + jnp.log(-jnp.expm1(-dt))
    gdn_head_norm = gain((N_GDN, GDN_HEAD_DIM))
    gdn_w_out = nrm((N_GDN, GDN_V_DIM, D), GDN_V_DIM ** -0.5)

    diff_w_in = nrm((N_DIFF, D, DIFF_IN), D ** -0.5)
    diff_q_norm = gain((N_DIFF, DIFF_DQK))
    diff_k_norm = gain((N_DIFF, DIFF_DQK))
    diff_lambda = nrm((N_DIFF, 4, DIFF_DQK), 0.1)
    diff_subln = gain((N_DIFF, DIFF_DV))
    diff_w_out = nrm((N_DIFF, DIFF_HEADS * DIFF_DV, D), (DIFF_HEADS * DIFF_DV) ** -0.5)

    swa_w_in = nrm((N_SWA, D, SWA_IN), D ** -0.5)
    swa_q_norm = gain((N_SWA, ATTN_HEAD_DIM))
    swa_k_norm = gain((N_SWA, ATTN_HEAD_DIM))
    swa_sink = nrm((N_SWA, SWA_HEADS), 0.5)
    swa_w_out = nrm((N_SWA, SWA_HEADS * ATTN_HEAD_DIM, D), (SWA_HEADS * ATTN_HEAD_DIM) ** -0.5)

    moe_router = nrm((DEPTH, D, N_EXPERTS), D ** -0.5)
    moe_w1 = nrm((DEPTH, N_EXPERTS, D, EXPERT_DFF), D ** -0.5)
    moe_w3 = nrm((DEPTH, N_EXPERTS, D, EXPERT_DFF), D ** -0.5)
    moe_w2 = nrm((DEPTH, N_EXPERTS, EXPERT_DFF, D), EXPERT_DFF ** -0.5)

    return {"x": x, "rel_bias": rel_bias, "norm_mix": norm_mix, "norm_ffn": norm_ffn,
            "gla_w_in": gla_w_in, "gla_w_gate_up": gla_w_gate_up, "gla_b_gate": gla_b_gate,
            "gla_head_norm": gla_head_norm, "gla_w_out": gla_w_out,
            "gdn_w_in": gdn_w_in, "gdn_conv": gdn_conv, "gdn_a_log": gdn_a_log, "gdn_dt_bias": gdn_dt_bias,
            "gdn_head_norm": gdn_head_norm, "gdn_w_out": gdn_w_out,
            "diff_w_in": diff_w_in, "diff_q_norm": diff_q_norm, "diff_k_norm": diff_k_norm,
            "diff_lambda": diff_lambda, "diff_subln": diff_subln, "diff_w_out": diff_w_out,
            "swa_w_in": swa_w_in, "swa_q_norm": swa_q_norm, "swa_k_norm": swa_k_norm,
            "swa_sink": swa_sink, "swa_w_out": swa_w_out,
            "moe_router": moe_router, "moe_w1": moe_w1, "moe_w3": moe_w3, "moe_w2": moe_w2}


def reference(x, rel_bias, norm_mix, norm_ffn,
              gla_w_in, gla_w_gate_up, gla_b_gate, gla_head_norm, gla_w_out,
              gdn_w_in, gdn_conv, gdn_a_log, gdn_dt_bias, gdn_head_norm, gdn_w_out,
              diff_w_in, diff_q_norm, diff_k_norm, diff_lambda, diff_subln, diff_w_out,
              swa_w_in, swa_q_norm, swa_k_norm, swa_sink, swa_w_out,
              moe_router, moe_w1, moe_w3, moe_w2):
    h = x
    for i in range(DEPTH):
        m, j = i % N_MIXERS, i // N_MIXERS
        hn = rms_norm(h, norm_mix[i])
        if m == 0:
            mix = gla_mixer(hn, gla_w_in[j], gla_w_gate_up[j], gla_b_gate[j], gla_head_norm[j], gla_w_out[j])
        elif m == 1:
            mix = gdn_mixer(hn, gdn_w_in[j], gdn_conv[j], gdn_a_log[j], gdn_dt_bias[j], gdn_head_norm[j], gdn_w_out[j])
        elif m == 2:
            mix = diff_mixer(hn, diff_w_in[j], diff_q_norm[j], diff_k_norm[j], diff_lambda[j], diff_subln[j],
                             diff_w_out[j], rel_bias, i)
        else:
            mix = swa_mixer(hn, swa_w_in[j], swa_q_norm[j], swa_k_norm[j], swa_sink[j], swa_w_out[j], rel_bias)
        h = h + mix.astype(h.dtype)
        h = h + ec_moe(rms_norm(h, norm_ffn[i]), moe_router[i], moe_w1[i], moe_w3[i], moe_w2[i])
    return h
```

```python
import functools
import math

import jax
import jax.numpy as jnp
from jax import lax
from jax.experimental import pallas as pl
from jax.experimental.pallas import tpu as pltpu

F32 = jnp.float32
BF16 = jnp.bfloat16
I32 = jnp.int32
HI = lax.Precision.HIGHEST

RMS_EPS = 1e-6
VMEM_LIMIT_BYTES = 56 * 1024 * 1024
LANES = 128

REL_BUCKETS = 32
REL_MAX_DIST = 128
CHUNK = 64
GLA_HEADS, GLA_DK, GLA_DV, GLA_RANK, GLA_TAU = 4, 256, 512, 16, 16.0
GDN_QK_HEADS, GDN_V_HEADS, GDN_HD, GDN_CONV = 16, 32, 128, 5
DIFF_HEADS, DIFF_DQK, DIFF_DV, DIFF_QB = 16, 64, 128, 256
SWA_HEADS, SWA_KV, SWA_GROUP, SWA_HD, SWA_W, SWA_QB = 16, 4, 4, 128, 128, 128
N_EXPERTS = 16
EC_CAPACITY_FACTOR = 2


def _params(sem):
    return pltpu.CompilerParams(dimension_semantics=sem, vmem_limit_bytes=VMEM_LIMIT_BYTES)


def _dot(a, b, **kw):
    return jnp.dot(a, b, preferred_element_type=F32, **kw)


def _dot_nt(a, b, **kw):
    return lax.dot_general(a, b, (((1,), (1,)), ((), ())), preferred_element_type=F32, **kw)


def _dot_tn(a, b, **kw):
    return lax.dot_general(a, b, (((0,), (0,)), ((), ())), preferred_element_type=F32, **kw)


def _rms(x):
    return x * lax.rsqrt(jnp.mean(x * x, axis=-1, keepdims=True) + RMS_EPS)


def _sigmoid(x):
    return 1.0 / (1.0 + jnp.exp(-x))


def _softplus(x):
    return jnp.maximum(x, 0.0) + jnp.log(1.0 + jnp.exp(-jnp.abs(x)))


def _norm_matmul_kernel(x_ref, g_ref, w_ref, o_ref, xn_ref):
    @pl.when(pl.program_id(1) == 0)
    def _():
        xn_ref[...] = (_rms(x_ref[...]) * g_ref[...]).astype(BF16)

    o_ref[...] = _dot(xn_ref[...], w_ref[...])


def norm_matmul(x, gain, w, *, tm=1024, tn=512, name):
    T, D = x.shape
    N = w.shape[1]
    tm, tn = min(tm, T), min(tn, N)
    assert T % tm == 0 and N % tn == 0
    return pl.pallas_call(
        _norm_matmul_kernel,
        grid=(T // tm, N // tn),
        in_specs=[pl.BlockSpec((tm, D), lambda i, j: (i, 0)),
                  pl.BlockSpec((1, D), lambda i, j: (0, 0)),
                  pl.BlockSpec((D, tn), lambda i, j: (0, j))],
        out_specs=pl.BlockSpec((tm, tn), lambda i, j: (i, j)),
        out_shape=jax.ShapeDtypeStruct((T, N), F32),
        scratch_shapes=[pltpu.VMEM((tm, D), BF16)],
        compiler_params=_params(("parallel", "arbitrary")),
        name=name,
    )(x, gain.reshape(1, D), w)


def _matmul_res_kernel(y_ref, w_ref, h_ref, o_ref):
    o_ref[...] = h_ref[...] + _dot(y_ref[...].astype(BF16), w_ref[...])


def matmul_residual(y, w, h, *, tm=512, tn=512, name):
    T, K = y.shape
    N = w.shape[1]
    tm, tn = min(tm, T), min(tn, N)
    assert T % tm == 0 and N % tn == 0
    return pl.pallas_call(
        _matmul_res_kernel,
        grid=(T // tm, N // tn),
        in_specs=[pl.BlockSpec((tm, K), lambda i, j: (i, 0)),
                  pl.BlockSpec((K, tn), lambda i, j: (0, j)),
                  pl.BlockSpec((tm, tn), lambda i, j: (i, j))],
        out_specs=pl.BlockSpec((tm, tn), lambda i, j: (i, j)),
        out_shape=jax.ShapeDtypeStruct((T, N), F32),
        compiler_params=_params(("parallel", "parallel")),
        name=name,
    )(y, w, h)


def _tri_masks(bwd):
    row = lax.broadcasted_iota(I32, (CHUNK, CHUNK), 0)
    col = lax.broadcasted_iota(I32, (CHUNK, CHUNK), 1)
    incl = jnp.where(bwd, (col >= row).astype(F32), (col <= row).astype(F32))
    strict = jnp.where(bwd, (col > row).astype(F32), (col < row).astype(F32))
    return incl, strict


def _gla_kernel(q_ref, k_ref, v_ref, r_ref, glo_ref, wg_ref, bg_ref, hn_ref, o_ref,
                la_ref, of_ref, st_ref, *, nb, blk):
    i = pl.program_id(2)
    bwd = i >= nb
    sb = jnp.where(bwd, 2 * nb - 1 - i, i)
    nc = blk // CHUNK

    @pl.when((i == 0) | (i == nb))
    def _():
        st_ref[...] = jnp.zeros_like(st_ref)

    lane = lax.broadcasted_iota(I32, (blk, LANES), 1)
    lo = jnp.where(bwd, GLA_RANK, 0)
    gsel = jnp.where((lane >= lo) & (lane < lo + GLA_RANK), glo_ref[...], 0.0)
    bg = bg_ref[...]
    gate = _dot(gsel.astype(BF16), wg_ref[...]) + jnp.where(bwd, bg[1:2], bg[0:1])
    la_ref[...] = (jnp.minimum(gate, 0.0) - jnp.log(1.0 + jnp.exp(-jnp.abs(gate)))) * (1.0 / GLA_TAU)

    incl, _ = _tri_masks(bwd)
    gain = hn_ref[...]

    def body(c, carry):
        cc = jnp.where(bwd, nc - 1 - c, c)
        r0 = pl.multiple_of(cc * CHUNK, CHUNK)
        rows = pl.ds(r0, CHUNK)
        la = la_ref[rows, :]
        cum = _dot(incl, la, precision=HI)
        tot = jnp.sum(la, axis=0, keepdims=True)
        q = q_ref[rows, :] * (GLA_DK ** -0.5)
        k = k_ref[rows, :]
        v = v_ref[rows, :].astype(BF16)
        qd = (q * jnp.exp(cum)).astype(BF16)
        kin = (k * jnp.exp(-cum)).astype(BF16)
        kst = (k * jnp.exp(tot - cum)).astype(BF16)
        s = _dot_nt(qd, kin) * incl
        st = st_ref[...]
        o = _dot(s.astype(BF16), v) + _dot_nt(qd, st.astype(BF16))
        st_ref[...] = st * jnp.exp(tot) + _dot_tn(v, kst)
        grow = pl.ds(pl.multiple_of(sb * blk + r0, CHUNK), CHUNK)

        @pl.when(jnp.logical_not(bwd))
        def _():
            of_ref[grow, :] = o

        @pl.when(bwd)
        def _():
            ot = of_ref[grow, :] + o
            r = r_ref[rows, :]
            o_ref[rows, :] = _rms(ot) * gain * (r * _sigmoid(r))

        return carry

    lax.fori_loop(0, nc, body, 0)


def gla_scan(proj, glo, wg, bg, head_norm, *, B, S, blk=512):
    T = B * S
    blk = min(blk, S)
    nb = S // blk
    H = GLA_HEADS

    def rowblk(b, i):
        return b * nb + jnp.where(i >= nb, 2 * nb - 1 - i, i)

    def outblk(b, i):
        return b * nb + jnp.where(i >= nb, 2 * nb - 1 - i, nb - 1)

    kq = GLA_HEADS * GLA_DK // GLA_DK
    kv = 2 * GLA_HEADS * GLA_DK // GLA_DV
    kr = kv + GLA_HEADS
    return pl.pallas_call(
        functools.partial(_gla_kernel, nb=nb, blk=blk),
        grid=(B, H, 2 * nb),
        in_specs=[pl.BlockSpec((blk, GLA_DK), lambda b, h, i: (rowblk(b, i), h)),
                  pl.BlockSpec((blk, GLA_DK), lambda b, h, i: (rowblk(b, i), kq + h)),
                  pl.BlockSpec((blk, GLA_DV), lambda b, h, i: (rowblk(b, i), kv + h)),
                  pl.BlockSpec((blk, GLA_DV), lambda b, h, i: (outblk(b, i), kr + h)),
                  pl.BlockSpec((blk, LANES), lambda b, h, i: (rowblk(b, i), 0)),
                  pl.BlockSpec((LANES, GLA_DK), lambda b, h, i: (0, h)),
                  pl.BlockSpec((2, GLA_DK), lambda b, h, i: (0, h)),
                  pl.BlockSpec((1, GLA_DV), lambda b, h, i: (0, 0))],
        out_specs=pl.BlockSpec((blk, GLA_DV), lambda b, h, i: (outblk(b, i), h)),
        out_shape=jax.ShapeDtypeStruct((T, H * GLA_DV), F32),
        scratch_shapes=[pltpu.VMEM((blk, GLA_DK), F32),
                        pltpu.VMEM((S, GLA_DV), F32),
                        pltpu.VMEM((GLA_DV, GLA_DK), F32)],
        compiler_params=_params(("parallel", "parallel", "arbitrary")),
        name="gla_scan",
    )(proj, proj, proj, proj, glo, wg, bg, head_norm.reshape(1, GLA_DV))


def _gdn_conv_kernel(x_ref, w_ref, o_ref, xp_ref, *, S, rows):
    c = pl.program_id(1)
    pad = 8
    xp_ref[0:pad, :] = jnp.zeros((pad, LANES), F32)
    xp_ref[pad + S:2 * pad + S, :] = jnp.zeros((pad, LANES), F32)
    xp_ref[pad:pad + S, :] = x_ref[...]
    w = w_ref[...]
    win = rows + 2 * pad
    is_qk = c < 2 * GDN_QK_HEADS
    scale = jnp.where(c < GDN_QK_HEADS, GDN_HD ** -0.5, 1.0)

    def body(t, carry):
        r0 = pl.multiple_of(t * rows, rows)
        xw = xp_ref[pl.ds(r0, win), :]
        acc = jnp.zeros((rows, LANES), F32)
        for j in range(GDN_CONV):
            sh = (GDN_CONV // 2 - j) % win
            xs = xw if sh == 0 else pltpu.roll(xw, sh, 0)
            acc = acc + xs[pad:pad + rows, :] * w[j:j + 1, :]
        y = acc * _sigmoid(acc)
        yn = y * lax.rsqrt(jnp.sum(y * y, axis=-1, keepdims=True) + RMS_EPS) * scale
        o_ref[pl.ds(r0, rows), :] = jnp.where(is_qk, yn, y)
        return carry

    lax.fori_loop(0, S // rows, body, 0)


def gdn_conv(proj, conv_w, *, B, S):
    T = B * S
    nch = conv_w.shape[1] // LANES
    rows = min(256, S)
    return pl.pallas_call(
        functools.partial(_gdn_conv_kernel, S=S, rows=rows),
        grid=(B, nch),
        in_specs=[pl.BlockSpec((S, LANES), lambda b, c: (b, c)),
                  pl.BlockSpec((GDN_CONV, LANES), lambda b, c: (0, c))],
        out_specs=pl.BlockSpec((S, LANES), lambda b, c: (b, c)),
        out_shape=jax.ShapeDtypeStruct((T, nch * LANES), F32),
        scratch_shapes=[pltpu.VMEM((S + 16, LANES), F32)],
        compiler_params=_params(("parallel", "parallel")),
        name="gdn_conv",
    )(proj, conv_w)


def _mm_bf16(a, b):
    return _dot(a.astype(BF16), b.astype(BF16))


def _unit_tri_inverse(L):
    row = lax.broadcasted_iota(I32, (CHUNK, CHUNK), 0)
    col = lax.broadcasted_iota(I32, (CHUNK, CHUNK), 1)
    p = (row == col).astype(F32) - L
    pw = _mm_bf16(L, L)
    n = 2
    while True:
        p = p + _mm_bf16(p, pw)
        n *= 2
        if n >= CHUNK:
            return p
        pw = _mm_bf16(pw, pw)


def _gdn_kernel(q_ref, k_ref, v_ref, z_ref, ab_ref, alog_ref, dtb_ref, hn_ref, o_ref,
                gb_ref, bb_ref, of_ref, st_ref, *, nb, blk):
    hv = pl.program_id(1)
    i = pl.program_id(2)
    bwd = i >= nb
    sb = jnp.where(bwd, 2 * nb - 1 - i, i)
    nc = blk // CHUNK

    @pl.when((i == 0) | (i == nb))
    def _():
        st_ref[...] = jnp.zeros_like(st_ref)

    x = ab_ref[...]
    gfull = -jnp.exp(alog_ref[...]) * _softplus(x + dtb_ref[...])
    bfull = _sigmoid(x)
    d = jnp.where(bwd, GDN_V_HEADS, 0)
    rowi = lax.broadcasted_iota(I32, (LANES, LANES), 0)
    gb_ref[...] = _dot(gfull, (rowi == d + hv).astype(F32), precision=HI)
    bb_ref[...] = _dot(bfull, (rowi == 2 * GDN_V_HEADS + d + hv).astype(F32), precision=HI)

    incl, strict = _tri_masks(bwd)
    sd = strict
    gain = hn_ref[...]

    def body(c, carry):
        cc = jnp.where(bwd, nc - 1 - c, c)
        r0 = pl.multiple_of(cc * CHUNK, CHUNK)
        rows = pl.ds(r0, CHUNK)
        gb = gb_ref[rows, :]
        bb = bb_ref[rows, :]
        gcb = _dot(incl, gb, precision=HI)
        dd = _dot(incl, gb[:, :CHUNK] * sd, precision=HI)
        gamma = jnp.where(incl > 0, jnp.exp(dd), 0.0)
        tot = jnp.sum(gb, axis=0, keepdims=True)
        q = q_ref[rows, :]
        k = k_ref[rows, :]
        v = v_ref[rows, :]
        kb = k * bb
        kbf = k.astype(BF16)
        L = jnp.where(strict > 0, _dot_nt(kb.astype(BF16), kbf) * gamma, 0.0)
        tinv = _unit_tri_inverse(L)
        egc = jnp.exp(gcb)
        rhs = jnp.concatenate([v * bb, kb * egc], axis=1)
        uw = _mm_bf16(tinv, rhs)
        u = uw[:, :GDN_HD]
        w = uw[:, GDN_HD:]
        qk = _dot_nt(q.astype(BF16), kbf) * gamma
        qd = (q * egc).astype(BF16)
        kst = (k * jnp.exp(tot - gcb)).astype(BF16)
        st = st_ref[...]
        stb = st.astype(BF16)
        vnew = u - _dot(w.astype(BF16), stb)
        vnb = vnew.astype(BF16)
        o = _dot(qd, stb) + _dot(qk.astype(BF16), vnb)
        st_ref[...] = st * jnp.exp(tot) + _dot_tn(kst, vnb)
        grow = pl.ds(pl.multiple_of(sb * blk + r0, CHUNK), CHUNK)

        @pl.when(jnp.logical_not(bwd))
        def _():
            of_ref[grow, :] = o

        @pl.when(bwd)
        def _():
            ot = of_ref[grow, :] + o
            z = z_ref[rows, :]
            o_ref[rows, :] = _rms(ot) * gain * (z * _sigmoid(z))

        return carry

    lax.fori_loop(0, nc, body, 0)


def gdn_scan(qkv, proj, ab, alog_row, dtb_row, head_norm, *, B, S, blk=512):
    T = B * S
    blk = min(blk, S)
    nb = S // blk
    HV = GDN_V_HEADS
    rep = GDN_V_HEADS // GDN_QK_HEADS
    zoff = (2 * GDN_QK_HEADS + GDN_V_HEADS)

    def rowblk(b, i):
        return b * nb + jnp.where(i >= nb, 2 * nb - 1 - i, i)

    def outblk(b, i):
        return b * nb + jnp.where(i >= nb, 2 * nb - 1 - i, nb - 1)

    return pl.pallas_call(
        functools.partial(_gdn_kernel, nb=nb, blk=blk),
        grid=(B, HV, 2 * nb),
        in_specs=[pl.BlockSpec((blk, GDN_HD), lambda b, h, i: (rowblk(b, i), h // rep)),
                  pl.BlockSpec((blk, GDN_HD), lambda b, h, i: (rowblk(b, i), GDN_QK_HEADS + h // rep)),
                  pl.BlockSpec((blk, GDN_HD), lambda b, h, i: (rowblk(b, i), 2 * GDN_QK_HEADS + h)),
                  pl.BlockSpec((blk, GDN_HD), lambda b, h, i: (outblk(b, i), zoff + h)),
                  pl.BlockSpec((blk, LANES), lambda b, h, i: (rowblk(b, i), 0)),
                  pl.BlockSpec((1, LANES), lambda b, h, i: (0, 0)),
                  pl.BlockSpec((1, LANES), lambda b, h, i: (0, 0)),
                  pl.BlockSpec((1, GDN_HD), lambda b, h, i: (0, 0))],
        out_specs=pl.BlockSpec((blk, GDN_HD), lambda b, h, i: (outblk(b, i), h)),
        out_shape=jax.ShapeDtypeStruct((T, HV * GDN_HD), F32),
        scratch_shapes=[pltpu.VMEM((blk, LANES), F32),
                        pltpu.VMEM((blk, LANES), F32),
                        pltpu.VMEM((S, GDN_HD), F32),
                        pltpu.VMEM((GDN_HD, GDN_HD), F32)],
        compiler_params=_params(("parallel", "parallel", "arbitrary")),
        name="gdn_scan",
    )(qkv, qkv, qkv, proj, ab, alog_row, dtb_row, head_norm.reshape(1, GDN_HD))


def _t5_bucket(rel):
    half = REL_BUCKETS // 2
    max_exact = half // 2
    n = jnp.abs(rel)
    log_ratio = jnp.log(jnp.maximum(n, 1).astype(F32) / max_exact) / math.log(REL_MAX_DIST / max_exact)
    large = jnp.minimum(max_exact + (log_ratio * (half - max_exact)).astype(I32), half - 1)
    return jnp.where(rel > 0, half, 0) + jnp.where(n < max_exact, n, large)


def _rel_bias_heads(table, rel):
    return jnp.moveaxis(table[_t5_bucket(rel)].astype(F32), -1, 0)


def _half_rms(x, ones_bd):
    ms = _dot(x * x, ones_bd, precision=HI) * (1.0 / DIFF_DQK)
    return x * lax.rsqrt(ms + RMS_EPS)


def _diff_kernel(q_ref, k_ref, v_ref, qn_ref, kn_ref, lam_ref, sub_ref, bias_ref, o_ref,
                 kb_ref, s_ref, *, nk, lambda_init):
    i = pl.program_id(2)
    qb = DIFF_QB
    r = lax.broadcasted_iota(I32, (LANES, LANES), 0) // DIFF_DQK
    c = lax.broadcasted_iota(I32, (LANES, LANES), 1) // DIFF_DQK
    ones_bd = (r == c).astype(F32)

    @pl.when(i == 0)
    def _():
        def kbody(t, carry):
            rows = pl.ds(pl.multiple_of(t * qb, qb), qb)
            kb_ref[rows, :] = (_half_rms(k_ref[rows, :], ones_bd) * kn_ref[...]).astype(BF16)
            return carry
        lax.fori_loop(0, nk, kbody, 0)

    q = _half_rms(q_ref[...], ones_bd) * qn_ref[...] * (DIFF_DQK ** -0.5)
    lane = lax.broadcasted_iota(I32, (qb, LANES), 1)
    q0 = jnp.where(lane < DIFF_DQK, q, 0.0).astype(BF16)
    q1 = jnp.where(lane >= DIFF_DQK, q, 0.0).astype(BF16)

    lam = lam_ref[...]
    lam_full = (jnp.exp(jnp.sum(lam[0:1] * lam[1:2], axis=-1, keepdims=True))
                - jnp.exp(jnp.sum(lam[2:3] * lam[3:4], axis=-1, keepdims=True)) + lambda_init)

    neg = jnp.full((qb, 1), -jnp.inf, F32)

    def pass_a(t, carry):
        m0, m1 = carry
        kc = kb_ref[pl.ds(pl.multiple_of(t * qb, qb), qb), :]
        bias = bias_ref[0, jnp.clip(t - i, -2, 2) + 2]
        s0 = _dot_nt(q0, kc) + bias
        s1 = _dot_nt(q1, kc) + bias
        s_ref[0, t] = s0
        s_ref[1, t] = s1
        return (jnp.maximum(m0, jnp.max(s0, axis=-1, keepdims=True)),
                jnp.maximum(m1, jnp.max(s1, axis=-1, keepdims=True)))

    m0, m1 = lax.fori_loop(0, nk, pass_a, (neg, neg))

    zero = jnp.zeros((qb, 1), F32)

    def pass_b(t, carry):
        l0, l1 = carry
        p0 = jnp.exp(s_ref[0, t] - m0)
        p1 = jnp.exp(s_ref[1, t] - m1)
        s_ref[0, t] = p0
        s_ref[1, t] = p1
        return (l0 + jnp.sum(p0, axis=-1, keepdims=True), l1 + jnp.sum(p1, axis=-1, keepdims=True))

    l0, l1 = lax.fori_loop(0, nk, pass_b, (zero, zero))
    r0 = 1.0 / l0
    r1 = lam_full / l1

    def pass_c(t, acc):
        a = (s_ref[0, t] * r0 - s_ref[1, t] * r1).astype(BF16)
        vc = v_ref[pl.ds(pl.multiple_of(t * qb, qb), qb), :].astype(BF16)
        return acc + _dot(a, vc)

    o = lax.fori_loop(0, nk, pass_c, jnp.zeros((qb, DIFF_DV), F32))
    o_ref[...] = _rms(o) * sub_ref[...] * (1.0 - lambda_init)


def diff_attention(proj, q_norm, k_norm, lam, subln, bias_tiles, *, B, S, layer_idx):
    T = B * S
    qb = DIFF_QB
    nq = S // qb
    H = DIFF_HEADS
    lambda_init = 0.8 - 0.6 * math.exp(-0.3 * layer_idx)
    qn2 = jnp.concatenate([q_norm, q_norm]).reshape(1, LANES)
    kn2 = jnp.concatenate([k_norm, k_norm]).reshape(1, LANES)
    return pl.pallas_call(
        functools.partial(_diff_kernel, nk=nq, lambda_init=lambda_init),
        grid=(B, H, nq),
        in_specs=[pl.BlockSpec((qb, LANES), lambda b, h, i: (b * nq + i, h)),
                  pl.BlockSpec((S, LANES), lambda b, h, i: (b, H + h)),
                  pl.BlockSpec((S, LANES), lambda b, h, i: (b, 2 * H + h)),
                  pl.BlockSpec((1, LANES), lambda b, h, i: (0, 0)),
                  pl.BlockSpec((1, LANES), lambda b, h, i: (0, 0)),
                  pl.BlockSpec((4, DIFF_DQK), lambda b, h, i: (0, 0)),
                  pl.BlockSpec((1, DIFF_DV), lambda b, h, i: (0, 0)),
                  pl.BlockSpec((1, 5, qb, qb), lambda b, h, i: (h, 0, 0, 0))],
        out_specs=pl.BlockSpec((qb, DIFF_DV), lambda b, h, i: (b * nq + i, h)),
        out_shape=jax.ShapeDtypeStruct((T, H * DIFF_DV), F32),
        scratch_shapes=[pltpu.VMEM((S, LANES), BF16),
                        pltpu.VMEM((2, nq, qb, qb), F32)],
        compiler_params=_params(("parallel", "parallel", "arbitrary")),
        name="diff_attn",
    )(proj, proj, proj, qn2, kn2, lam, subln.reshape(1, DIFF_DV), bias_tiles)


def diff_bias_tiles(table):
    qb = DIFF_QB
    ii = jnp.arange(qb, dtype=I32)[:, None]
    jj = jnp.arange(qb, dtype=I32)[None, :]
    d = jnp.arange(-2, 3, dtype=I32)[:, None, None]
    rel = d * qb + jj[None] - ii[None]
    return _rel_bias_heads(table, rel)


def _swa_kernel(q_ref, k0_ref, k1_ref, k2_ref, v0_ref, v1_ref, v2_ref, qn_ref, kn_ref, sink_ref,
                bias_ref, o_ref, *, nq, S):
    i = pl.program_id(2)
    qb = SWA_QB
    span = 3 * qb
    kcat = jnp.concatenate([k0_ref[...], k1_ref[...], k2_ref[...]], axis=0)
    kcat = (_rms(kcat) * kn_ref[...]).astype(BF16)
    vcat = jnp.concatenate([v0_ref[...], v1_ref[...], v2_ref[...]], axis=0).astype(BF16)
    row = lax.broadcasted_iota(I32, (qb, span), 0)
    col = lax.broadcasted_iota(I32, (qb, span), 1)
    rel = col - SWA_W - row
    key_pos = i * qb - SWA_W + col
    valid = (jnp.abs(rel) <= SWA_W) & (key_pos >= 0) & (key_pos < S)
    sink_all = sink_ref[0]
    for g in range(SWA_GROUP):
        q = q_ref[:, g * SWA_HD:(g + 1) * SWA_HD]
        q = (_rms(q) * qn_ref[...] * (SWA_HD ** -0.5)).astype(BF16)
        s = _dot_nt(q, kcat) + bias_ref[g]
        s = jnp.where(valid, s, -jnp.inf)
        sink = sink_all[g:g + 1, 0:1]
        m = jnp.maximum(jnp.max(s, axis=-1, keepdims=True), sink)
        p = jnp.exp(s - m)
        p = p / (jnp.sum(p, axis=-1, keepdims=True) + jnp.exp(sink - m))
        o_ref[:, g * SWA_HD:(g + 1) * SWA_HD] = _dot(p.astype(BF16), vcat)


def swa_attention(proj, q_norm, k_norm, sink, bias, *, B, S):
    T = B * S
    qb = SWA_QB
    nq = S // qb
    koff = SWA_HEADS
    voff = SWA_HEADS + SWA_KV
    gw = SWA_GROUP * SWA_HD

    def kvspec(off, d):
        return pl.BlockSpec((qb, SWA_HD),
                            lambda b, kv, i: (b * nq + jnp.clip(i + d, 0, nq - 1), off + kv))

    sink_b = jnp.broadcast_to(sink.astype(F32).reshape(SWA_KV, SWA_GROUP, 1), (SWA_KV, SWA_GROUP, LANES))
    return pl.pallas_call(
        functools.partial(_swa_kernel, nq=nq, S=S),
        grid=(B, SWA_KV, nq),
        in_specs=[pl.BlockSpec((qb, gw), lambda b, kv, i: (b * nq + i, kv)),
                  kvspec(koff, -1), kvspec(koff, 0), kvspec(koff, 1),
                  kvspec(voff, -1), kvspec(voff, 0), kvspec(voff, 1),
                  pl.BlockSpec((1, SWA_HD), lambda b, kv, i: (0, 0)),
                  pl.BlockSpec((1, SWA_HD), lambda b, kv, i: (0, 0)),
                  pl.BlockSpec((1, SWA_GROUP, LANES), lambda b, kv, i: (kv, 0, 0)),
                  pl.BlockSpec((SWA_GROUP, qb, 3 * qb), lambda b, kv, i: (kv, 0, 0))],
        out_specs=pl.BlockSpec((qb, gw), lambda b, kv, i: (b * nq + i, kv)),
        out_shape=jax.ShapeDtypeStruct((T, SWA_HEADS * SWA_HD), F32),
        compiler_params=_params(("parallel", "parallel", "parallel")),
        name="swa_attn",
    )(proj, proj, proj, proj, proj, proj, proj, q_norm.reshape(1, SWA_HD), k_norm.reshape(1, SWA_HD),
      sink_b, bias)


def swa_bias(table):
    qb = SWA_QB
    rel = jnp.arange(3 * qb, dtype=I32)[None, :] - SWA_W - jnp.arange(qb, dtype=I32)[:, None]
    return _rel_bias_heads(table, rel)


def _router_kernel(h_ref, g_ref, wr_ref, hx_ref, at_ref, *, D):
    hn = _rms(h_ref[...]) * g_ref[...]
    logits = _dot(hn, wr_ref[...], precision=HI)
    lane = lax.broadcasted_iota(I32, logits.shape, 1)
    logits = jnp.where(lane < N_EXPERTS, logits, -jnp.inf)
    m = jnp.max(logits, axis=-1, keepdims=True)
    e = jnp.exp(logits - m)
    aff = e / jnp.sum(e, axis=-1, keepdims=True)
    hx_ref[:, :D] = hn
    hx_ref[:, D:] = aff
    at_ref[0] = jnp.transpose(aff)


def moe_router(h, gain, router, *, B, S, tm=512):
    T, D = h.shape
    tm = min(tm, S)
    ns = S // tm
    wr = jnp.pad(router.astype(F32), ((0, 0), (0, LANES - N_EXPERTS)))
    return pl.pallas_call(
        functools.partial(_router_kernel, D=D),
        grid=(B, ns),
        in_specs=[pl.BlockSpec((tm, D), lambda b, s: (b * ns + s, 0)),
                  pl.BlockSpec((1, D), lambda b, s: (0, 0)),
                  pl.BlockSpec((D, LANES), lambda b, s: (0, 0))],
        out_specs=[pl.BlockSpec((tm, D + LANES), lambda b, s: (b * ns + s, 0)),
                   pl.BlockSpec((1, LANES, tm), lambda b, s: (b, 0, s))],
        out_shape=[jax.ShapeDtypeStruct((T, D + LANES), F32),
                   jax.ShapeDtypeStruct((B, LANES, S), F32)],
        compiler_params=_params(("parallel", "parallel")),
        name="moe_router",
    )(h, gain.reshape(1, D), wr)


def _topk_kernel(aff_ref, idx_ref, pos_ref, *, S, cap):
    E = N_EXPERTS
    v = aff_ref[0]
    bits = pltpu.bitcast(v, I32)

    def search(_, carry):
        lo, hi = carry
        mid = lo + ((hi - lo) >> 1)
        cnt = jnp.sum((bits >= mid).astype(F32), axis=1, keepdims=True)
        ok = cnt >= cap
        return jnp.where(ok, mid, lo), jnp.where(ok, hi, mid)

    lo0 = jnp.zeros((E, 1), I32)
    hi0 = jnp.full((E, 1), 0x7F800001, I32)
    thr, _ = lax.fori_loop(0, 32, search, (lo0, hi0))
    gt = bits > thr
    eq = bits == thr
    need = cap - jnp.sum(gt.astype(F32), axis=1, keepdims=True)

    r = lax.broadcasted_iota(I32, (LANES, LANES), 0)
    c = lax.broadcasted_iota(I32, (LANES, LANES), 1)
    upper = (r < c).astype(BF16)
    run_e = jnp.zeros((E, 1), F32)
    run_s = jnp.zeros((E, 1), F32)
    sub = 512 // LANES
    for t in range(S // LANES):
        sl = slice(t * LANES, (t + 1) * LANES)
        eq_t = eq[:, sl].astype(F32)
        pe = _dot(eq_t.astype(BF16), upper) + run_e
        sel_t = jnp.where(gt[:, sl], 1.0, jnp.where(pe < need, eq_t, 0.0))
        ps = _dot(sel_t.astype(BF16), upper) + run_s
        pos_ref[t // sub, :, (t % sub) * LANES:(t % sub + 1) * LANES] = jnp.where(sel_t > 0, ps, -1.0)
        run_e = run_e + jnp.sum(eq_t, axis=1, keepdims=True)
        run_s = run_s + jnp.sum(sel_t, axis=1, keepdims=True)

    pi = lax.broadcasted_iota(I32, (cap, 512), 0).astype(F32)
    lane = lax.broadcasted_iota(I32, (8, 512), 1)
    rowv = lax.broadcasted_iota(I32, (8, 512), 0)

    def per_expert(e, carry):
        def per_chunk(t, acc):
            pos = pos_ref[t, pl.ds(e, 1), :]
            onehot = (pi == pos).astype(BF16)
            tok = t * 512 + lane
            vals = jnp.where(rowv == 0, tok >> 6, jnp.where(rowv == 1, tok & 63, 0)).astype(F32).astype(BF16)
            return acc + _dot_nt(vals, onehot)
        acc = lax.fori_loop(0, S // 512, per_chunk, jnp.zeros((8, cap), F32))
        idx_ref[0, pl.ds(e, 1), :] = (acc[0:1] * 64.0 + acc[1:2]).astype(I32)
        return carry

    lax.fori_loop(0, E, per_expert, 0)


def moe_topk(aff_t, *, B, S):
    cap = EC_CAPACITY_FACTOR * S // N_EXPERTS
    return pl.pallas_call(
        functools.partial(_topk_kernel, S=S, cap=cap),
        grid=(B,),
        in_specs=[pl.BlockSpec((1, N_EXPERTS, S), lambda b: (b, 0, 0))],
        out_specs=pl.BlockSpec((1, N_EXPERTS, cap), lambda b: (b, 0, 0)),
        out_shape=jax.ShapeDtypeStruct((B, N_EXPERTS, cap), I32),
        scratch_shapes=[pltpu.VMEM((S // 512, N_EXPERTS, 512), F32)],
        compiler_params=_params(("parallel",)),
        name="moe_topk",
    )(aff_t)


def _ffn_kernel(idx_ref, hx_hbm, h_in, w1_ref, w3_ref, w2_ref, h_out, xbuf, acc, sem, *, S, D, cap):
    del h_in
    e = pl.program_id(0)
    b = pl.program_id(1)
    base = b * S

    def token_row(r):
        return pl.ds(base + idx_ref[0, 0, r], 1)

    def gather_x(r):
        return pltpu.make_async_copy(hx_hbm.at[token_row(r), :], xbuf.at[pl.ds(r, 1), :], sem.at[0])

    def gather_h(r):
        return pltpu.make_async_copy(h_out.at[token_row(r), :], acc.at[pl.ds(r, 1), :], sem.at[1])

    def scatter_h(r):
        return pltpu.make_async_copy(acc.at[pl.ds(r, 1), :], h_out.at[token_row(r), :], sem.at[2])

    def issue_in(r, carry):
        gather_x(r).start()
        gather_h(r).start()
        return carry

    lax.fori_loop(0, cap, issue_in, 0)

    def wait_in(r, carry):
        gather_x(r).wait()
        gather_h(r).wait()
        return carry

    lax.fori_loop(0, cap, wait_in, 0)

    x = xbuf[:, :D].astype(BF16)
    a = _dot(x, w1_ref[0])
    g = _dot(x, w3_ref[0])
    hm = (a * _sigmoid(a) * g).astype(BF16)
    y = _dot(hm, w2_ref[0])
    rowi = lax.broadcasted_iota(I32, (LANES, LANES), 0)
    gate = _dot(xbuf[:, D:], (rowi == e).astype(F32), precision=HI)
    acc[...] = acc[...] + y * gate[:, 0:1]

    def issue_out(r, carry):
        scatter_h(r).start()
        return carry

    lax.fori_loop(0, cap, issue_out, 0)

    def wait_out(r, carry):
        scatter_h(r).wait()
        return carry

    lax.fori_loop(0, cap, wait_out, 0)


def moe_ffn(idx, hx, h, w1, w3, w2, *, B, S):
    T, D = h.shape
    E = N_EXPERTS
    cap = idx.shape[-1]
    F = w1.shape[-1]
    idx3 = idx.reshape(B * E, 1, cap)
    return pl.pallas_call(
        functools.partial(_ffn_kernel, S=S, D=D, cap=cap),
        grid=(E, B),
        in_specs=[pl.BlockSpec((1, 1, cap), lambda e, b: (b * E + e, 0, 0), memory_space=pltpu.SMEM),
                  pl.BlockSpec(memory_space=pl.ANY),
                  pl.BlockSpec(memory_space=pl.ANY),
                  pl.BlockSpec((1, D, F), lambda e, b: (e, 0, 0)),
                  pl.BlockSpec((1, D, F), lambda e, b: (e, 0, 0)),
                  pl.BlockSpec((1, F, D), lambda e, b: (e, 0, 0))],
        out_specs=pl.BlockSpec(memory_space=pl.ANY),
        out_shape=jax.ShapeDtypeStruct((T, D), F32),
        scratch_shapes=[pltpu.VMEM((cap, D + LANES), F32),
                        pltpu.VMEM((cap, D), F32),
                        pltpu.SemaphoreType.DMA((3,))],
        input_output_aliases={2: 0},
        compiler_params=_params(("arbitrary", "arbitrary")),
        name="moe_ffn",
    )(idx3, hx, h, w1, w3, w2)


def ec_moe(h, gain, router, w1, w3, w2, *, B, S):
    hx, aff_t = moe_router(h, gain, router, B=B, S=S)
    idx = moe_topk(aff_t, B=B, S=S)
    return moe_ffn(idx, hx, h, w1.astype(BF16), w3.astype(BF16), w2.astype(BF16), B=B, S=S)


def _pad_cols(w, n):
    return jnp.pad(w, ((0, 0), (0, n - w.shape[1])))


def gla_layer(h, norm_gain, w_in, w_gate_up, b_gate, head_norm, w_out, *, B, S):
    nmain = 2 * GLA_HEADS * GLA_DK + 2 * GLA_HEADS * GLA_DV
    proj = norm_matmul(h, norm_gain, w_in[:, :nmain].astype(BF16), name="gla_in")
    glo = norm_matmul(h, norm_gain, _pad_cols(w_in[:, nmain:], LANES).astype(BF16), tn=LANES, name="gla_in_gate")
    wg = jnp.pad(w_gate_up.reshape(2 * GLA_RANK, -1), ((0, LANES - 2 * GLA_RANK), (0, 0))).astype(BF16)
    y = gla_scan(proj, glo, wg, b_gate.astype(F32), head_norm, B=B, S=S)
    return matmul_residual(y, w_out.astype(BF16), h, name="gla_out")


def gdn_layer(h, norm_gain, w_in, conv_w, a_log, dt_bias, head_norm, w_out, *, B, S):
    nconv = conv_w.shape[1]
    nmain = nconv + GDN_V_HEADS * GDN_HD
    proj = norm_matmul(h, norm_gain, w_in[:, :nmain].astype(BF16), name="gdn_in")
    ab = norm_matmul(h, norm_gain, w_in[:, nmain:].astype(BF16), tn=LANES, name="gdn_in_gate")
    qkv = gdn_conv(proj, conv_w, B=B, S=S)
    zeros = jnp.zeros((2 * GDN_V_HEADS,), F32)
    alog_row = jnp.concatenate([a_log.astype(F32).reshape(-1), zeros]).reshape(1, LANES)
    dtb_row = jnp.concatenate([dt_bias.astype(F32).reshape(-1), zeros]).reshape(1, LANES)
    y = gdn_scan(qkv, proj, ab, alog_row, dtb_row, head_norm, B=B, S=S)
    return matmul_residual(y, w_out.astype(BF16), h, name="gdn_out")


def diff_layer(h, norm_gain, w_in, q_norm, k_norm, lam, subln, w_out, bias_tiles, layer_idx, *, B, S):
    proj = norm_matmul(h, norm_gain, w_in.astype(BF16), name="diff_in")
    y = diff_attention(proj, q_norm, k_norm, lam, subln, bias_tiles, B=B, S=S, layer_idx=layer_idx)
    return matmul_residual(y, w_out.astype(BF16), h, name="diff_out")


def swa_layer(h, norm_gain, w_in, q_norm, k_norm, sink, w_out, bias, *, B, S):
    proj = norm_matmul(h, norm_gain, w_in.astype(BF16), name="swa_in")
    y = swa_attention(proj, q_norm, k_norm, sink, bias, B=B, S=S)
    return matmul_residual(y, w_out.astype(BF16), h, name="swa_out")


def kernel(x, rel_bias, norm_mix, norm_ffn, gla_w_in, gla_w_gate_up, gla_b_gate, gla_head_norm, gla_w_out, gdn_w_in, gdn_conv, gdn_a_log, gdn_dt_bias, gdn_head_norm, gdn_w_out, diff_w_in, diff_q_norm, diff_k_norm, diff_lambda, diff_subln, diff_w_out, swa_w_in, swa_q_norm, swa_k_norm, swa_sink, swa_w_out, moe_router, moe_w1, moe_w3, moe_w2):
    B, S, D = x.shape
    depth = norm_mix.shape[0]
    h = x.reshape(B * S, D)
    for i in range(depth):
        m, j = i % 4, i // 4
        if m == 0:
            h = gla_layer(h, norm_mix[i], gla_w_in[j], gla_w_gate_up[j], gla_b_gate[j], gla_head_norm[j],
                          gla_w_out[j], B=B, S=S)
        elif m == 1:
            h = gdn_layer(h, norm_mix[i], gdn_w_in[j], gdn_conv[j], gdn_a_log[j], gdn_dt_bias[j],
                          gdn_head_norm[j], gdn_w_out[j], B=B, S=S)
        elif m == 2:
            h = diff_layer(h, norm_mix[i], diff_w_in[j], diff_q_norm[j], diff_k_norm[j], diff_lambda[j],
                           diff_subln[j], diff_w_out[j], diff_bias_tiles(rel_bias), i, B=B, S=S)
        else:
            h = swa_layer(h, norm_mix[i], swa_w_in[j], swa_q_norm[j], swa_k_norm[j], swa_sink[j],
                          swa_w_out[j], swa_bias(rel_bias), B=B, S=S)
        h = ec_moe(h, norm_ffn[i], moe_router[i], moe_w1[i], moe_w3[i], moe_w2[i], B=B, S=S)
    return h.reshape(B, S, D)
```

```python
import functools
import math

import jax
import jax.numpy as jnp
from jax import lax
from jax.experimental import pallas as pl
from jax.experimental.pallas import tpu as pltpu

F32 = jnp.float32
BF16 = jnp.bfloat16
I32 = jnp.int32
HI = lax.Precision.HIGHEST

RMS_EPS = 1e-6
VMEM_LIMIT_BYTES = 56 * 1024 * 1024
LANES = 128

REL_BUCKETS = 32
REL_MAX_DIST = 128
CHUNK = 64
GLA_HEADS, GLA_DK, GLA_DV, GLA_RANK, GLA_TAU = 4, 256, 512, 16, 16.0
GDN_QK_HEADS, GDN_V_HEADS, GDN_HD, GDN_CONV = 16, 32, 128, 5
DIFF_HEADS, DIFF_DQK, DIFF_DV, DIFF_QB = 16, 64, 128, 256
DIFF_KB = 512
SWA_HEADS, SWA_KV, SWA_GROUP, SWA_HD, SWA_W, SWA_QB = 16, 4, 4, 128, 128, 128
N_EXPERTS = 16
EC_CAPACITY_FACTOR = 2


def _params(sem):
    return pltpu.CompilerParams(dimension_semantics=sem, vmem_limit_bytes=VMEM_LIMIT_BYTES)


def _dot(a, b, **kw):
    return jnp.dot(a, b, preferred_element_type=F32, **kw)


def _dot_nt(a, b, **kw):
    return lax.dot_general(a, b, (((1,), (1,)), ((), ())), preferred_element_type=F32, **kw)


def _dot_tn(a, b, **kw):
    return lax.dot_general(a, b, (((0,), (0,)), ((), ())), preferred_element_type=F32, **kw)


def _rms(x):
    return x * lax.rsqrt(jnp.mean(x * x, axis=-1, keepdims=True) + RMS_EPS)


def _sigmoid(x):
    return 1.0 / (1.0 + jnp.exp(-x))


def _softplus(x):
    return jnp.maximum(x, 0.0) + jnp.log(1.0 + jnp.exp(-jnp.abs(x)))


def _norm_matmul_kernel(x_ref, g_ref, w_ref, o_ref, xn_ref):
    @pl.when(pl.program_id(1) == 0)
    def _():
        xn_ref[...] = (_rms(x_ref[...]) * g_ref[...]).astype(BF16)

    o_ref[...] = _dot(xn_ref[...], w_ref[...])


def norm_matmul(x, gain, w, *, tm=1024, tn=512, name):
    T, D = x.shape
    N = w.shape[1]
    tm, tn = min(tm, T), min(tn, N)
    assert T % tm == 0 and N % tn == 0
    return pl.pallas_call(
        _norm_matmul_kernel,
        grid=(T // tm, N // tn),
        in_specs=[pl.BlockSpec((tm, D), lambda i, j: (i, 0)),
                  pl.BlockSpec((1, D), lambda i, j: (0, 0)),
                  pl.BlockSpec((D, tn), lambda i, j: (0, j))],
        out_specs=pl.BlockSpec((tm, tn), lambda i, j: (i, j)),
        out_shape=jax.ShapeDtypeStruct((T, N), F32),
        scratch_shapes=[pltpu.VMEM((tm, D), BF16)],
        compiler_params=_params(("parallel", "arbitrary")),
        name=name,
    )(x, gain.reshape(1, D), w)


def _matmul_res_kernel(y_ref, w_ref, h_ref, o_ref):
    o_ref[...] = h_ref[...] + _dot(y_ref[...].astype(BF16), w_ref[...])


def matmul_residual(y, w, h, *, tm=512, tn=512, name):
    T, K = y.shape
    N = w.shape[1]
    tm, tn = min(tm, T), min(tn, N)
    assert T % tm == 0 and N % tn == 0
    return pl.pallas_call(
        _matmul_res_kernel,
        grid=(T // tm, N // tn),
        in_specs=[pl.BlockSpec((tm, K), lambda i, j: (i, 0)),
                  pl.BlockSpec((K, tn), lambda i, j: (0, j)),
                  pl.BlockSpec((tm, tn), lambda i, j: (i, j))],
        out_specs=pl.BlockSpec((tm, tn), lambda i, j: (i, j)),
        out_shape=jax.ShapeDtypeStruct((T, N), F32),
        compiler_params=_params(("parallel", "parallel")),
        name=name,
    )(y, w, h)


def _tri_masks(bwd):
    row = lax.broadcasted_iota(I32, (CHUNK, CHUNK), 0)
    col = lax.broadcasted_iota(I32, (CHUNK, CHUNK), 1)
    incl = jnp.where(bwd, (col >= row).astype(F32), (col <= row).astype(F32))
    strict = jnp.where(bwd, (col > row).astype(F32), (col < row).astype(F32))
    return incl, strict


def _gla_kernel(q_ref, k_ref, v_ref, r_ref, glo_ref, wg_ref, bg_ref, hn_ref, o_ref,
                la_ref, of_ref, st_ref, *, nb, blk):
    i = pl.program_id(2)
    bwd = i >= nb
    sb = jnp.where(bwd, 2 * nb - 1 - i, i)
    nc = blk // CHUNK

    @pl.when((i == 0) | (i == nb))
    def _():
        st_ref[...] = jnp.zeros_like(st_ref)

    lane = lax.broadcasted_iota(I32, (blk, LANES), 1)
    lo = jnp.where(bwd, GLA_RANK, 0)
    gsel = jnp.where((lane >= lo) & (lane < lo + GLA_RANK), glo_ref[...], 0.0)
    bg = bg_ref[...]
    gate = _dot(gsel.astype(BF16), wg_ref[...]) + jnp.where(bwd, bg[1:2], bg[0:1])
    la_ref[...] = (jnp.minimum(gate, 0.0) - jnp.log(1.0 + jnp.exp(-jnp.abs(gate)))) * (1.0 / GLA_TAU)

    incl, _ = _tri_masks(bwd)
    gain = hn_ref[...]

    def body(c, carry):
        cc = jnp.where(bwd, nc - 1 - c, c)
        r0 = pl.multiple_of(cc * CHUNK, CHUNK)
        rows = pl.ds(r0, CHUNK)
        la = la_ref[rows, :]
        cum = _dot(incl, la, precision=HI)
        tot = jnp.sum(la, axis=0, keepdims=True)
        q = q_ref[rows, :] * (GLA_DK ** -0.5)
        k = k_ref[rows, :]
        v = v_ref[rows, :].astype(BF16)
        qd = (q * jnp.exp(cum)).astype(BF16)
        kin = (k * jnp.exp(-cum)).astype(BF16)
        kst = (k * jnp.exp(tot - cum)).astype(BF16)
        s = _dot_nt(qd, kin) * incl
        st = st_ref[...]
        o = _dot(s.astype(BF16), v) + _dot_nt(qd, st.astype(BF16))
        st_ref[...] = st * jnp.exp(tot) + _dot_tn(v, kst)
        grow = pl.ds(pl.multiple_of(sb * blk + r0, CHUNK), CHUNK)

        @pl.when(jnp.logical_not(bwd))
        def _():
            of_ref[grow, :] = o

        @pl.when(bwd)
        def _():
            ot = of_ref[grow, :] + o
            r = r_ref[rows, :]
            o_ref[rows, :] = _rms(ot) * gain * (r * _sigmoid(r))

        return carry

    lax.fori_loop(0, nc, body, 0)


def gla_scan(proj, glo, wg, bg, head_norm, *, B, S, blk=512):
    T = B * S
    blk = min(blk, S)
    nb = S // blk
    H = GLA_HEADS

    def rowblk(b, i):
        return b * nb + jnp.where(i >= nb, 2 * nb - 1 - i, i)

    def outblk(b, i):
        return b * nb + jnp.where(i >= nb, 2 * nb - 1 - i, nb - 1)

    kq = GLA_HEADS * GLA_DK // GLA_DK
    kv = 2 * GLA_HEADS * GLA_DK // GLA_DV
    kr = kv + GLA_HEADS
    return pl.pallas_call(
        functools.partial(_gla_kernel, nb=nb, blk=blk),
        grid=(B, H, 2 * nb),
        in_specs=[pl.BlockSpec((blk, GLA_DK), lambda b, h, i: (rowblk(b, i), h)),
                  pl.BlockSpec((blk, GLA_DK), lambda b, h, i: (rowblk(b, i), kq + h)),
                  pl.BlockSpec((blk, GLA_DV), lambda b, h, i: (rowblk(b, i), kv + h)),
                  pl.BlockSpec((blk, GLA_DV), lambda b, h, i: (outblk(b, i), kr + h)),
                  pl.BlockSpec((blk, LANES), lambda b, h, i: (rowblk(b, i), 0)),
                  pl.BlockSpec((LANES, GLA_DK), lambda b, h, i: (0, h)),
                  pl.BlockSpec((2, GLA_DK), lambda b, h, i: (0, h)),
                  pl.BlockSpec((1, GLA_DV), lambda b, h, i: (0, 0))],
        out_specs=pl.BlockSpec((blk, GLA_DV), lambda b, h, i: (outblk(b, i), h)),
        out_shape=jax.ShapeDtypeStruct((T, H * GLA_DV), F32),
        scratch_shapes=[pltpu.VMEM((blk, GLA_DK), F32),
                        pltpu.VMEM((S, GLA_DV), F32),
                        pltpu.VMEM((GLA_DV, GLA_DK), F32)],
        compiler_params=_params(("parallel", "parallel", "arbitrary")),
        name="gla_scan",
    )(proj, proj, proj, proj, glo, wg, bg, head_norm.reshape(1, GLA_DV))


def _gdn_conv_kernel(x_ref, w_ref, o_ref, xp_ref, *, S, rows):
    c = pl.program_id(1)
    pad = 8
    xp_ref[0:pad, :] = jnp.zeros((pad, LANES), F32)
    xp_ref[pad + S:2 * pad + S, :] = jnp.zeros((pad, LANES), F32)
    xp_ref[pad:pad + S, :] = x_ref[...]
    w = w_ref[...]
    win = rows + 2 * pad
    is_qk = c < 2 * GDN_QK_HEADS
    scale = jnp.where(c < GDN_QK_HEADS, GDN_HD ** -0.5, 1.0)

    def body(t, carry):
        r0 = pl.multiple_of(t * rows, rows)
        xw = xp_ref[pl.ds(r0, win), :]
        acc = jnp.zeros((rows, LANES), F32)
        for j in range(GDN_CONV):
            sh = (GDN_CONV // 2 - j) % win
            xs = xw if sh == 0 else pltpu.roll(xw, sh, 0)
            acc = acc + xs[pad:pad + rows, :] * w[j:j + 1, :]
        y = acc * _sigmoid(acc)
        yn = y * lax.rsqrt(jnp.sum(y * y, axis=-1, keepdims=True) + RMS_EPS) * scale
        o_ref[pl.ds(r0, rows), :] = jnp.where(is_qk, yn, y)
        return carry

    lax.fori_loop(0, S // rows, body, 0)


def gdn_conv(proj, conv_w, *, B, S):
    T = B * S
    nch = conv_w.shape[1] // LANES
    rows = min(256, S)
    return pl.pallas_call(
        functools.partial(_gdn_conv_kernel, S=S, rows=rows),
        grid=(B, nch),
        in_specs=[pl.BlockSpec((S, LANES), lambda b, c: (b, c)),
                  pl.BlockSpec((GDN_CONV, LANES), lambda b, c: (0, c))],
        out_specs=pl.BlockSpec((S, LANES), lambda b, c: (b, c)),
        out_shape=jax.ShapeDtypeStruct((T, nch * LANES), F32),
        scratch_shapes=[pltpu.VMEM((S + 16, LANES), F32)],
        compiler_params=_params(("parallel", "parallel")),
        name="gdn_conv",
    )(proj, conv_w)


def _mm_bf16(a, b):
    return _dot(a.astype(BF16), b.astype(BF16))


def _unit_tri_inverses(Ls):
    row = lax.broadcasted_iota(I32, (CHUNK, CHUNK), 0)
    col = lax.broadcasted_iota(I32, (CHUNK, CHUNK), 1)
    eye = (row == col).astype(F32)
    ps = [eye - L for L in Ls]
    pws = [_mm_bf16(L, L) for L in Ls]
    n = 2
    while True:
        ps = [p + _mm_bf16(p, pw) for p, pw in zip(ps, pws)]
        n *= 2
        if n >= CHUNK:
            return ps
        pws = [_mm_bf16(pw, pw) for pw in pws]


def _split3_bf16(x):
    hi = x.astype(BF16)
    r1 = x - hi.astype(F32)
    mid = r1.astype(BF16)
    lo = (r1 - mid.astype(F32)).astype(BF16)
    return hi, mid, lo


def _gdn_prepare(q_ref, k_ref, v_ref, ab_ref, tri_ref, alog, dtb, *, d, hv, blk):
    bwd = d == 1
    nc = blk // CHUNK
    cs = range(nc)
    row = lax.broadcasted_iota(I32, (CHUNK, CHUNK), 0)
    col = lax.broadcasted_iota(I32, (CHUNK, CHUNK), 1)
    incl = (col >= row) if bwd else (col <= row)
    strict = (col > row) if bwd else (col < row)

    x = ab_ref[...]
    gfull = -jnp.exp(alog) * _softplus(x + dtb)
    bfull = _sigmoid(x)
    rowi = lax.broadcasted_iota(I32, (LANES, LANES), 0)
    lane_g = d * GDN_V_HEADS + hv
    gb = _dot(gfull, (rowi == lane_g).astype(F32), precision=HI)
    bb = _dot(bfull, (rowi == 2 * GDN_V_HEADS + lane_g).astype(F32), precision=HI)
    tri = tri_ref[...]
    g_hi, g_mid, g_lo = _split3_bf16(gb)
    gc_blk = _dot(tri, g_hi) + _dot(tri, g_mid) + _dot(tri, g_lo)

    sl = [slice(c * CHUNK, (c + 1) * CHUNK) for c in cs]
    gc = [gc_blk[s] for s in sl]
    last = [c * CHUNK if bwd else (c + 1) * CHUNK - 1 for c in cs]
    tot = [gc_blk[r:r + 1] for r in last]
    gamma = [jnp.where(incl, jnp.exp(g[:, :CHUNK] - jnp.transpose(g)[:CHUNK, :]), 0.0) for g in gc]
    q = [q_ref[s, :] for s in sl]
    k = [k_ref[s, :] for s in sl]
    kbf = [t.astype(BF16) for t in k]
    kb = [k[c] * bb[sl[c]] for c in cs]
    kk = [_dot_nt(kb[c].astype(BF16), kbf[c]) for c in cs]
    tinv = _unit_tri_inverses([jnp.where(strict, kk[c] * gamma[c], 0.0) for c in cs])
    egc = [jnp.exp(g) for g in gc]
    rhs = [jnp.concatenate([v_ref[sl[c], :] * bb[sl[c]], kb[c] * egc[c]], axis=1) for c in cs]
    uw = [_mm_bf16(tinv[c], rhs[c]).astype(BF16) for c in cs]
    qk = [_dot_nt(q[c].astype(BF16), kbf[c]) for c in cs]
    qk = [(qk[c] * gamma[c]).astype(BF16) for c in cs]
    kst = [(k[c] * jnp.exp(tot[c] - gc[c])).astype(BF16) for c in cs]
    kuw = [_dot_tn(kst[c], uw[c]) for c in cs]
    quw = [_dot(qk[c], uw[c]) for c in cs]
    qt = [(q[c] * egc[c] - quw[c][:, GDN_HD:]).astype(BF16) for c in cs]
    return dict(qt=qt, qu=[t[:, :GDN_HD] for t in quw], kw=[t[:, GDN_HD:].astype(BF16) for t in kuw],
                ku=[t[:, :GDN_HD] for t in kuw], dec=[jnp.exp(t) for t in tot])


def _gdn_chunk_step(p, c, S):
    sb16 = S.astype(BF16)
    o = _dot(p["qt"][c], sb16) + p["qu"][c]
    return o, S * p["dec"][c] - _dot(p["kw"][c], sb16) + p["ku"][c]


def _gdn_kernel(qf_ref, kf_ref, vf_ref, abf_ref, qb_ref, kb_ref, vb_ref, abb_ref, trif_ref, trib_ref,
                z_ref, alog_ref, dtb_ref, hn_ref, o_ref, of_ref, ob_ref, st_ref, *, nb, blk, S):
    hv = pl.program_id(1)
    i = pl.program_id(2)
    nc = blk // CHUNK

    @pl.when(i == 0)
    def _():
        st_ref[...] = jnp.zeros_like(st_ref)

    alog = alog_ref[...]
    dtb = dtb_ref[...]
    pf = _gdn_prepare(qf_ref, kf_ref, vf_ref, abf_ref, trif_ref, alog, dtb, d=0, hv=hv, blk=blk)
    pb = _gdn_prepare(qb_ref, kb_ref, vb_ref, abb_ref, trib_ref, alog, dtb, d=1, hv=hv, blk=blk)
    sf = st_ref[0]
    sb = st_ref[1]
    rowf = i * blk
    rowb = (nb - 1 - i) * blk
    for c in range(nc):
        cb = nc - 1 - c
        of, sf = _gdn_chunk_step(pf, c, sf)
        ob, sb = _gdn_chunk_step(pb, cb, sb)
        of_ref[pl.ds(pl.multiple_of(rowf + c * CHUNK, CHUNK), CHUNK), :] = of
        ob_ref[pl.ds(pl.multiple_of(rowb + cb * CHUNK, CHUNK), CHUNK), :] = ob
    st_ref[0] = sf
    st_ref[1] = sb

    @pl.when(i == nb - 1)
    def _():
        gain = hn_ref[...]
        rows_e = min(256, S)

        def ebody(t, carry):
            rows = pl.ds(pl.multiple_of(t * rows_e, rows_e), rows_e)
            ot = of_ref[rows, :] + ob_ref[rows, :]
            z = z_ref[rows, :]
            o_ref[rows, :] = _rms(ot) * gain * (z * _sigmoid(z))
            return carry

        lax.fori_loop(0, S // rows_e, ebody, 0)


def _block_tri(blk, bwd):
    r = jnp.arange(blk, dtype=I32)[:, None]
    c = jnp.arange(blk, dtype=I32)[None, :]
    same = (r // CHUNK) == (c // CHUNK)
    return (same & ((c >= r) if bwd else (c <= r))).astype(BF16)


def gdn_scan(qkv, proj, ab, alog_row, dtb_row, head_norm, *, B, S, blk=512):
    T = B * S
    blk = min(blk, S)
    nb = S // blk
    HV = GDN_V_HEADS
    rep = GDN_V_HEADS // GDN_QK_HEADS
    zoff = (2 * GDN_QK_HEADS + GDN_V_HEADS)

    def fwd(b, i):
        return b * nb + i

    def bwd(b, i):
        return b * nb + nb - 1 - i

    def dir_specs(rb):
        return [pl.BlockSpec((blk, GDN_HD), lambda b, h, i: (rb(b, i), h // rep)),
                pl.BlockSpec((blk, GDN_HD), lambda b, h, i: (rb(b, i), GDN_QK_HEADS + h // rep)),
                pl.BlockSpec((blk, GDN_HD), lambda b, h, i: (rb(b, i), 2 * GDN_QK_HEADS + h)),
                pl.BlockSpec((blk, LANES), lambda b, h, i: (rb(b, i), 0))]

    const2 = lambda b, h, i: (0, 0)
    return pl.pallas_call(
        functools.partial(_gdn_kernel, nb=nb, blk=blk, S=S),
        grid=(B, HV, nb),
        in_specs=dir_specs(fwd) + dir_specs(bwd) + [
            pl.BlockSpec((blk, blk), const2),
            pl.BlockSpec((blk, blk), const2),
            pl.BlockSpec((S, GDN_HD), lambda b, h, i: (b, zoff + h)),
            pl.BlockSpec((1, LANES), const2),
            pl.BlockSpec((1, LANES), const2),
            pl.BlockSpec((1, GDN_HD), const2)],
        out_specs=pl.BlockSpec((S, GDN_HD), lambda b, h, i: (b, h)),
        out_shape=jax.ShapeDtypeStruct((T, HV * GDN_HD), F32),
        scratch_shapes=[pltpu.VMEM((S, GDN_HD), F32),
                        pltpu.VMEM((S, GDN_HD), F32),
                        pltpu.VMEM((2, GDN_HD, GDN_HD), F32)],
        compiler_params=_params(("parallel", "parallel", "arbitrary")),
        name="gdn_scan",
    )(qkv, qkv, qkv, ab, qkv, qkv, qkv, ab, _block_tri(blk, False), _block_tri(blk, True),
      proj, alog_row, dtb_row, head_norm.reshape(1, GDN_HD))


def _t5_bucket(rel):
    half = REL_BUCKETS // 2
    max_exact = half // 2
    n = jnp.abs(rel)
    log_ratio = jnp.log(jnp.maximum(n, 1).astype(F32) / max_exact) / math.log(REL_MAX_DIST / max_exact)
    large = jnp.minimum(max_exact + (log_ratio * (half - max_exact)).astype(I32), half - 1)
    return jnp.where(rel > 0, half, 0) + jnp.where(n < max_exact, n, large)


def _rel_bias_heads(table, rel):
    return jnp.moveaxis(table[_t5_bucket(rel)].astype(F32), -1, 0)


def _toeplitz(w, n, m, off):
    lw = w.shape[-1]
    assert lw == n + m - 1 and m <= lw - 1
    w_rot = jnp.roll(w, -off, axis=-1)
    flat = jnp.tile(w_rot, (1,) * (w.ndim - 1) + (n,))[..., :n * (lw - 1)]
    return flat.reshape(w.shape[:-1] + (n, lw - 1))[..., :m]


def _half_rms(x, ones_bd):
    ms = _dot(x * x, ones_bd, precision=HI) * (1.0 / DIFF_DQK)
    return x * lax.rsqrt(ms + RMS_EPS)


LOG2E = math.log2(math.e)


def _diff_kernel(q_ref, k_ref, v_ref, qn_ref, kn_ref, lam_ref, sub_ref, bias_ref, o_ref,
                 kb_ref, vb_ref, m_ref, l_ref, acc_ref, s_ref, *, nk, lambda_init):
    i = pl.program_id(2)
    qb = DIFF_QB
    r = lax.broadcasted_iota(I32, (LANES, LANES), 0) // DIFF_DQK
    c = lax.broadcasted_iota(I32, (LANES, LANES), 1) // DIFF_DQK
    ones_bd = (r == c).astype(F32)

    @pl.when(i == 0)
    def _():
        def kbody(t, carry):
            rows = pl.ds(pl.multiple_of(t * qb, qb), qb)
            kb_ref[rows, :] = (_half_rms(k_ref[rows, :], ones_bd) * kn_ref[...]).astype(BF16)
            vb_ref[rows, :] = v_ref[rows, :].astype(BF16)
            return carry
        lax.fori_loop(0, nk, kbody, 0)

    q = _half_rms(q_ref[...], ones_bd) * qn_ref[...] * (DIFF_DQK ** -0.5 * LOG2E)
    lane = lax.broadcasted_iota(I32, (qb, LANES), 1)
    qs = (jnp.where(lane < DIFF_DQK, q, 0.0).astype(BF16), jnp.where(lane >= DIFF_DQK, q, 0.0).astype(BF16))

    m_ref[...] = jnp.full(m_ref.shape, -jnp.inf, F32)
    l_ref[...] = jnp.zeros(l_ref.shape, F32)
    acc_ref[...] = jnp.zeros(acc_ref.shape, F32)

    kw = DIFF_KB // qb
    maps = range(2)

    nsteps = nk * qb // DIFF_KB

    def key_rows(t):
        start = t * DIFF_KB
        return pl.ds(start if isinstance(t, int) else pl.multiple_of(start, DIFF_KB), DIFF_KB)

    def scores(t, slot):
        kc = kb_ref[key_rows(t), :]
        bias = jnp.concatenate([bias_ref[0, jnp.clip(kw * t + u - i, -2, 2) + 2] for u in range(kw)], axis=1)
        for mi in maps:
            s_ref[slot, mi] = _dot_nt(qs[mi], kc) + bias

    def chunk(t, prefetch):
        slot = t % 2
        s = [s_ref[slot, mi] for mi in maps]
        if prefetch:
            scores(t + 1, 1 - slot)
        vc = vb_ref[key_rows(t), :]
        m_cur = [jnp.max(s[mi], axis=1, keepdims=True) for mi in maps]
        m_prev = [m_ref[mi] for mi in maps]
        m_new = [jnp.maximum(m_prev[mi], m_cur[mi]) for mi in maps]
        alpha = [jnp.exp2(m_prev[mi] - m_new[mi]) for mi in maps]
        p = [jnp.exp2(s[mi] - jnp.concatenate([m_new[mi]] * (DIFF_KB // LANES), axis=1)) for mi in maps]
        psum = [jnp.sum(p[mi], axis=1, keepdims=True) for mi in maps]
        pv = [_dot(p[mi].astype(BF16), vc) for mi in maps]
        for mi in maps:
            l_ref[mi] = alpha[mi] * l_ref[mi] + psum[mi]
            acc_ref[mi] = alpha[mi] * acc_ref[mi] + pv[mi]
            m_ref[mi] = m_new[mi]

    scores(0, 0)
    for t in range(nsteps - 1):
        chunk(t, True)
    chunk(nsteps - 1, False)

    lam = lam_ref[...]
    lam_full = (jnp.exp(jnp.sum(lam[0:1] * lam[1:2], axis=-1, keepdims=True))
                - jnp.exp(jnp.sum(lam[2:3] * lam[3:4], axis=-1, keepdims=True)) + lambda_init)
    o = acc_ref[0] / l_ref[0] - lam_full * (acc_ref[1] / l_ref[1])
    o_ref[...] = _rms(o) * sub_ref[...] * (1.0 - lambda_init)


def diff_attention(proj, q_norm, k_norm, lam, subln, bias_tiles, *, B, S, layer_idx):
    T = B * S
    qb = DIFF_QB
    nq = S // qb
    H = DIFF_HEADS
    lambda_init = 0.8 - 0.6 * math.exp(-0.3 * layer_idx)
    qn2 = jnp.concatenate([q_norm, q_norm]).reshape(1, LANES)
    kn2 = jnp.concatenate([k_norm, k_norm]).reshape(1, LANES)
    return pl.pallas_call(
        functools.partial(_diff_kernel, nk=nq, lambda_init=lambda_init),
        grid=(B, H, nq),
        in_specs=[pl.BlockSpec((qb, LANES), lambda b, h, i: (b * nq + i, h)),
                  pl.BlockSpec((S, LANES), lambda b, h, i: (b, H + h)),
                  pl.BlockSpec((S, LANES), lambda b, h, i: (b, 2 * H + h)),
                  pl.BlockSpec((1, LANES), lambda b, h, i: (0, 0)),
                  pl.BlockSpec((1, LANES), lambda b, h, i: (0, 0)),
                  pl.BlockSpec((4, DIFF_DQK), lambda b, h, i: (0, 0)),
                  pl.BlockSpec((1, DIFF_DV), lambda b, h, i: (0, 0)),
                  pl.BlockSpec((1, 5, qb, qb), lambda b, h, i: (h, 0, 0, 0))],
        out_specs=pl.BlockSpec((qb, DIFF_DV), lambda b, h, i: (b * nq + i, h)),
        out_shape=jax.ShapeDtypeStruct((T, H * DIFF_DV), F32),
        scratch_shapes=[pltpu.VMEM((S, LANES), BF16),
                        pltpu.VMEM((S, DIFF_DV), BF16),
                        pltpu.VMEM((2, qb, LANES), F32),
                        pltpu.VMEM((2, qb, LANES), F32),
                        pltpu.VMEM((2, qb, DIFF_DV), F32),
                        pltpu.VMEM((2, 2, qb, DIFF_KB), F32)],
        compiler_params=_params(("parallel", "parallel", "arbitrary")),
        name="diff_attn",
    )(proj, proj, proj, qn2, kn2, lam, subln.reshape(1, DIFF_DV), bias_tiles)


def diff_bias_tiles(table):
    qb = DIFF_QB
    span = 3 * qb - 1
    vec = _rel_bias_heads(table, jnp.arange(-span, span + 1, dtype=I32)) * LOG2E
    w = jnp.stack([vec[:, (d + 2) * qb:(d + 2) * qb + 2 * qb - 1] for d in range(-2, 3)], axis=1)
    return _toeplitz(w, qb, qb, qb - 1)


def _swa_kernel(q_ref, k0_ref, k1_ref, k2_ref, v0_ref, v1_ref, v2_ref, qn_ref, kn_ref, sink_ref,
                bias_ref, o_ref, *, nq, S):
    i = pl.program_id(2)
    qb = SWA_QB
    span = 3 * qb
    kcat = jnp.concatenate([k0_ref[...], k1_ref[...], k2_ref[...]], axis=0)
    kcat = (_rms(kcat) * kn_ref[...]).astype(BF16)
    vcat = jnp.concatenate([v0_ref[...], v1_ref[...], v2_ref[...]], axis=0).astype(BF16)
    row = lax.broadcasted_iota(I32, (qb, span), 0)
    col = lax.broadcasted_iota(I32, (qb, span), 1)
    rel = col - SWA_W - row
    key_pos = i * qb - SWA_W + col
    valid = (jnp.abs(rel) <= SWA_W) & (key_pos >= 0) & (key_pos < S)
    sink_all = sink_ref[0]
    for g in range(SWA_GROUP):
        q = q_ref[:, g * SWA_HD:(g + 1) * SWA_HD]
        q = (_rms(q) * qn_ref[...] * (SWA_HD ** -0.5)).astype(BF16)
        s = _dot_nt(q, kcat) + bias_ref[g]
        s = jnp.where(valid, s, -jnp.inf)
        sink = sink_all[g:g + 1, 0:1]
        m = jnp.maximum(jnp.max(s, axis=-1, keepdims=True), sink)
        p = jnp.exp(s - m)
        p = p / (jnp.sum(p, axis=-1, keepdims=True) + jnp.exp(sink - m))
        o_ref[:, g * SWA_HD:(g + 1) * SWA_HD] = _dot(p.astype(BF16), vcat)


def swa_attention(proj, q_norm, k_norm, sink, bias, *, B, S):
    T = B * S
    qb = SWA_QB
    nq = S // qb
    koff = SWA_HEADS
    voff = SWA_HEADS + SWA_KV
    gw = SWA_GROUP * SWA_HD

    def kvspec(off, d):
        return pl.BlockSpec((qb, SWA_HD),
                            lambda b, kv, i: (b * nq + jnp.clip(i + d, 0, nq - 1), off + kv))

    sink_b = jnp.broadcast_to(sink.astype(F32).reshape(SWA_KV, SWA_GROUP, 1), (SWA_KV, SWA_GROUP, LANES))
    return pl.pallas_call(
        functools.partial(_swa_kernel, nq=nq, S=S),
        grid=(B, SWA_KV, nq),
        in_specs=[pl.BlockSpec((qb, gw), lambda b, kv, i: (b * nq + i, kv)),
                  kvspec(koff, -1), kvspec(koff, 0), kvspec(koff, 1),
                  kvspec(voff, -1), kvspec(voff, 0), kvspec(voff, 1),
                  pl.BlockSpec((1, SWA_HD), lambda b, kv, i: (0, 0)),
                  pl.BlockSpec((1, SWA_HD), lambda b, kv, i: (0, 0)),
                  pl.BlockSpec((1, SWA_GROUP, LANES), lambda b, kv, i: (kv, 0, 0)),
                  pl.BlockSpec((SWA_GROUP, qb, 3 * qb), lambda b, kv, i: (kv, 0, 0))],
        out_specs=pl.BlockSpec((qb, gw), lambda b, kv, i: (b * nq + i, kv)),
        out_shape=jax.ShapeDtypeStruct((T, SWA_HEADS * SWA_HD), F32),
        compiler_params=_params(("parallel", "parallel", "parallel")),
        name="swa_attn",
    )(proj, proj, proj, proj, proj, proj, proj, q_norm.reshape(1, SWA_HD), k_norm.reshape(1, SWA_HD),
      sink_b, bias)


def swa_bias(table):
    qb = SWA_QB
    span = 3 * qb
    vec = _rel_bias_heads(table, jnp.arange(-(qb - 1) - SWA_W, span - SWA_W, dtype=I32))
    return _toeplitz(vec, qb, span, qb - 1)


def _router_kernel(h_ref, g_ref, wr_ref, hx_ref, at_ref, *, D):
    hn = _rms(h_ref[...]) * g_ref[...]
    logits = _dot(hn, wr_ref[...], precision=HI)
    lane = lax.broadcasted_iota(I32, logits.shape, 1)
    logits = jnp.where(lane < N_EXPERTS, logits, -jnp.inf)
    m = jnp.max(logits, axis=-1, keepdims=True)
    e = jnp.exp(logits - m)
    aff = e / jnp.sum(e, axis=-1, keepdims=True)
    hx_ref[:, :D] = hn
    hx_ref[:, D:] = aff
    at_ref[0] = jnp.transpose(aff)


def moe_router(h, gain, router, *, B, S, tm=512):
    T, D = h.shape
    tm = min(tm, S)
    ns = S // tm
    wr = jnp.pad(router.astype(F32), ((0, 0), (0, LANES - N_EXPERTS)))
    return pl.pallas_call(
        functools.partial(_router_kernel, D=D),
        grid=(B, ns),
        in_specs=[pl.BlockSpec((tm, D), lambda b, s: (b * ns + s, 0)),
                  pl.BlockSpec((1, D), lambda b, s: (0, 0)),
                  pl.BlockSpec((D, LANES), lambda b, s: (0, 0))],
        out_specs=[pl.BlockSpec((tm, D + LANES), lambda b, s: (b * ns + s, 0)),
                   pl.BlockSpec((1, LANES, tm), lambda b, s: (b, 0, s))],
        out_shape=[jax.ShapeDtypeStruct((T, D + LANES), F32),
                   jax.ShapeDtypeStruct((B, LANES, S), F32)],
        compiler_params=_params(("parallel", "parallel")),
        name="moe_router",
    )(h, gain.reshape(1, D), wr)


def _topk_kernel(aff_ref, idx_ref, pos_ref, *, S, cap):
    E = N_EXPERTS
    v = aff_ref[0]
    bits = pltpu.bitcast(v, I32)

    def search(_, carry):
        lo, hi = carry
        mid = lo + ((hi - lo) >> 1)
        cnt = jnp.sum((bits >= mid).astype(F32), axis=1, keepdims=True)
        ok = cnt >= cap
        return jnp.where(ok, mid, lo), jnp.where(ok, hi, mid)

    lo0 = jnp.zeros((E, 1), I32)
    hi0 = jnp.full((E, 1), 0x7F800001, I32)
    thr, _ = lax.fori_loop(0, 32, search, (lo0, hi0))
    gt = bits > thr
    eq = bits == thr
    need = cap - jnp.sum(gt.astype(F32), axis=1, keepdims=True)

    r = lax.broadcasted_iota(I32, (LANES, LANES), 0)
    c = lax.broadcasted_iota(I32, (LANES, LANES), 1)
    upper = (r < c).astype(BF16)
    run_e = jnp.zeros((E, 1), F32)
    run_s = jnp.zeros((E, 1), F32)
    sub = 512 // LANES
    for t in range(S // LANES):
        sl = slice(t * LANES, (t + 1) * LANES)
        eq_t = eq[:, sl].astype(F32)
        pe = _dot(eq_t.astype(BF16), upper) + run_e
        sel_t = jnp.where(gt[:, sl], 1.0, jnp.where(pe < need, eq_t, 0.0))
        ps = _dot(sel_t.astype(BF16), upper) + run_s
        pos_ref[t // sub, :, (t % sub) * LANES:(t % sub + 1) * LANES] = jnp.where(sel_t > 0, ps, -1.0)
        run_e = run_e + jnp.sum(eq_t, axis=1, keepdims=True)
        run_s = run_s + jnp.sum(sel_t, axis=1, keepdims=True)

    pi = lax.broadcasted_iota(I32, (cap, 512), 0).astype(F32)
    lane = lax.broadcasted_iota(I32, (8, 512), 1)
    rowv = lax.broadcasted_iota(I32, (8, 512), 0)

    def per_expert(e, carry):
        def per_chunk(t, acc):
            pos = pos_ref[t, pl.ds(e, 1), :]
            onehot = (pi == pos).astype(BF16)
            tok = t * 512 + lane
            vals = jnp.where(rowv == 0, tok >> 6, jnp.where(rowv == 1, tok & 63, 0)).astype(F32).astype(BF16)
            return acc + _dot_nt(vals, onehot)
        acc = lax.fori_loop(0, S // 512, per_chunk, jnp.zeros((8, cap), F32))
        idx_ref[0, pl.ds(e, 1), :] = (acc[0:1] * 64.0 + acc[1:2]).astype(I32)
        return carry

    lax.fori_loop(0, E, per_expert, 0)


def moe_topk(aff_t, *, B, S):
    cap = EC_CAPACITY_FACTOR * S // N_EXPERTS
    return pl.pallas_call(
        functools.partial(_topk_kernel, S=S, cap=cap),
        grid=(B,),
        in_specs=[pl.BlockSpec((1, N_EXPERTS, S), lambda b: (b, 0, 0))],
        out_specs=pl.BlockSpec((1, N_EXPERTS, cap), lambda b: (b, 0, 0)),
        out_shape=jax.ShapeDtypeStruct((B, N_EXPERTS, cap), I32),
        scratch_shapes=[pltpu.VMEM((S // 512, N_EXPERTS, 512), F32)],
        compiler_params=_params(("parallel",)),
        name="moe_topk",
    )(aff_t)


def _ffn_kernel(idx_ref, hx_hbm, h_in, w1_ref, w3_ref, w2_ref, h_out, xbuf, acc, sem, *, S, D, cap):
    del h_in
    e = pl.program_id(0)
    b = pl.program_id(1)
    base = b * S

    def token_row(r):
        return pl.ds(base + idx_ref[0, 0, r], 1)

    def gather_x(r):
        return pltpu.make_async_copy(hx_hbm.at[token_row(r), :], xbuf.at[pl.ds(r, 1), :], sem.at[0])

    def gather_h(r):
        return pltpu.make_async_copy(h_out.at[token_row(r), :], acc.at[pl.ds(r, 1), :], sem.at[1])

    def scatter_h(r):
        return pltpu.make_async_copy(acc.at[pl.ds(r, 1), :], h_out.at[token_row(r), :], sem.at[2])

    def issue_in(r, carry):
        gather_x(r).start()
        gather_h(r).start()
        return carry

    lax.fori_loop(0, cap, issue_in, 0, unroll=8)
    pltpu.make_async_copy(hx_hbm.at[pl.ds(0, cap), :], xbuf, sem.at[0]).wait()
    pltpu.make_async_copy(h_out.at[pl.ds(0, cap), :], acc, sem.at[1]).wait()

    x = xbuf[:, :D].astype(BF16)
    a = _dot(x, w1_ref[0])
    g = _dot(x, w3_ref[0])
    hm = (a * _sigmoid(a) * g).astype(BF16)
    y = _dot(hm, w2_ref[0])
    rowi = lax.broadcasted_iota(I32, (LANES, LANES), 0)
    gate = _dot(xbuf[:, D:], (rowi == e).astype(F32), precision=HI)
    acc[...] = acc[...] + y * gate[:, 0:1]

    def issue_out(r, carry):
        scatter_h(r).start()
        return carry

    lax.fori_loop(0, cap, issue_out, 0, unroll=8)
    pltpu.make_async_copy(acc, h_out.at[pl.ds(0, cap), :], sem.at[2]).wait()


def moe_ffn(idx, hx, h, w1, w3, w2, *, B, S):
    T, D = h.shape
    E = N_EXPERTS
    cap = idx.shape[-1]
    F = w1.shape[-1]
    idx3 = idx.reshape(B * E, 1, cap)
    return pl.pallas_call(
        functools.partial(_ffn_kernel, S=S, D=D, cap=cap),
        grid=(E, B),
        in_specs=[pl.BlockSpec((1, 1, cap), lambda e, b: (b * E + e, 0, 0), memory_space=pltpu.SMEM),
                  pl.BlockSpec(memory_space=pl.ANY),
                  pl.BlockSpec(memory_space=pl.ANY),
                  pl.BlockSpec((1, D, F), lambda e, b: (e, 0, 0)),
                  pl.BlockSpec((1, D, F), lambda e, b: (e, 0, 0)),
                  pl.BlockSpec((1, F, D), lambda e, b: (e, 0, 0))],
        out_specs=pl.BlockSpec(memory_space=pl.ANY),
        out_shape=jax.ShapeDtypeStruct((T, D), F32),
        scratch_shapes=[pltpu.VMEM((cap, D + LANES), F32),
                        pltpu.VMEM((cap, D), F32),
                        pltpu.SemaphoreType.DMA((3,))],
        input_output_aliases={2: 0},
        compiler_params=_params(("arbitrary", "arbitrary")),
        name="moe_ffn",
    )(idx3, hx, h, w1, w3, w2)


def ec_moe(h, gain, router, w1, w3, w2, *, B, S):
    hx, aff_t = moe_router(h, gain, router, B=B, S=S)
    idx = moe_topk(aff_t, B=B, S=S)
    return moe_ffn(idx, hx, h, w1.astype(BF16), w3.astype(BF16), w2.astype(BF16), B=B, S=S)


def _pad_cols(w, n):
    return jnp.pad(w, ((0, 0), (0, n - w.shape[1])))


def gla_layer(h, norm_gain, w_in, w_gate_up, b_gate, head_norm, w_out, *, B, S):
    nmain = 2 * GLA_HEADS * GLA_DK + 2 * GLA_HEADS * GLA_DV
    proj = norm_matmul(h, norm_gain, w_in[:, :nmain].astype(BF16), name="gla_in")
    glo = norm_matmul(h, norm_gain, _pad_cols(w_in[:, nmain:], LANES).astype(BF16), tn=LANES, name="gla_in_gate")
    wg = jnp.pad(w_gate_up.reshape(2 * GLA_RANK, -1), ((0, LANES - 2 * GLA_RANK), (0, 0))).astype(BF16)
    y = gla_scan(proj, glo, wg, b_gate.astype(F32), head_norm, B=B, S=S)
    return matmul_residual(y, w_out.astype(BF16), h, name="gla_out")


def gdn_layer(h, norm_gain, w_in, conv_w, a_log, dt_bias, head_norm, w_out, *, B, S):
    nconv = conv_w.shape[1]
    nmain = nconv + GDN_V_HEADS * GDN_HD
    proj = norm_matmul(h, norm_gain, w_in[:, :nmain].astype(BF16), name="gdn_in")
    ab = norm_matmul(h, norm_gain, w_in[:, nmain:].astype(BF16), tn=LANES, name="gdn_in_gate")
    qkv = gdn_conv(proj, conv_w, B=B, S=S)
    zeros = jnp.zeros((2 * GDN_V_HEADS,), F32)
    alog_row = jnp.concatenate([a_log.astype(F32).reshape(-1), zeros]).reshape(1, LANES)
    dtb_row = jnp.concatenate([dt_bias.astype(F32).reshape(-1), zeros]).reshape(1, LANES)
    y = gdn_scan(qkv, proj, ab, alog_row, dtb_row, head_norm, B=B, S=S)
    return matmul_residual(y, w_out.astype(BF16), h, name="gdn_out")


def diff_layer(h, norm_gain, w_in, q_norm, k_norm, lam, subln, w_out, bias_tiles, layer_idx, *, B, S):
    proj = norm_matmul(h, norm_gain, w_in.astype(BF16), name="diff_in")
    y = diff_attention(proj, q_norm, k_norm, lam, subln, bias_tiles, B=B, S=S, layer_idx=layer_idx)
    return matmul_residual(y, w_out.astype(BF16), h, name="diff_out")


def swa_layer(h, norm_gain, w_in, q_norm, k_norm, sink, w_out, bias, *, B, S):
    proj = norm_matmul(h, norm_gain, w_in.astype(BF16), name="swa_in")
    y = swa_attention(proj, q_norm, k_norm, sink, bias, B=B, S=S)
    return matmul_residual(y, w_out.astype(BF16), h, name="swa_out")


def kernel(x, rel_bias, norm_mix, norm_ffn, gla_w_in, gla_w_gate_up, gla_b_gate, gla_head_norm, gla_w_out, gdn_w_in, gdn_conv, gdn_a_log, gdn_dt_bias, gdn_head_norm, gdn_w_out, diff_w_in, diff_q_norm, diff_k_norm, diff_lambda, diff_subln, diff_w_out, swa_w_in, swa_q_norm, swa_k_norm, swa_sink, swa_w_out, moe_router, moe_w1, moe_w3, moe_w2):
    B, S, D = x.shape
    depth = norm_mix.shape[0]
    h = x.reshape(B * S, D)
    for i in range(depth):
        m, j = i % 4, i // 4
        if m == 0:
            h = gla_layer(h, norm_mix[i], gla_w_in[j], gla_w_gate_up[j], gla_b_gate[j], gla_head_norm[j],
                          gla_w_out[j], B=B, S=S)
        elif m == 1:
            h = gdn_layer(h, norm_mix[i], gdn_w_in[j], gdn_conv[j], gdn_a_log[j], gdn_dt_bias[j],
                          gdn_head_norm[j], gdn_w_out[j], B=B, S=S)
        elif m == 2:
            h = diff_layer(h, norm_mix[i], diff_w_in[j], diff_q_norm[j], diff_k_norm[j], diff_lambda[j],
                           diff_subln[j], diff_w_out[j], diff_bias_tiles(rel_bias), i, B=B, S=S)
        else:
            h = swa_layer(h, norm_mix[i], swa_w_in[j], swa_q_norm[j], swa_k_norm[j], swa_sink[j],
                          swa_w_out[j], swa_bias(rel_bias), B=B, S=S)
        h = ec_moe(h, norm_ffn[i], moe_router[i], moe_w1[i], moe_w3[i], moe_w2[i], B=B, S=S)
    return h.reshape(B, S, D)
```

```python
import functools
import math

import jax
import jax.numpy as jnp
from jax import lax
from jax.experimental import pallas as pl
from jax.experimental.pallas import tpu as pltpu

F32 = jnp.float32
BF16 = jnp.bfloat16
I32 = jnp.int32
HI = lax.Precision.HIGHEST

RMS_EPS = 1e-6
VMEM_LIMIT_BYTES = 56 * 1024 * 1024
LANES = 128

REL_BUCKETS = 32
REL_MAX_DIST = 128
CHUNK = 64
GLA_HEADS, GLA_DK, GLA_DV, GLA_RANK, GLA_TAU = 4, 256, 512, 16, 16.0
GDN_QK_HEADS, GDN_V_HEADS, GDN_HD, GDN_CONV = 16, 32, 128, 5
GDN_REP = GDN_V_HEADS // GDN_QK_HEADS
TRI_ROWS = 256
DIFF_HEADS, DIFF_DQK, DIFF_DV, DIFF_QB = 16, 64, 128, 256
DIFF_KB = 512
SWA_HEADS, SWA_KV, SWA_GROUP, SWA_HD, SWA_W, SWA_QB = 16, 4, 4, 128, 128, 128
N_EXPERTS = 16
EC_CAPACITY_FACTOR = 2


def _params(sem):
    return pltpu.CompilerParams(dimension_semantics=sem, vmem_limit_bytes=VMEM_LIMIT_BYTES)


def _dot(a, b, **kw):
    return jnp.dot(a, b, preferred_element_type=F32, **kw)


def _dot_nt(a, b, **kw):
    return lax.dot_general(a, b, (((1,), (1,)), ((), ())), preferred_element_type=F32, **kw)


def _dot_tn(a, b, **kw):
    return lax.dot_general(a, b, (((0,), (0,)), ((), ())), preferred_element_type=F32, **kw)


def _rms(x):
    return x * lax.rsqrt(jnp.mean(x * x, axis=-1, keepdims=True) + RMS_EPS)


def _sigmoid(x):
    return 1.0 / (1.0 + jnp.exp(-x))


def _softplus(x):
    return jnp.maximum(x, 0.0) + jnp.log(1.0 + jnp.exp(-jnp.abs(x)))


def _norm_matmul_kernel(x_ref, g_ref, w_ref, o_ref, xn_ref):
    @pl.when(pl.program_id(1) == 0)
    def _():
        xn_ref[...] = (_rms(x_ref[...]) * g_ref[...]).astype(BF16)

    o_ref[...] = _dot(xn_ref[...], w_ref[...])


def norm_matmul(x, gain, w, *, tm=1024, tn=512, name):
    T, D = x.shape
    N = w.shape[1]
    tm, tn = min(tm, T), min(tn, N)
    assert T % tm == 0 and N % tn == 0
    return pl.pallas_call(
        _norm_matmul_kernel,
        grid=(T // tm, N // tn),
        in_specs=[pl.BlockSpec((tm, D), lambda i, j: (i, 0)),
                  pl.BlockSpec((1, D), lambda i, j: (0, 0)),
                  pl.BlockSpec((D, tn), lambda i, j: (0, j))],
        out_specs=pl.BlockSpec((tm, tn), lambda i, j: (i, j)),
        out_shape=jax.ShapeDtypeStruct((T, N), F32),
        scratch_shapes=[pltpu.VMEM((tm, D), BF16)],
        compiler_params=_params(("parallel", "arbitrary")),
        name=name,
    )(x, gain.reshape(1, D), w)


def _matmul_res_kernel(y_ref, w_ref, h_ref, o_ref):
    o_ref[...] = h_ref[...] + _dot(y_ref[...].astype(BF16), w_ref[...])


def matmul_residual(y, w, h, *, tm=512, tn=512, name):
    T, K = y.shape
    N = w.shape[1]
    tm, tn = min(tm, T), min(tn, N)
    assert T % tm == 0 and N % tn == 0
    return pl.pallas_call(
        _matmul_res_kernel,
        grid=(T // tm, N // tn),
        in_specs=[pl.BlockSpec((tm, K), lambda i, j: (i, 0)),
                  pl.BlockSpec((K, tn), lambda i, j: (0, j)),
                  pl.BlockSpec((tm, tn), lambda i, j: (i, j))],
        out_specs=pl.BlockSpec((tm, tn), lambda i, j: (i, j)),
        out_shape=jax.ShapeDtypeStruct((T, N), F32),
        compiler_params=_params(("parallel", "parallel")),
        name=name,
    )(y, w, h)


def _tri_masks(bwd):
    row = lax.broadcasted_iota(I32, (CHUNK, CHUNK), 0)
    col = lax.broadcasted_iota(I32, (CHUNK, CHUNK), 1)
    incl = jnp.where(bwd, (col >= row).astype(F32), (col <= row).astype(F32))
    strict = jnp.where(bwd, (col > row).astype(F32), (col < row).astype(F32))
    return incl, strict


def _gla_kernel(q_ref, k_ref, v_ref, r_ref, glo_ref, wg_ref, bg_ref, hn_ref, trif_ref, trib_ref, o_ref,
                cum_ref, of_ref, st_ref, *, nb, blk):
    i = pl.program_id(2)
    bwd = i >= nb
    sb = jnp.where(bwd, 2 * nb - 1 - i, i)
    nc = blk // CHUNK
    cs = range(nc)

    @pl.when((i == 0) | (i == nb))
    def _():
        st_ref[...] = jnp.zeros_like(st_ref)

    lane = lax.broadcasted_iota(I32, (blk, LANES), 1)
    lo = jnp.where(bwd, GLA_RANK, 0)
    gsel = jnp.where((lane >= lo) & (lane < lo + GLA_RANK), glo_ref[...], 0.0)
    bg = bg_ref[...]
    gate = _dot(gsel.astype(BF16), wg_ref[...]) + jnp.where(bwd, bg[1:2], bg[0:1])
    la = (jnp.minimum(gate, 0.0) - jnp.log(1.0 + jnp.exp(-jnp.abs(gate)))) * (1.0 / GLA_TAU)
    tri = jnp.where(bwd, trib_ref[...], trif_ref[...])
    grp = tri.shape[0]
    pieces = _split3_bf16(la)
    for r in range(0, blk, grp):
        cum_ref[r:r + grp, :] = sum(_dot(tri, pc[r:r + grp]) for pc in pieces)

    incl, _ = _tri_masks(bwd)
    r0 = [pl.multiple_of(jnp.where(bwd, nc - 1 - c, c) * CHUNK, CHUNK) for c in cs]
    rows = [pl.ds(r, CHUNK) for r in r0]
    cum = [cum_ref[rw, :] for rw in rows]
    tot = [cum_ref[pl.ds(r0[c] + jnp.where(bwd, 0, CHUNK - 1), 1), :] for c in cs]
    q = [q_ref[rw, :] * (GLA_DK ** -0.5) for rw in rows]
    k = [k_ref[rw, :] for rw in rows]
    v = [v_ref[rw, :].astype(BF16) for rw in rows]
    qd = [(q[c] * jnp.exp(cum[c])).astype(BF16) for c in cs]
    kin = [(k[c] * jnp.exp(-cum[c])).astype(BF16) for c in cs]
    kst = [(k[c] * jnp.exp(tot[c] - cum[c])).astype(BF16) for c in cs]
    s = [(_dot_nt(qd[c], kin[c]) * incl).astype(BF16) for c in cs]
    o = [_dot(s[c], v[c]) for c in cs]
    st = st_ref[...]
    upd = _dot_tn(v[0], kst[0])
    for c in cs:
        nxt = _dot_tn(v[c + 1], kst[c + 1]) if c + 1 < nc else None
        o[c] = o[c] + _dot_nt(qd[c], st.astype(BF16))
        st = st * jnp.exp(tot[c]) + upd
        upd = nxt
    st_ref[...] = st

    @pl.when(jnp.logical_not(bwd))
    def _():
        for c in cs:
            of_ref[pl.ds(pl.multiple_of(sb * blk + r0[c], CHUNK), CHUNK), :] = o[c]

    @pl.when(bwd)
    def _():
        gain = hn_ref[...]
        for c in cs:
            ot = of_ref[pl.ds(pl.multiple_of(sb * blk + r0[c], CHUNK), CHUNK), :] + o[c]
            r = r_ref[rows[c], :]
            o_ref[rows[c], :] = _rms(ot) * gain * (r * _sigmoid(r))


def gla_scan(proj, glo, wg, bg, head_norm, *, B, S, blk=512):
    T = B * S
    blk = min(blk, S)
    nb = S // blk
    H = GLA_HEADS
    grp = min(TRI_ROWS, blk)

    def rowblk(b, i):
        return b * nb + jnp.where(i >= nb, 2 * nb - 1 - i, i)

    def outblk(b, i):
        return b * nb + jnp.where(i >= nb, 2 * nb - 1 - i, nb - 1)

    kq = GLA_HEADS * GLA_DK // GLA_DK
    kv = 2 * GLA_HEADS * GLA_DK // GLA_DV
    kr = kv + GLA_HEADS
    return pl.pallas_call(
        functools.partial(_gla_kernel, nb=nb, blk=blk),
        grid=(B, H, 2 * nb),
        in_specs=[pl.BlockSpec((blk, GLA_DK), lambda b, h, i: (rowblk(b, i), h)),
                  pl.BlockSpec((blk, GLA_DK), lambda b, h, i: (rowblk(b, i), kq + h)),
                  pl.BlockSpec((blk, GLA_DV), lambda b, h, i: (rowblk(b, i), kv + h)),
                  pl.BlockSpec((blk, GLA_DV), lambda b, h, i: (outblk(b, i), kr + h)),
                  pl.BlockSpec((blk, LANES), lambda b, h, i: (rowblk(b, i), 0)),
                  pl.BlockSpec((LANES, GLA_DK), lambda b, h, i: (0, h)),
                  pl.BlockSpec((2, GLA_DK), lambda b, h, i: (0, h)),
                  pl.BlockSpec((1, GLA_DV), lambda b, h, i: (0, 0)),
                  pl.BlockSpec((grp, grp), lambda b, h, i: (0, 0)),
                  pl.BlockSpec((grp, grp), lambda b, h, i: (0, 0))],
        out_specs=pl.BlockSpec((blk, GLA_DV), lambda b, h, i: (outblk(b, i), h)),
        out_shape=jax.ShapeDtypeStruct((T, H * GLA_DV), F32),
        scratch_shapes=[pltpu.VMEM((blk, GLA_DK), F32),
                        pltpu.VMEM((S, GLA_DV), F32),
                        pltpu.VMEM((GLA_DV, GLA_DK), F32)],
        compiler_params=_params(("parallel", "parallel", "arbitrary")),
        name="gla_scan",
    )(proj, proj, proj, proj, glo, wg, bg, head_norm.reshape(1, GLA_DV),
      _block_tri(grp, False), _block_tri(grp, True))


def _gdn_conv_kernel(x_ref, w_ref, o_ref, xp_ref, *, S, rows):
    c = pl.program_id(1)
    pad = 8
    xp_ref[0:pad, :] = jnp.zeros((pad, LANES), F32)
    xp_ref[pad + S:2 * pad + S, :] = jnp.zeros((pad, LANES), F32)
    xp_ref[pad:pad + S, :] = x_ref[...]
    w = w_ref[...]
    win = rows + 2 * pad
    is_qk = c < 2 * GDN_QK_HEADS
    scale = jnp.where(c < GDN_QK_HEADS, GDN_HD ** -0.5, 1.0)

    def body(t, carry):
        r0 = pl.multiple_of(t * rows, rows)
        xw = xp_ref[pl.ds(r0, win), :]
        acc = jnp.zeros((rows, LANES), F32)
        for j in range(GDN_CONV):
            sh = (GDN_CONV // 2 - j) % win
            xs = xw if sh == 0 else pltpu.roll(xw, sh, 0)
            acc = acc + xs[pad:pad + rows, :] * w[j:j + 1, :]
        y = acc * _sigmoid(acc)
        yn = y * lax.rsqrt(jnp.sum(y * y, axis=-1, keepdims=True) + RMS_EPS) * scale
        o_ref[pl.ds(r0, rows), :] = jnp.where(is_qk, yn, y)
        return carry

    lax.fori_loop(0, S // rows, body, 0)


def gdn_conv(proj, conv_w, *, B, S):
    T = B * S
    nch = conv_w.shape[1] // LANES
    rows = min(256, S)
    return pl.pallas_call(
        functools.partial(_gdn_conv_kernel, S=S, rows=rows),
        grid=(B, nch),
        in_specs=[pl.BlockSpec((S, LANES), lambda b, c: (b, c)),
                  pl.BlockSpec((GDN_CONV, LANES), lambda b, c: (0, c))],
        out_specs=pl.BlockSpec((S, LANES), lambda b, c: (b, c)),
        out_shape=jax.ShapeDtypeStruct((T, nch * LANES), F32),
        scratch_shapes=[pltpu.VMEM((S + 16, LANES), F32)],
        compiler_params=_params(("parallel", "parallel")),
        name="gdn_conv",
    )(proj, conv_w)


def _mm_bf16(a, b):
    return _dot(a.astype(BF16), b.astype(BF16))


def _unit_tri_inverses(Ls):
    row = lax.broadcasted_iota(I32, (CHUNK, CHUNK), 0)
    col = lax.broadcasted_iota(I32, (CHUNK, CHUNK), 1)
    eye = (row == col).astype(F32)
    ps = [eye - L for L in Ls]
    pws = [_mm_bf16(L, L) for L in Ls]
    n = 2
    while True:
        ps = [p + _mm_bf16(p, pw) for p, pw in zip(ps, pws)]
        n *= 2
        if n >= CHUNK:
            return ps
        pws = [_mm_bf16(pw, pw) for pw in pws]


def _split3_bf16(x):
    hi = x.astype(BF16)
    r1 = x - hi.astype(F32)
    mid = r1.astype(BF16)
    lo = (r1 - mid.astype(F32)).astype(BF16)
    return hi, mid, lo


def _gdn_prepare(q_ref, k_ref, v_ref, ab_ref, tri_ref, alog, dtb, *, d, qh, blk):
    bwd = d == 1
    nc = blk // CHUNK
    cs = range(nc)
    hs = range(GDN_REP)
    hc = [(h, c) for h in hs for c in cs]
    row = lax.broadcasted_iota(I32, (CHUNK, CHUNK), 0)
    col = lax.broadcasted_iota(I32, (CHUNK, CHUNK), 1)
    incl = (col >= row) if bwd else (col <= row)
    strict = (col > row) if bwd else (col < row)

    x = ab_ref[...]
    gfull = -jnp.exp(alog) * _softplus(x + dtb)
    bfull = _sigmoid(x)
    lane = lax.broadcasted_iota(I32, (blk, LANES), 1)

    def pick(full, ln):
        return jnp.broadcast_to(jnp.sum(jnp.where(lane == ln, full, 0.0), axis=1, keepdims=True), (blk, LANES))

    lane_g = [d * GDN_V_HEADS + GDN_REP * qh + h for h in hs]
    gb = [pick(gfull, ln) for ln in lane_g]
    bb = [pick(bfull, 2 * GDN_V_HEADS + ln) for ln in lane_g]
    tri = tri_ref[...]
    grp = tri.shape[0]
    pieces = [_split3_bf16(g) for g in gb]
    gc_blk = [jnp.concatenate([sum(_dot(tri, pc[r:r + grp]) for pc in pieces[h]) for r in range(0, blk, grp)], axis=0)
              for h in hs]

    sl = [slice(c * CHUNK, (c + 1) * CHUNK) for c in cs]
    last = [c * CHUNK if bwd else (c + 1) * CHUNK - 1 for c in cs]
    gc = [gc_blk[h][sl[c]] for h, c in hc]
    tot = [gc_blk[h][last[c]:last[c] + 1] for h, c in hc]
    beta = [bb[h][sl[c]] for h, c in hc]
    gamma = [jnp.where(incl, jnp.exp(g[:, :CHUNK] - jnp.transpose(g)[:CHUNK, :]), 0.0) for g in gc]
    q = [q_ref[s, :] for s in sl]
    k = [k_ref[s, :] for s in sl]
    qbf = [t.astype(BF16) for t in q]
    kbf = [t.astype(BF16) for t in k]
    kk = [_dot_nt(kbf[c], kbf[c]) for c in cs]
    qkr = [_dot_nt(qbf[c], kbf[c]) for c in cs]
    tinv = _unit_tri_inverses([jnp.where(strict, kk[c] * beta[j][:, :CHUNK] * gamma[j], 0.0)
                               for j, (h, c) in enumerate(hc)])
    egc = [jnp.exp(g) for g in gc]
    rhs = [jnp.concatenate([v_ref[sl[c], h * GDN_HD:(h + 1) * GDN_HD] * beta[j], k[c] * beta[j] * egc[j]], axis=1)
           for j, (h, c) in enumerate(hc)]
    uw = [_mm_bf16(tinv[j], rhs[j]).astype(BF16) for j in range(len(hc))]
    qk = [(qkr[c] * gamma[j]).astype(BF16) for j, (h, c) in enumerate(hc)]
    kst = [(k[c] * jnp.exp(tot[j] - gc[j])).astype(BF16) for j, (h, c) in enumerate(hc)]
    kuw = [_dot_tn(kst[j], uw[j]) for j in range(len(hc))]
    quw = [_dot(qk[j], uw[j]) for j in range(len(hc))]
    qt = [(q[c] * egc[j] - quw[j][:, GDN_HD:]).astype(BF16) for j, (h, c) in enumerate(hc)]
    return dict(qt=qt, qu=[t[:, :GDN_HD] for t in quw], kw=[t[:, GDN_HD:].astype(BF16) for t in kuw],
                ku=[t[:, :GDN_HD] for t in kuw], dec=[jnp.exp(t) for t in tot])


def _gdn_chunk_step(p, j, S):
    sb16 = S.astype(BF16)
    o = _dot(p["qt"][j], sb16) + p["qu"][j]
    return o, S * p["dec"][j] - _dot(p["kw"][j], sb16) + p["ku"][j]


def _gdn_kernel(qf_ref, kf_ref, vf_ref, abf_ref, qb_ref, kb_ref, vb_ref, abb_ref, trif_ref, trib_ref,
                z_ref, alog_ref, dtb_ref, hn_ref, o_ref, of_ref, ob_ref, st_ref, *, nb, blk, S):
    qh = pl.program_id(1)
    i = pl.program_id(2)
    nc = blk // CHUNK

    @pl.when(i == 0)
    def _():
        st_ref[...] = jnp.zeros_like(st_ref)

    alog = alog_ref[...]
    dtb = dtb_ref[...]
    pf = _gdn_prepare(qf_ref, kf_ref, vf_ref, abf_ref, trif_ref, alog, dtb, d=0, qh=qh, blk=blk)
    pb = _gdn_prepare(qb_ref, kb_ref, vb_ref, abb_ref, trib_ref, alog, dtb, d=1, qh=qh, blk=blk)
    sf = [st_ref[0, h] for h in range(GDN_REP)]
    sb = [st_ref[1, h] for h in range(GDN_REP)]
    rowf = i * blk
    rowb = (nb - 1 - i) * blk
    for c in range(nc):
        cb = nc - 1 - c
        outs = []
        for h in range(GDN_REP):
            of, sf[h] = _gdn_chunk_step(pf, h * nc + c, sf[h])
            ob, sb[h] = _gdn_chunk_step(pb, h * nc + cb, sb[h])
            outs.append((of, ob))
        for h, (of, ob) in enumerate(outs):
            cols = slice(h * GDN_HD, (h + 1) * GDN_HD)
            of_ref[pl.ds(pl.multiple_of(rowf + c * CHUNK, CHUNK), CHUNK), cols] = of
            ob_ref[pl.ds(pl.multiple_of(rowb + cb * CHUNK, CHUNK), CHUNK), cols] = ob
    for h in range(GDN_REP):
        st_ref[0, h] = sf[h]
        st_ref[1, h] = sb[h]

    @pl.when(i == nb - 1)
    def _():
        gain = hn_ref[...]
        rows_e = min(256, S)

        def ebody(t, carry):
            rows = pl.ds(pl.multiple_of(t * rows_e, rows_e), rows_e)
            for h in range(GDN_REP):
                cols = slice(h * GDN_HD, (h + 1) * GDN_HD)
                ot = of_ref[rows, cols] + ob_ref[rows, cols]
                z = z_ref[rows, cols]
                o_ref[rows, cols] = _rms(ot) * gain * (z * _sigmoid(z))
            return carry

        lax.fori_loop(0, S // rows_e, ebody, 0)


def _block_tri(n, bwd):
    r = jnp.arange(n, dtype=I32)[:, None]
    c = jnp.arange(n, dtype=I32)[None, :]
    same = (r // CHUNK) == (c // CHUNK)
    return (same & ((c >= r) if bwd else (c <= r))).astype(BF16)


def gdn_scan(qkv, proj, ab, alog_row, dtb_row, head_norm, *, B, S, blk=512):
    T = B * S
    blk = min(blk, S)
    nb = S // blk
    grp = min(TRI_ROWS, blk)
    vw = GDN_REP * GDN_HD
    voff = 2 * GDN_QK_HEADS * GDN_HD // vw
    zoff = (2 * GDN_QK_HEADS + GDN_V_HEADS) * GDN_HD // vw

    def fwd(b, i):
        return b * nb + i

    def bwd(b, i):
        return b * nb + nb - 1 - i

    def dir_specs(rb):
        return [pl.BlockSpec((blk, GDN_HD), lambda b, h, i: (rb(b, i), h)),
                pl.BlockSpec((blk, GDN_HD), lambda b, h, i: (rb(b, i), GDN_QK_HEADS + h)),
                pl.BlockSpec((blk, vw), lambda b, h, i: (rb(b, i), voff + h)),
                pl.BlockSpec((blk, LANES), lambda b, h, i: (rb(b, i), 0))]

    const2 = lambda b, h, i: (0, 0)
    return pl.pallas_call(
        functools.partial(_gdn_kernel, nb=nb, blk=blk, S=S),
        grid=(B, GDN_QK_HEADS, nb),
        in_specs=dir_specs(fwd) + dir_specs(bwd) + [
            pl.BlockSpec((grp, grp), const2),
            pl.BlockSpec((grp, grp), const2),
            pl.BlockSpec((S, vw), lambda b, h, i: (b, zoff + h)),
            pl.BlockSpec((1, LANES), const2),
            pl.BlockSpec((1, LANES), const2),
            pl.BlockSpec((1, GDN_HD), const2)],
        out_specs=pl.BlockSpec((S, vw), lambda b, h, i: (b, h)),
        out_shape=jax.ShapeDtypeStruct((T, GDN_V_HEADS * GDN_HD), F32),
        scratch_shapes=[pltpu.VMEM((S, vw), F32),
                        pltpu.VMEM((S, vw), F32),
                        pltpu.VMEM((2, GDN_REP, GDN_HD, GDN_HD), F32)],
        compiler_params=_params(("parallel", "parallel", "arbitrary")),
        name="gdn_scan",
    )(qkv, qkv, qkv, ab, qkv, qkv, qkv, ab, _block_tri(grp, False), _block_tri(grp, True),
      proj, alog_row, dtb_row, head_norm.reshape(1, GDN_HD))


def _t5_bucket(rel):
    half = REL_BUCKETS // 2
    max_exact = half // 2
    n = jnp.abs(rel)
    log_ratio = jnp.log(jnp.maximum(n, 1).astype(F32) / max_exact) / math.log(REL_MAX_DIST / max_exact)
    large = jnp.minimum(max_exact + (log_ratio * (half - max_exact)).astype(I32), half - 1)
    return jnp.where(rel > 0, half, 0) + jnp.where(n < max_exact, n, large)


def _rel_bias_heads(table, rel):
    return jnp.moveaxis(table[_t5_bucket(rel)].astype(F32), -1, 0)


def _toeplitz(w, n, m, off):
    lw = w.shape[-1]
    assert lw == n + m - 1 and m <= lw - 1
    w_rot = jnp.roll(w, -off, axis=-1)
    flat = jnp.tile(w_rot, (1,) * (w.ndim - 1) + (n,))[..., :n * (lw - 1)]
    return flat.reshape(w.shape[:-1] + (n, lw - 1))[..., :m]


def _half_rms(x, ones_bd):
    ms = _dot(x * x, ones_bd, precision=HI) * (1.0 / DIFF_DQK)
    return x * lax.rsqrt(ms + RMS_EPS)


LOG2E = math.log2(math.e)


def _diff_kernel(q_ref, k_ref, v_ref, qn_ref, kn_ref, lam_ref, sub_ref, bias_ref, o_ref,
                 kb_ref, vb_ref, m_ref, l_ref, acc_ref, s_ref, *, nk, lambda_init):
    i = pl.program_id(2)
    qb = DIFF_QB
    r = lax.broadcasted_iota(I32, (LANES, LANES), 0) // DIFF_DQK
    c = lax.broadcasted_iota(I32, (LANES, LANES), 1) // DIFF_DQK
    ones_bd = (r == c).astype(F32)

    @pl.when(i == 0)
    def _():
        def kbody(t, carry):
            rows = pl.ds(pl.multiple_of(t * qb, qb), qb)
            kb_ref[rows, :] = (_half_rms(k_ref[rows, :], ones_bd) * kn_ref[...]).astype(BF16)
            vb_ref[rows, :] = v_ref[rows, :].astype(BF16)
            return carry
        lax.fori_loop(0, nk, kbody, 0)

    q = _half_rms(q_ref[...], ones_bd) * qn_ref[...] * (DIFF_DQK ** -0.5 * LOG2E)
    lane = lax.broadcasted_iota(I32, (qb, LANES), 1)
    qs = (jnp.where(lane < DIFF_DQK, q, 0.0).astype(BF16), jnp.where(lane >= DIFF_DQK, q, 0.0).astype(BF16))

    m_ref[...] = jnp.full(m_ref.shape, -jnp.inf, F32)
    l_ref[...] = jnp.zeros(l_ref.shape, F32)
    acc_ref[...] = jnp.zeros(acc_ref.shape, F32)

    kw = DIFF_KB // qb
    maps = range(2)

    nsteps = nk * qb // DIFF_KB

    def key_rows(t):
        start = t * DIFF_KB
        return pl.ds(start if isinstance(t, int) else pl.multiple_of(start, DIFF_KB), DIFF_KB)

    def scores(t, slot):
        kc = kb_ref[key_rows(t), :]
        bias = jnp.concatenate([bias_ref[0, jnp.clip(kw * t + u - i, -2, 2) + 2] for u in range(kw)], axis=1)
        for mi in maps:
            s_ref[slot, mi] = _dot_nt(qs[mi], kc) + bias

    def chunk(t, prefetch):
        slot = t % 2
        s = [s_ref[slot, mi] for mi in maps]
        if prefetch:
            scores(t + 1, 1 - slot)
        vc = vb_ref[key_rows(t), :]
        m_cur = [jnp.max(s[mi], axis=1, keepdims=True) for mi in maps]
        m_prev = [m_ref[mi] for mi in maps]
        m_new = [jnp.maximum(m_prev[mi], m_cur[mi]) for mi in maps]
        alpha = [jnp.exp2(m_prev[mi] - m_new[mi]) for mi in maps]
        p = [jnp.exp2(s[mi] - jnp.concatenate([m_new[mi]] * (DIFF_KB // LANES), axis=1)) for mi in maps]
        psum = [jnp.sum(p[mi], axis=1, keepdims=True) for mi in maps]
        pv = [_dot(p[mi].astype(BF16), vc) for mi in maps]
        for mi in maps:
            l_ref[mi] = alpha[mi] * l_ref[mi] + psum[mi]
            acc_ref[mi] = alpha[mi] * acc_ref[mi] + pv[mi]
            m_ref[mi] = m_new[mi]

    scores(0, 0)
    for t in range(nsteps - 1):
        chunk(t, True)
    chunk(nsteps - 1, False)

    lam = lam_ref[...]
    lam_full = (jnp.exp(jnp.sum(lam[0:1] * lam[1:2], axis=-1, keepdims=True))
                - jnp.exp(jnp.sum(lam[2:3] * lam[3:4], axis=-1, keepdims=True)) + lambda_init)
    o = acc_ref[0] / l_ref[0] - lam_full * (acc_ref[1] / l_ref[1])
    o_ref[...] = _rms(o) * sub_ref[...] * (1.0 - lambda_init)


def diff_attention(proj, q_norm, k_norm, lam, subln, bias_tiles, *, B, S, layer_idx):
    T = B * S
    qb = DIFF_QB
    nq = S // qb
    H = DIFF_HEADS
    lambda_init = 0.8 - 0.6 * math.exp(-0.3 * layer_idx)
    qn2 = jnp.concatenate([q_norm, q_norm]).reshape(1, LANES)
    kn2 = jnp.concatenate([k_norm, k_norm]).reshape(1, LANES)
    return pl.pallas_call(
        functools.partial(_diff_kernel, nk=nq, lambda_init=lambda_init),
        grid=(B, H, nq),
        in_specs=[pl.BlockSpec((qb, LANES), lambda b, h, i: (b * nq + i, h)),
                  pl.BlockSpec((S, LANES), lambda b, h, i: (b, H + h)),
                  pl.BlockSpec((S, LANES), lambda b, h, i: (b, 2 * H + h)),
                  pl.BlockSpec((1, LANES), lambda b, h, i: (0, 0)),
                  pl.BlockSpec((1, LANES), lambda b, h, i: (0, 0)),
                  pl.BlockSpec((4, DIFF_DQK), lambda b, h, i: (0, 0)),
                  pl.BlockSpec((1, DIFF_DV), lambda b, h, i: (0, 0)),
                  pl.BlockSpec((1, 5, qb, qb), lambda b, h, i: (h, 0, 0, 0))],
        out_specs=pl.BlockSpec((qb, DIFF_DV), lambda b, h, i: (b * nq + i, h)),
        out_shape=jax.ShapeDtypeStruct((T, H * DIFF_DV), F32),
        scratch_shapes=[pltpu.VMEM((S, LANES), BF16),
                        pltpu.VMEM((S, DIFF_DV), BF16),
                        pltpu.VMEM((2, qb, LANES), F32),
                        pltpu.VMEM((2, qb, LANES), F32),
                        pltpu.VMEM((2, qb, DIFF_DV), F32),
                        pltpu.VMEM((2, 2, qb, DIFF_KB), F32)],
        compiler_params=_params(("parallel", "parallel", "arbitrary")),
        name="diff_attn",
    )(proj, proj, proj, qn2, kn2, lam, subln.reshape(1, DIFF_DV), bias_tiles)


def diff_bias_tiles(table):
    qb = DIFF_QB
    span = 3 * qb - 1
    vec = _rel_bias_heads(table, jnp.arange(-span, span + 1, dtype=I32)) * LOG2E
    w = jnp.stack([vec[:, (d + 2) * qb:(d + 2) * qb + 2 * qb - 1] for d in range(-2, 3)], axis=1)
    return _toeplitz(w, qb, qb, qb - 1)


def _swa_kernel(q_ref, k0_ref, k1_ref, k2_ref, v0_ref, v1_ref, v2_ref, qn_ref, kn_ref, sink_ref,
                bias_ref, o_ref, *, nq, S):
    i = pl.program_id(2)
    qb = SWA_QB
    span = 3 * qb
    kcat = jnp.concatenate([k0_ref[...], k1_ref[...], k2_ref[...]], axis=0)
    kcat = (_rms(kcat) * kn_ref[...]).astype(BF16)
    vcat = jnp.concatenate([v0_ref[...], v1_ref[...], v2_ref[...]], axis=0).astype(BF16)
    row = lax.broadcasted_iota(I32, (qb, span), 0)
    col = lax.broadcasted_iota(I32, (qb, span), 1)
    rel = col - SWA_W - row
    key_pos = i * qb - SWA_W + col
    valid = (jnp.abs(rel) <= SWA_W) & (key_pos >= 0) & (key_pos < S)
    sink_all = sink_ref[0]
    gs = range(SWA_GROUP)
    q = [q_ref[:, g * SWA_HD:(g + 1) * SWA_HD] for g in gs]
    q = [(_rms(t) * qn_ref[...] * (SWA_HD ** -0.5)).astype(BF16) for t in q]
    s = [_dot_nt(q[g], kcat) + bias_ref[g] for g in gs]
    s = [jnp.where(valid, t, -jnp.inf) for t in s]
    sink = [sink_all[g:g + 1, 0:1] for g in gs]
    m = [jnp.maximum(jnp.max(s[g], axis=-1, keepdims=True), sink[g]) for g in gs]
    p = [jnp.exp(s[g] - m[g]) for g in gs]
    den = [jnp.sum(p[g], axis=-1, keepdims=True) + jnp.exp(sink[g] - m[g]) for g in gs]
    p = [(p[g] / den[g]).astype(BF16) for g in gs]
    o = [_dot(p[g], vcat) for g in gs]
    for g in gs:
        o_ref[:, g * SWA_HD:(g + 1) * SWA_HD] = o[g]


def swa_attention(proj, q_norm, k_norm, sink, bias, *, B, S):
    T = B * S
    qb = SWA_QB
    nq = S // qb
    koff = SWA_HEADS
    voff = SWA_HEADS + SWA_KV
    gw = SWA_GROUP * SWA_HD

    def kvspec(off, d):
        return pl.BlockSpec((qb, SWA_HD),
                            lambda b, kv, i: (b * nq + jnp.clip(i + d, 0, nq - 1), off + kv))

    sink_b = jnp.broadcast_to(sink.astype(F32).reshape(SWA_KV, SWA_GROUP, 1), (SWA_KV, SWA_GROUP, LANES))
    return pl.pallas_call(
        functools.partial(_swa_kernel, nq=nq, S=S),
        grid=(B, SWA_KV, nq),
        in_specs=[pl.BlockSpec((qb, gw), lambda b, kv, i: (b * nq + i, kv)),
                  kvspec(koff, -1), kvspec(koff, 0), kvspec(koff, 1),
                  kvspec(voff, -1), kvspec(voff, 0), kvspec(voff, 1),
                  pl.BlockSpec((1, SWA_HD), lambda b, kv, i: (0, 0)),
                  pl.BlockSpec((1, SWA_HD), lambda b, kv, i: (0, 0)),
                  pl.BlockSpec((1, SWA_GROUP, LANES), lambda b, kv, i: (kv, 0, 0)),
                  pl.BlockSpec((SWA_GROUP, qb, 3 * qb), lambda b, kv, i: (kv, 0, 0))],
        out_specs=pl.BlockSpec((qb, gw), lambda b, kv, i: (b * nq + i, kv)),
        out_shape=jax.ShapeDtypeStruct((T, SWA_HEADS * SWA_HD), F32),
        compiler_params=_params(("parallel", "parallel", "parallel")),
        name="swa_attn",
    )(proj, proj, proj, proj, proj, proj, proj, q_norm.reshape(1, SWA_HD), k_norm.reshape(1, SWA_HD),
      sink_b, bias)


def swa_bias(table):
    qb = SWA_QB
    span = 3 * qb
    vec = _rel_bias_heads(table, jnp.arange(-(qb - 1) - SWA_W, span - SWA_W, dtype=I32))
    return _toeplitz(vec, qb, span, qb - 1)


def _router_kernel(h_ref, g_ref, wr_ref, hx_ref, at_ref, *, D):
    hn = _rms(h_ref[...]) * g_ref[...]
    logits = _dot(hn, wr_ref[...], precision=HI)
    lane = lax.broadcasted_iota(I32, logits.shape, 1)
    logits = jnp.where(lane < N_EXPERTS, logits, -jnp.inf)
    m = jnp.max(logits, axis=-1, keepdims=True)
    e = jnp.exp(logits - m)
    aff = e / jnp.sum(e, axis=-1, keepdims=True)
    hx_ref[:, :D] = hn
    hx_ref[:, D:] = aff
    at_ref[0] = jnp.transpose(aff)


def moe_router(h, gain, router, *, B, S, tm=512):
    T, D = h.shape
    tm = min(tm, S)
    ns = S // tm
    wr = jnp.pad(router.astype(F32), ((0, 0), (0, LANES - N_EXPERTS)))
    return pl.pallas_call(
        functools.partial(_router_kernel, D=D),
        grid=(B, ns),
        in_specs=[pl.BlockSpec((tm, D), lambda b, s: (b * ns + s, 0)),
                  pl.BlockSpec((1, D), lambda b, s: (0, 0)),
                  pl.BlockSpec((D, LANES), lambda b, s: (0, 0))],
        out_specs=[pl.BlockSpec((tm, D + LANES), lambda b, s: (b * ns + s, 0)),
                   pl.BlockSpec((1, LANES, tm), lambda b, s: (b, 0, s))],
        out_shape=[jax.ShapeDtypeStruct((T, D + LANES), F32),
                   jax.ShapeDtypeStruct((B, LANES, S), F32)],
        compiler_params=_params(("parallel", "parallel")),
        name="moe_router",
    )(h, gain.reshape(1, D), wr)


def _topk_kernel(aff_ref, idx_ref, pos_ref, *, S, cap):
    E = N_EXPERTS
    v = aff_ref[0]
    bits = pltpu.bitcast(v, I32)

    def search(_, carry):
        lo, hi = carry
        mid = lo + ((hi - lo) >> 1)
        cnt = jnp.sum((bits >= mid).astype(F32), axis=1, keepdims=True)
        ok = cnt >= cap
        return jnp.where(ok, mid, lo), jnp.where(ok, hi, mid)

    lo0 = jnp.zeros((E, 1), I32)
    hi0 = jnp.full((E, 1), 0x7F800001, I32)
    thr, _ = lax.fori_loop(0, 32, search, (lo0, hi0))
    gt = bits > thr
    eq = bits == thr
    need = cap - jnp.sum(gt.astype(F32), axis=1, keepdims=True)

    r = lax.broadcasted_iota(I32, (LANES, LANES), 0)
    c = lax.broadcasted_iota(I32, (LANES, LANES), 1)
    upper = (r < c).astype(BF16)
    run_e = jnp.zeros((E, 1), F32)
    run_s = jnp.zeros((E, 1), F32)
    sub = 512 // LANES
    for t in range(S // LANES):
        sl = slice(t * LANES, (t + 1) * LANES)
        eq_t = eq[:, sl].astype(F32)
        pe = _dot(eq_t.astype(BF16), upper) + run_e
        sel_t = jnp.where(gt[:, sl], 1.0, jnp.where(pe < need, eq_t, 0.0))
        ps = _dot(sel_t.astype(BF16), upper) + run_s
        pos_ref[t // sub, :, (t % sub) * LANES:(t % sub + 1) * LANES] = jnp.where(sel_t > 0, ps, -1.0)
        run_e = run_e + jnp.sum(eq_t, axis=1, keepdims=True)
        run_s = run_s + jnp.sum(sel_t, axis=1, keepdims=True)

    pi = lax.broadcasted_iota(I32, (cap, 512), 0).astype(F32)
    lane = lax.broadcasted_iota(I32, (8, 512), 1)
    rowv = lax.broadcasted_iota(I32, (8, 512), 0)

    def per_expert(e, carry):
        def per_chunk(t, acc):
            pos = pos_ref[t, pl.ds(e, 1), :]
            onehot = (pi == pos).astype(BF16)
            tok = t * 512 + lane
            vals = jnp.where(rowv == 0, tok >> 6, jnp.where(rowv == 1, tok & 63, 0)).astype(F32).astype(BF16)
            return acc + _dot_nt(vals, onehot)
        acc = lax.fori_loop(0, S // 512, per_chunk, jnp.zeros((8, cap), F32))
        idx_ref[0, pl.ds(e, 1), :] = (acc[0:1] * 64.0 + acc[1:2]).astype(I32)
        return carry

    lax.fori_loop(0, E, per_expert, 0)


def moe_topk(aff_t, *, B, S):
    cap = EC_CAPACITY_FACTOR * S // N_EXPERTS
    return pl.pallas_call(
        functools.partial(_topk_kernel, S=S, cap=cap),
        grid=(B,),
        in_specs=[pl.BlockSpec((1, N_EXPERTS, S), lambda b: (b, 0, 0))],
        out_specs=pl.BlockSpec((1, N_EXPERTS, cap), lambda b: (b, 0, 0)),
        out_shape=jax.ShapeDtypeStruct((B, N_EXPERTS, cap), I32),
        scratch_shapes=[pltpu.VMEM((S // 512, N_EXPERTS, 512), F32)],
        compiler_params=_params(("parallel",)),
        name="moe_topk",
    )(aff_t)


def _ffn_kernel(idx0_ref, idx1_ref, idxn_ref, hx_hbm, h_in, w1_ref, w3_ref, w2_ref, h_out,
                xbuf, acc, sem_x, sem_h, sem_s, *, S, D, cap, nj):
    del h_in
    e = pl.program_id(0)
    j = pl.program_id(1)
    first = (e == 0) & (j == 0)
    last = (e == pl.num_programs(0) - 1) & (j == nj - 1)
    base0 = (2 * j) * S
    base1 = base0 + S
    jn = jnp.where(last, j, (j + 1) % nj)
    basen = (2 * jn) * S

    def gather_x(idx_ref, base, slot, r):
        return pltpu.make_async_copy(hx_hbm.at[pl.ds(base + idx_ref[0, 0, r], 1), :],
                                     xbuf.at[slot, pl.ds(r, 1), :], sem_x.at[slot])

    def gather_h(idx_ref, base, slot, r):
        return pltpu.make_async_copy(h_out.at[pl.ds(base + idx_ref[0, 0, r], 1), :],
                                     acc.at[slot, pl.ds(r, 1), :], sem_h.at[slot])

    def scatter_h(idx_ref, base, slot, r):
        return pltpu.make_async_copy(acc.at[slot, pl.ds(r, 1), :],
                                     h_out.at[pl.ds(base + idx_ref[0, 0, r], 1), :], sem_s.at[slot])

    def wait_x(slot):
        pltpu.make_async_copy(hx_hbm.at[pl.ds(0, cap), :], xbuf.at[slot], sem_x.at[slot]).wait()

    def wait_h(slot):
        pltpu.make_async_copy(h_out.at[pl.ds(0, cap), :], acc.at[slot], sem_h.at[slot]).wait()

    def wait_s(slot):
        pltpu.make_async_copy(acc.at[slot], h_out.at[pl.ds(0, cap), :], sem_s.at[slot]).wait()

    def swiglu(slot):
        x = xbuf[slot, :, :D].astype(BF16)
        a = _dot(x, w1_ref[0])
        g = _dot(x, w3_ref[0])
        return _dot((a * _sigmoid(a) * g).astype(BF16), w2_ref[0])

    def accumulate(slot, y):
        aff = xbuf[slot, :, D:]
        lane = lax.broadcasted_iota(I32, aff.shape, 1)
        gate = jnp.sum(jnp.where(lane == e, aff, 0.0), axis=1, keepdims=True)
        acc[slot] = acc[slot] + y * gate

    @pl.when(first)
    def _():
        def body(r, carry):
            gather_x(idx0_ref, base0, 0, r).start()
            return carry
        lax.fori_loop(0, cap, body, 0, unroll=8)

    wait_x(0)
    for r in range(cap):
        gather_x(idx1_ref, base1, 1, r).start()
        gather_h(idx0_ref, base0, 0, r).start()
    y = swiglu(0)
    wait_h(0)
    accumulate(0, y)
    wait_x(1)
    for r in range(cap):
        scatter_h(idx0_ref, base0, 0, r).start()
        gather_x(idxn_ref, basen, 0, r).start()
        gather_h(idx1_ref, base1, 1, r).start()
    y = swiglu(1)
    wait_h(1)
    accumulate(1, y)

    def body_out(r, carry):
        scatter_h(idx1_ref, base1, 1, r).start()
        return carry

    lax.fori_loop(0, cap, body_out, 0, unroll=8)
    wait_s(0)
    wait_s(1)

    @pl.when(last)
    def _():
        wait_x(0)


def moe_ffn(idx, hx, h, w1, w3, w2, *, B, S):
    T, D = h.shape
    E = N_EXPERTS
    cap = idx.shape[-1]
    F = w1.shape[-1]
    assert B % 2 == 0
    nj = B // 2
    idx3 = idx.reshape(B * E, 1, cap)

    def idx_spec(fn):
        return pl.BlockSpec((1, 1, cap), fn, memory_space=pltpu.SMEM)

    def nxt(e, j):
        is_last = (e == E - 1) & (j == nj - 1)
        en = jnp.where(is_last, e, e + (j + 1) // nj)
        jn = jnp.where(is_last, j, (j + 1) % nj)
        return (2 * jn * E + en, 0, 0)

    return pl.pallas_call(
        functools.partial(_ffn_kernel, S=S, D=D, cap=cap, nj=nj),
        grid=(E, nj),
        in_specs=[idx_spec(lambda e, j: (2 * j * E + e, 0, 0)),
                  idx_spec(lambda e, j: ((2 * j + 1) * E + e, 0, 0)),
                  idx_spec(nxt),
                  pl.BlockSpec(memory_space=pl.ANY),
                  pl.BlockSpec(memory_space=pl.ANY),
                  pl.BlockSpec((1, D, F), lambda e, j: (e, 0, 0)),
                  pl.BlockSpec((1, D, F), lambda e, j: (e, 0, 0)),
                  pl.BlockSpec((1, F, D), lambda e, j: (e, 0, 0))],
        out_specs=pl.BlockSpec(memory_space=pl.ANY),
        out_shape=jax.ShapeDtypeStruct((T, D), F32),
        scratch_shapes=[pltpu.VMEM((2, cap, D + LANES), F32),
                        pltpu.VMEM((2, cap, D), F32),
                        pltpu.SemaphoreType.DMA((2,)),
                        pltpu.SemaphoreType.DMA((2,)),
                        pltpu.SemaphoreType.DMA((2,))],
        input_output_aliases={4: 0},
        compiler_params=_params(("arbitrary", "arbitrary")),
        name="moe_ffn",
    )(idx3, idx3, idx3, hx, h, w1, w3, w2)


def ec_moe(h, gain, router, w1, w3, w2, *, B, S):
    hx, aff_t = moe_router(h, gain, router, B=B, S=S)
    idx = moe_topk(aff_t, B=B, S=S)
    return moe_ffn(idx, hx, h, w1.astype(BF16), w3.astype(BF16), w2.astype(BF16), B=B, S=S)


def _pad_cols(w, n):
    return jnp.pad(w, ((0, 0), (0, n - w.shape[1])))


def gla_layer(h, norm_gain, w_in, w_gate_up, b_gate, head_norm, w_out, *, B, S):
    nmain = 2 * GLA_HEADS * GLA_DK + 2 * GLA_HEADS * GLA_DV
    proj = norm_matmul(h, norm_gain, w_in[:, :nmain].astype(BF16), name="gla_in")
    glo = norm_matmul(h, norm_gain, _pad_cols(w_in[:, nmain:], LANES).astype(BF16), tn=LANES, name="gla_in_gate")
    wg = jnp.pad(w_gate_up.reshape(2 * GLA_RANK, -1), ((0, LANES - 2 * GLA_RANK), (0, 0))).astype(BF16)
    y = gla_scan(proj, glo, wg, b_gate.astype(F32), head_norm, B=B, S=S)
    return matmul_residual(y, w_out.astype(BF16), h, name="gla_out")


def gdn_layer(h, norm_gain, w_in, conv_w, a_log, dt_bias, head_norm, w_out, *, B, S):
    nconv = conv_w.shape[1]
    nmain = nconv + GDN_V_HEADS * GDN_HD
    proj = norm_matmul(h, norm_gain, w_in[:, :nmain].astype(BF16), name="gdn_in")
    ab = norm_matmul(h, norm_gain, w_in[:, nmain:].astype(BF16), tn=LANES, name="gdn_in_gate")
    qkv = gdn_conv(proj, conv_w, B=B, S=S)
    zeros = jnp.zeros((2 * GDN_V_HEADS,), F32)
    alog_row = jnp.concatenate([a_log.astype(F32).reshape(-1), zeros]).reshape(1, LANES)
    dtb_row = jnp.concatenate([dt_bias.astype(F32).reshape(-1), zeros]).reshape(1, LANES)
    y = gdn_scan(qkv, proj, ab, alog_row, dtb_row, head_norm, B=B, S=S)
    return matmul_residual(y, w_out.astype(BF16), h, name="gdn_out")


def diff_layer(h, norm_gain, w_in, q_norm, k_norm, lam, subln, w_out, bias_tiles, layer_idx, *, B, S):
    proj = norm_matmul(h, norm_gain, w_in.astype(BF16), name="diff_in")
    y = diff_attention(proj, q_norm, k_norm, lam, subln, bias_tiles, B=B, S=S, layer_idx=layer_idx)
    return matmul_residual(y, w_out.astype(BF16), h, name="diff_out")


def swa_layer(h, norm_gain, w_in, q_norm, k_norm, sink, w_out, bias, *, B, S):
    proj = norm_matmul(h, norm_gain, w_in.astype(BF16), name="swa_in")
    y = swa_attention(proj, q_norm, k_norm, sink, bias, B=B, S=S)
    return matmul_residual(y, w_out.astype(BF16), h, name="swa_out")


def kernel(x, rel_bias, norm_mix, norm_ffn, gla_w_in, gla_w_gate_up, gla_b_gate, gla_head_norm, gla_w_out, gdn_w_in, gdn_conv, gdn_a_log, gdn_dt_bias, gdn_head_norm, gdn_w_out, diff_w_in, diff_q_norm, diff_k_norm, diff_lambda, diff_subln, diff_w_out, swa_w_in, swa_q_norm, swa_k_norm, swa_sink, swa_w_out, moe_router, moe_w1, moe_w3, moe_w2):
    B, S, D = x.shape
    depth = norm_mix.shape[0]
    h = x.reshape(B * S, D)
    for i in range(depth):
        m, j = i % 4, i // 4
        if m == 0:
            h = gla_layer(h, norm_mix[i], gla_w_in[j], gla_w_gate_up[j], gla_b_gate[j], gla_head_norm[j],
                          gla_w_out[j], B=B, S=S)
        elif m == 1:
            h = gdn_layer(h, norm_mix[i], gdn_w_in[j], gdn_conv[j], gdn_a_log[j], gdn_dt_bias[j],
                          gdn_head_norm[j], gdn_w_out[j], B=B, S=S)
        elif m == 2:
            h = diff_layer(h, norm_mix[i], diff_w_in[j], diff_q_norm[j], diff_k_norm[j], diff_lambda[j],
                           diff_subln[j], diff_w_out[j], diff_bias_tiles(rel_bias), i, B=B, S=S)
        else:
            h = swa_layer(h, norm_mix[i], swa_w_in[j], swa_q_norm[j], swa_k_norm[j], swa_sink[j],
                          swa_w_out[j], swa_bias(rel_bias), B=B, S=S)
        h = ec_moe(h, norm_ffn[i], moe_router[i], moe_w1[i], moe_w3[i], moe_w2[i], B=B, S=S)
    return h.reshape(B, S, D)
```

```python
import functools
import math

import jax
import jax.numpy as jnp
from jax import lax
from jax.experimental import pallas as pl
from jax.experimental.pallas import tpu as pltpu

F32 = jnp.float32
BF16 = jnp.bfloat16
I32 = jnp.int32
HI = lax.Precision.HIGHEST

RMS_EPS = 1e-6
VMEM_LIMIT_BYTES = 56 * 1024 * 1024
LANES = 128

REL_BUCKETS = 32
REL_MAX_DIST = 128
CHUNK = 64
GLA_HEADS, GLA_DK, GLA_DV, GLA_RANK, GLA_TAU = 4, 256, 512, 16, 16.0
GDN_QK_HEADS, GDN_V_HEADS, GDN_HD, GDN_CONV = 16, 32, 128, 5
GDN_REP = GDN_V_HEADS // GDN_QK_HEADS
TRI_ROWS = 256
DIFF_HEADS, DIFF_DQK, DIFF_DV, DIFF_QB = 16, 64, 128, 256
DIFF_KB = 512
SWA_HEADS, SWA_KV, SWA_GROUP, SWA_HD, SWA_W, SWA_QB = 16, 4, 4, 128, 128, 128
N_EXPERTS = 16
EC_CAPACITY_FACTOR = 2
FFN_SLABS = 4


def _params(sem):
    return pltpu.CompilerParams(dimension_semantics=sem, vmem_limit_bytes=VMEM_LIMIT_BYTES)


def _dot(a, b, **kw):
    return jnp.dot(a, b, preferred_element_type=F32, **kw)


def _dot_nt(a, b, **kw):
    return lax.dot_general(a, b, (((1,), (1,)), ((), ())), preferred_element_type=F32, **kw)


def _dot_tn(a, b, **kw):
    return lax.dot_general(a, b, (((0,), (0,)), ((), ())), preferred_element_type=F32, **kw)


def _rms(x):
    return x * lax.rsqrt(jnp.mean(x * x, axis=-1, keepdims=True) + RMS_EPS)


def _sigmoid(x):
    return 1.0 / (1.0 + jnp.exp(-x))


def _softplus(x):
    return jnp.maximum(x, 0.0) + jnp.log(1.0 + jnp.exp(-jnp.abs(x)))


def _norm_matmul_kernel(x_ref, g_ref, w_ref, o_ref, xn_ref):
    @pl.when(pl.program_id(1) == 0)
    def _():
        xn_ref[...] = (_rms(x_ref[...]) * g_ref[...]).astype(BF16)

    o_ref[...] = _dot(xn_ref[...], w_ref[...])


def norm_matmul(x, gain, w, *, tm=1024, tn=512, name):
    T, D = x.shape
    N = w.shape[1]
    tm, tn = min(tm, T), min(tn, N)
    assert T % tm == 0 and N % tn == 0
    return pl.pallas_call(
        _norm_matmul_kernel,
        grid=(T // tm, N // tn),
        in_specs=[pl.BlockSpec((tm, D), lambda i, j: (i, 0)),
                  pl.BlockSpec((1, D), lambda i, j: (0, 0)),
                  pl.BlockSpec((D, tn), lambda i, j: (0, j))],
        out_specs=pl.BlockSpec((tm, tn), lambda i, j: (i, j)),
        out_shape=jax.ShapeDtypeStruct((T, N), F32),
        scratch_shapes=[pltpu.VMEM((tm, D), BF16)],
        compiler_params=_params(("parallel", "arbitrary")),
        name=name,
    )(x, gain.reshape(1, D), w)


def _matmul_res_kernel(y_ref, w_ref, h_ref, o_ref):
    o_ref[...] = h_ref[...] + _dot(y_ref[...].astype(BF16), w_ref[...])


def matmul_residual(y, w, h, *, tm=256, tn=2048, name):
    T, K = y.shape
    N = w.shape[1]
    tm, tn = min(tm, T), min(tn, N)
    assert T % tm == 0 and N % tn == 0
    return pl.pallas_call(
        _matmul_res_kernel,
        grid=(T // tm, N // tn),
        in_specs=[pl.BlockSpec((tm, K), lambda i, j: (i, 0)),
                  pl.BlockSpec((K, tn), lambda i, j: (0, j)),
                  pl.BlockSpec((tm, tn), lambda i, j: (i, j))],
        out_specs=pl.BlockSpec((tm, tn), lambda i, j: (i, j)),
        out_shape=jax.ShapeDtypeStruct((T, N), F32),
        compiler_params=_params(("parallel", "parallel")),
        name=name,
    )(y, w, h)


def _tri_masks(bwd):
    row = lax.broadcasted_iota(I32, (CHUNK, CHUNK), 0)
    col = lax.broadcasted_iota(I32, (CHUNK, CHUNK), 1)
    incl = jnp.where(bwd, (col >= row).astype(F32), (col <= row).astype(F32))
    strict = jnp.where(bwd, (col > row).astype(F32), (col < row).astype(F32))
    return incl, strict


def _gla_kernel(q_ref, k_ref, v_ref, r_ref, glo_ref, wg_ref, bg_ref, hn_ref, trif_ref, trib_ref, o_ref,
                cum_ref, of_ref, st_ref, *, nb, blk):
    i = pl.program_id(2)
    bwd = i >= nb
    sb = jnp.where(bwd, 2 * nb - 1 - i, i)
    nc = blk // CHUNK
    cs = range(nc)

    @pl.when((i == 0) | (i == nb))
    def _():
        st_ref[...] = jnp.zeros_like(st_ref)

    lane = lax.broadcasted_iota(I32, (blk, LANES), 1)
    lo = jnp.where(bwd, GLA_RANK, 0)
    gsel = jnp.where((lane >= lo) & (lane < lo + GLA_RANK), glo_ref[...], 0.0)
    bg = bg_ref[...]
    gate = _dot(gsel.astype(BF16), wg_ref[...]) + jnp.where(bwd, bg[1:2], bg[0:1])
    la = (jnp.minimum(gate, 0.0) - jnp.log(1.0 + jnp.exp(-jnp.abs(gate)))) * (1.0 / GLA_TAU)
    tri = jnp.where(bwd, trib_ref[...], trif_ref[...])
    grp = tri.shape[0]
    pieces = _split3_bf16(la)
    for r in range(0, blk, grp):
        cum_ref[r:r + grp, :] = sum(_dot(tri, pc[r:r + grp]) for pc in pieces)

    incl, _ = _tri_masks(bwd)
    r0 = [pl.multiple_of(jnp.where(bwd, nc - 1 - c, c) * CHUNK, CHUNK) for c in cs]
    rows = [pl.ds(r, CHUNK) for r in r0]
    cum = [cum_ref[rw, :] for rw in rows]
    tot = [cum_ref[pl.ds(r0[c] + jnp.where(bwd, 0, CHUNK - 1), 1), :] for c in cs]
    q = [q_ref[rw, :] * (GLA_DK ** -0.5) for rw in rows]
    k = [k_ref[rw, :] for rw in rows]
    v = [v_ref[rw, :].astype(BF16) for rw in rows]
    qd = [(q[c] * jnp.exp(cum[c])).astype(BF16) for c in cs]
    kin = [(k[c] * jnp.exp(-cum[c])).astype(BF16) for c in cs]
    kst = [(k[c] * jnp.exp(tot[c] - cum[c])).astype(BF16) for c in cs]
    s = [(_dot_nt(qd[c], kin[c]) * incl).astype(BF16) for c in cs]
    o = [_dot(s[c], v[c]) for c in cs]
    st = st_ref[...]
    upd = _dot_tn(v[0], kst[0])
    for c in cs:
        nxt = _dot_tn(v[c + 1], kst[c + 1]) if c + 1 < nc else None
        o[c] = o[c] + _dot_nt(qd[c], st.astype(BF16))
        st = st * jnp.exp(tot[c]) + upd
        upd = nxt
    st_ref[...] = st

    @pl.when(jnp.logical_not(bwd))
    def _():
        for c in cs:
            of_ref[pl.ds(pl.multiple_of(sb * blk + r0[c], CHUNK), CHUNK), :] = o[c]

    @pl.when(bwd)
    def _():
        gain = hn_ref[...]
        for c in cs:
            ot = of_ref[pl.ds(pl.multiple_of(sb * blk + r0[c], CHUNK), CHUNK), :] + o[c]
            r = r_ref[rows[c], :]
            o_ref[rows[c], :] = _rms(ot) * gain * (r * _sigmoid(r))


def gla_scan(proj, glo, wg, bg, head_norm, *, B, S, blk=512):
    T = B * S
    blk = min(blk, S)
    nb = S // blk
    H = GLA_HEADS
    grp = min(TRI_ROWS, blk)

    def rowblk(b, i):
        return b * nb + jnp.where(i >= nb, 2 * nb - 1 - i, i)

    def outblk(b, i):
        return b * nb + jnp.where(i >= nb, 2 * nb - 1 - i, nb - 1)

    kq = GLA_HEADS * GLA_DK // GLA_DK
    kv = 2 * GLA_HEADS * GLA_DK // GLA_DV
    kr = kv + GLA_HEADS
    return pl.pallas_call(
        functools.partial(_gla_kernel, nb=nb, blk=blk),
        grid=(B, H, 2 * nb),
        in_specs=[pl.BlockSpec((blk, GLA_DK), lambda b, h, i: (rowblk(b, i), h)),
                  pl.BlockSpec((blk, GLA_DK), lambda b, h, i: (rowblk(b, i), kq + h)),
                  pl.BlockSpec((blk, GLA_DV), lambda b, h, i: (rowblk(b, i), kv + h)),
                  pl.BlockSpec((blk, GLA_DV), lambda b, h, i: (outblk(b, i), kr + h)),
                  pl.BlockSpec((blk, LANES), lambda b, h, i: (rowblk(b, i), 0)),
                  pl.BlockSpec((LANES, GLA_DK), lambda b, h, i: (0, h)),
                  pl.BlockSpec((2, GLA_DK), lambda b, h, i: (0, h)),
                  pl.BlockSpec((1, GLA_DV), lambda b, h, i: (0, 0)),
                  pl.BlockSpec((grp, grp), lambda b, h, i: (0, 0)),
                  pl.BlockSpec((grp, grp), lambda b, h, i: (0, 0))],
        out_specs=pl.BlockSpec((blk, GLA_DV), lambda b, h, i: (outblk(b, i), h)),
        out_shape=jax.ShapeDtypeStruct((T, H * GLA_DV), F32),
        scratch_shapes=[pltpu.VMEM((blk, GLA_DK), F32),
                        pltpu.VMEM((S, GLA_DV), F32),
                        pltpu.VMEM((GLA_DV, GLA_DK), F32)],
        compiler_params=_params(("parallel", "parallel", "arbitrary")),
        name="gla_scan",
    )(proj, proj, proj, proj, glo, wg, bg, head_norm.reshape(1, GLA_DV),
      _block_tri(grp, False), _block_tri(grp, True))


def _gdn_conv_kernel(x_ref, w_ref, o_ref, xp_ref, *, S, rows):
    c = pl.program_id(1)
    pad = 8
    xp_ref[0:pad, :] = jnp.zeros((pad, LANES), F32)
    xp_ref[pad + S:2 * pad + S, :] = jnp.zeros((pad, LANES), F32)
    xp_ref[pad:pad + S, :] = x_ref[...]
    w = w_ref[...]
    win = rows + 2 * pad
    is_qk = c < 2 * GDN_QK_HEADS
    scale = jnp.where(c < GDN_QK_HEADS, GDN_HD ** -0.5, 1.0)

    def body(t, carry):
        r0 = pl.multiple_of(t * rows, rows)
        xw = xp_ref[pl.ds(r0, win), :]
        acc = jnp.zeros((rows, LANES), F32)
        for j in range(GDN_CONV):
            sh = (GDN_CONV // 2 - j) % win
            xs = xw if sh == 0 else pltpu.roll(xw, sh, 0)
            acc = acc + xs[pad:pad + rows, :] * w[j:j + 1, :]
        y = acc * _sigmoid(acc)
        yn = y * lax.rsqrt(jnp.sum(y * y, axis=-1, keepdims=True) + RMS_EPS) * scale
        o_ref[pl.ds(r0, rows), :] = jnp.where(is_qk, yn, y)
        return carry

    lax.fori_loop(0, S // rows, body, 0)


def gdn_conv(proj, conv_w, *, B, S):
    T = B * S
    nch = conv_w.shape[1] // LANES
    rows = min(256, S)
    return pl.pallas_call(
        functools.partial(_gdn_conv_kernel, S=S, rows=rows),
        grid=(B, nch),
        in_specs=[pl.BlockSpec((S, LANES), lambda b, c: (b, c)),
                  pl.BlockSpec((GDN_CONV, LANES), lambda b, c: (0, c))],
        out_specs=pl.BlockSpec((S, LANES), lambda b, c: (b, c)),
        out_shape=jax.ShapeDtypeStruct((T, nch * LANES), F32),
        scratch_shapes=[pltpu.VMEM((S + 16, LANES), F32)],
        compiler_params=_params(("parallel", "parallel")),
        name="gdn_conv",
    )(proj, conv_w)


def _mm_bf16(a, b):
    return _dot(a.astype(BF16), b.astype(BF16))


def _unit_tri_inverses(Ls):
    row = lax.broadcasted_iota(I32, (CHUNK, CHUNK), 0)
    col = lax.broadcasted_iota(I32, (CHUNK, CHUNK), 1)
    eye = (row == col).astype(F32)
    ps = [eye - L for L in Ls]
    pws = [_mm_bf16(L, L) for L in Ls]
    n = 2
    while True:
        ps = [p + _mm_bf16(p, pw) for p, pw in zip(ps, pws)]
        n *= 2
        if n >= CHUNK:
            return ps
        pws = [_mm_bf16(pw, pw) for pw in pws]


def _split3_bf16(x):
    hi = x.astype(BF16)
    r1 = x - hi.astype(F32)
    mid = r1.astype(BF16)
    lo = (r1 - mid.astype(F32)).astype(BF16)
    return hi, mid, lo


def _gdn_prepare(q_ref, k_ref, v_ref, ab_ref, tri_ref, alog, dtb, *, d, qh, blk):
    bwd = d == 1
    nc = blk // CHUNK
    cs = range(nc)
    hs = range(GDN_REP)
    hc = [(h, c) for h in hs for c in cs]
    row = lax.broadcasted_iota(I32, (CHUNK, CHUNK), 0)
    col = lax.broadcasted_iota(I32, (CHUNK, CHUNK), 1)
    incl = (col >= row) if bwd else (col <= row)
    strict = (col > row) if bwd else (col < row)

    x = ab_ref[...]
    gfull = -jnp.exp(alog) * _softplus(x + dtb)
    bfull = _sigmoid(x)
    lane = lax.broadcasted_iota(I32, (blk, LANES), 1)

    def pick(full, ln):
        return jnp.broadcast_to(jnp.sum(jnp.where(lane == ln, full, 0.0), axis=1, keepdims=True), (blk, LANES))

    lane_g = [d * GDN_V_HEADS + GDN_REP * qh + h for h in hs]
    gb = [pick(gfull, ln) for ln in lane_g]
    bb = [pick(bfull, 2 * GDN_V_HEADS + ln) for ln in lane_g]
    tri = tri_ref[...]
    grp = tri.shape[0]
    pieces = [_split3_bf16(g) for g in gb]
    gc_blk = [jnp.concatenate([sum(_dot(tri, pc[r:r + grp]) for pc in pieces[h]) for r in range(0, blk, grp)], axis=0)
              for h in hs]

    sl = [slice(c * CHUNK, (c + 1) * CHUNK) for c in cs]
    last = [c * CHUNK if bwd else (c + 1) * CHUNK - 1 for c in cs]
    gc = [gc_blk[h][sl[c]] for h, c in hc]
    tot = [gc_blk[h][last[c]:last[c] + 1] for h, c in hc]
    beta = [bb[h][sl[c]] for h, c in hc]
    gamma = [jnp.where(incl, jnp.exp(g[:, :CHUNK] - jnp.transpose(g)[:CHUNK, :]), 0.0) for g in gc]
    q = [q_ref[s, :] for s in sl]
    k = [k_ref[s, :] for s in sl]
    qbf = [t.astype(BF16) for t in q]
    kbf = [t.astype(BF16) for t in k]
    kk = [_dot_nt(kbf[c], kbf[c]) for c in cs]
    qkr = [_dot_nt(qbf[c], kbf[c]) for c in cs]
    tinv = _unit_tri_inverses([jnp.where(strict, kk[c] * beta[j][:, :CHUNK] * gamma[j], 0.0)
                               for j, (h, c) in enumerate(hc)])
    egc = [jnp.exp(g) for g in gc]
    rhs = [jnp.concatenate([v_ref[sl[c], h * GDN_HD:(h + 1) * GDN_HD] * beta[j], k[c] * beta[j] * egc[j]], axis=1)
           for j, (h, c) in enumerate(hc)]
    uw = [_mm_bf16(tinv[j], rhs[j]).astype(BF16) for j in range(len(hc))]
    qk = [(qkr[c] * gamma[j]).astype(BF16) for j, (h, c) in enumerate(hc)]
    kst = [(k[c] * jnp.exp(tot[j] - gc[j])).astype(BF16) for j, (h, c) in enumerate(hc)]
    kuw = [_dot_tn(kst[j], uw[j]) for j in range(len(hc))]
    quw = [_dot(qk[j], uw[j]) for j in range(len(hc))]
    qt = [(q[c] * egc[j] - quw[j][:, GDN_HD:]).astype(BF16) for j, (h, c) in enumerate(hc)]
    return dict(qt=qt, qu=[t[:, :GDN_HD] for t in quw], kw=[t[:, GDN_HD:].astype(BF16) for t in kuw],
                ku=[t[:, :GDN_HD] for t in kuw], dec=[jnp.exp(t) for t in tot])


def _gdn_chunk_step(p, j, S):
    sb16 = S.astype(BF16)
    o = _dot(p["qt"][j], sb16) + p["qu"][j]
    return o, S * p["dec"][j] - _dot(p["kw"][j], sb16) + p["ku"][j]


def _gdn_kernel(qf_ref, kf_ref, vf_ref, abf_ref, qb_ref, kb_ref, vb_ref, abb_ref, trif_ref, trib_ref,
                z_ref, alog_ref, dtb_ref, hn_ref, o_ref, of_ref, ob_ref, st_ref, *, nb, blk, S):
    qh = pl.program_id(1)
    i = pl.program_id(2)
    nc = blk // CHUNK

    @pl.when(i == 0)
    def _():
        st_ref[...] = jnp.zeros_like(st_ref)

    alog = alog_ref[...]
    dtb = dtb_ref[...]
    pf = _gdn_prepare(qf_ref, kf_ref, vf_ref, abf_ref, trif_ref, alog, dtb, d=0, qh=qh, blk=blk)
    pb = _gdn_prepare(qb_ref, kb_ref, vb_ref, abb_ref, trib_ref, alog, dtb, d=1, qh=qh, blk=blk)
    sf = [st_ref[0, h] for h in range(GDN_REP)]
    sb = [st_ref[1, h] for h in range(GDN_REP)]
    rowf = i * blk
    rowb = (nb - 1 - i) * blk
    for c in range(nc):
        cb = nc - 1 - c
        outs = []
        for h in range(GDN_REP):
            of, sf[h] = _gdn_chunk_step(pf, h * nc + c, sf[h])
            ob, sb[h] = _gdn_chunk_step(pb, h * nc + cb, sb[h])
            outs.append((of, ob))
        for h, (of, ob) in enumerate(outs):
            cols = slice(h * GDN_HD, (h + 1) * GDN_HD)
            of_ref[pl.ds(pl.multiple_of(rowf + c * CHUNK, CHUNK), CHUNK), cols] = of
            ob_ref[pl.ds(pl.multiple_of(rowb + cb * CHUNK, CHUNK), CHUNK), cols] = ob
    for h in range(GDN_REP):
        st_ref[0, h] = sf[h]
        st_ref[1, h] = sb[h]

    @pl.when(i == nb - 1)
    def _():
        gain = hn_ref[...]
        rows_e = min(256, S)

        def ebody(t, carry):
            rows = pl.ds(pl.multiple_of(t * rows_e, rows_e), rows_e)
            for h in range(GDN_REP):
                cols = slice(h * GDN_HD, (h + 1) * GDN_HD)
                ot = of_ref[rows, cols] + ob_ref[rows, cols]
                z = z_ref[rows, cols]
                o_ref[rows, cols] = _rms(ot) * gain * (z * _sigmoid(z))
            return carry

        lax.fori_loop(0, S // rows_e, ebody, 0)


def _block_tri(n, bwd):
    r = jnp.arange(n, dtype=I32)[:, None]
    c = jnp.arange(n, dtype=I32)[None, :]
    same = (r // CHUNK) == (c // CHUNK)
    return (same & ((c >= r) if bwd else (c <= r))).astype(BF16)


def gdn_scan(qkv, proj, ab, alog_row, dtb_row, head_norm, *, B, S, blk=512):
    T = B * S
    blk = min(blk, S)
    nb = S // blk
    grp = min(TRI_ROWS, blk)
    vw = GDN_REP * GDN_HD
    voff = 2 * GDN_QK_HEADS * GDN_HD // vw
    zoff = (2 * GDN_QK_HEADS + GDN_V_HEADS) * GDN_HD // vw

    def fwd(b, i):
        return b * nb + i

    def bwd(b, i):
        return b * nb + nb - 1 - i

    def dir_specs(rb):
        return [pl.BlockSpec((blk, GDN_HD), lambda b, h, i: (rb(b, i), h)),
                pl.BlockSpec((blk, GDN_HD), lambda b, h, i: (rb(b, i), GDN_QK_HEADS + h)),
                pl.BlockSpec((blk, vw), lambda b, h, i: (rb(b, i), voff + h)),
                pl.BlockSpec((blk, LANES), lambda b, h, i: (rb(b, i), 0))]

    const2 = lambda b, h, i: (0, 0)
    return pl.pallas_call(
        functools.partial(_gdn_kernel, nb=nb, blk=blk, S=S),
        grid=(B, GDN_QK_HEADS, nb),
        in_specs=dir_specs(fwd) + dir_specs(bwd) + [
            pl.BlockSpec((grp, grp), const2),
            pl.BlockSpec((grp, grp), const2),
            pl.BlockSpec((S, vw), lambda b, h, i: (b, zoff + h)),
            pl.BlockSpec((1, LANES), const2),
            pl.BlockSpec((1, LANES), const2),
            pl.BlockSpec((1, GDN_HD), const2)],
        out_specs=pl.BlockSpec((S, vw), lambda b, h, i: (b, h)),
        out_shape=jax.ShapeDtypeStruct((T, GDN_V_HEADS * GDN_HD), F32),
        scratch_shapes=[pltpu.VMEM((S, vw), F32),
                        pltpu.VMEM((S, vw), F32),
                        pltpu.VMEM((2, GDN_REP, GDN_HD, GDN_HD), F32)],
        compiler_params=_params(("parallel", "parallel", "arbitrary")),
        name="gdn_scan",
    )(qkv, qkv, qkv, ab, qkv, qkv, qkv, ab, _block_tri(grp, False), _block_tri(grp, True),
      proj, alog_row, dtb_row, head_norm.reshape(1, GDN_HD))


def _t5_bucket(rel):
    half = REL_BUCKETS // 2
    max_exact = half // 2
    n = jnp.abs(rel)
    log_ratio = jnp.log(jnp.maximum(n, 1).astype(F32) / max_exact) / math.log(REL_MAX_DIST / max_exact)
    large = jnp.minimum(max_exact + (log_ratio * (half - max_exact)).astype(I32), half - 1)
    return jnp.where(rel > 0, half, 0) + jnp.where(n < max_exact, n, large)


def _rel_bias_heads(table, rel):
    return jnp.moveaxis(table[_t5_bucket(rel)].astype(F32), -1, 0)


def _toeplitz(w, n, m, off):
    lw = w.shape[-1]
    assert lw == n + m - 1 and m <= lw - 1
    w_rot = jnp.roll(w, -off, axis=-1)
    flat = jnp.tile(w_rot, (1,) * (w.ndim - 1) + (n,))[..., :n * (lw - 1)]
    return flat.reshape(w.shape[:-1] + (n, lw - 1))[..., :m]


def _half_rms(x, ones_bd):
    ms = _dot(x * x, ones_bd, precision=HI) * (1.0 / DIFF_DQK)
    return x * lax.rsqrt(ms + RMS_EPS)


LOG2E = math.log2(math.e)


def _diff_kernel(q_ref, k_ref, v_ref, qn_ref, kn_ref, lam_ref, sub_ref, bias_ref, o_ref,
                 kb_ref, vb_ref, m_ref, l_ref, acc_ref, s_ref, *, nk, lambda_init):
    i = pl.program_id(2)
    qb = DIFF_QB
    r = lax.broadcasted_iota(I32, (LANES, LANES), 0) // DIFF_DQK
    c = lax.broadcasted_iota(I32, (LANES, LANES), 1) // DIFF_DQK
    ones_bd = (r == c).astype(F32)

    @pl.when(i == 0)
    def _():
        def kbody(t, carry):
            rows = pl.ds(pl.multiple_of(t * qb, qb), qb)
            kb_ref[rows, :] = (_half_rms(k_ref[rows, :], ones_bd) * kn_ref[...]).astype(BF16)
            vb_ref[rows, :] = v_ref[rows, :].astype(BF16)
            return carry
        lax.fori_loop(0, nk, kbody, 0)

    q = _half_rms(q_ref[...], ones_bd) * qn_ref[...] * (DIFF_DQK ** -0.5 * LOG2E)
    lane = lax.broadcasted_iota(I32, (qb, LANES), 1)
    qs = (jnp.where(lane < DIFF_DQK, q, 0.0).astype(BF16), jnp.where(lane >= DIFF_DQK, q, 0.0).astype(BF16))

    m_ref[...] = jnp.full(m_ref.shape, -jnp.inf, F32)
    l_ref[...] = jnp.zeros(l_ref.shape, F32)
    acc_ref[...] = jnp.zeros(acc_ref.shape, F32)

    kw = DIFF_KB // qb
    maps = range(2)

    nsteps = nk * qb // DIFF_KB

    def key_rows(t):
        start = t * DIFF_KB
        return pl.ds(start if isinstance(t, int) else pl.multiple_of(start, DIFF_KB), DIFF_KB)

    def scores(t, slot):
        kc = kb_ref[key_rows(t), :]
        bias = jnp.concatenate([bias_ref[0, jnp.clip(kw * t + u - i, -2, 2) + 2] for u in range(kw)], axis=1)
        for mi in maps:
            s_ref[slot, mi] = _dot_nt(qs[mi], kc) + bias

    def chunk(t, prefetch):
        slot = t % 2
        s = [s_ref[slot, mi] for mi in maps]
        if prefetch:
            scores(t + 1, 1 - slot)
        vc = vb_ref[key_rows(t), :]
        m_cur = [jnp.max(s[mi], axis=1, keepdims=True) for mi in maps]
        m_prev = [m_ref[mi] for mi in maps]
        m_new = [jnp.maximum(m_prev[mi], m_cur[mi]) for mi in maps]
        alpha = [jnp.exp2(m_prev[mi] - m_new[mi]) for mi in maps]
        p = [jnp.exp2(s[mi] - jnp.concatenate([m_new[mi]] * (DIFF_KB // LANES), axis=1)) for mi in maps]
        psum = [jnp.sum(p[mi], axis=1, keepdims=True) for mi in maps]
        pv = [_dot(p[mi].astype(BF16), vc) for mi in maps]
        for mi in maps:
            l_ref[mi] = alpha[mi] * l_ref[mi] + psum[mi]
            acc_ref[mi] = alpha[mi] * acc_ref[mi] + pv[mi]
            m_ref[mi] = m_new[mi]

    scores(0, 0)
    for t in range(nsteps - 1):
        chunk(t, True)
    chunk(nsteps - 1, False)

    lam = lam_ref[...]
    lam_full = (jnp.exp(jnp.sum(lam[0:1] * lam[1:2], axis=-1, keepdims=True))
                - jnp.exp(jnp.sum(lam[2:3] * lam[3:4], axis=-1, keepdims=True)) + lambda_init)
    o = acc_ref[0] / l_ref[0] - lam_full * (acc_ref[1] / l_ref[1])
    o_ref[...] = _rms(o) * sub_ref[...] * (1.0 - lambda_init)


def diff_attention(proj, q_norm, k_norm, lam, subln, bias_tiles, *, B, S, layer_idx):
    T = B * S
    qb = DIFF_QB
    nq = S // qb
    H = DIFF_HEADS
    lambda_init = 0.8 - 0.6 * math.exp(-0.3 * layer_idx)
    qn2 = jnp.concatenate([q_norm, q_norm]).reshape(1, LANES)
    kn2 = jnp.concatenate([k_norm, k_norm]).reshape(1, LANES)
    return pl.pallas_call(
        functools.partial(_diff_kernel, nk=nq, lambda_init=lambda_init),
        grid=(B, H, nq),
        in_specs=[pl.BlockSpec((qb, LANES), lambda b, h, i: (b * nq + i, h)),
                  pl.BlockSpec((S, LANES), lambda b, h, i: (b, H + h)),
                  pl.BlockSpec((S, LANES), lambda b, h, i: (b, 2 * H + h)),
                  pl.BlockSpec((1, LANES), lambda b, h, i: (0, 0)),
                  pl.BlockSpec((1, LANES), lambda b, h, i: (0, 0)),
                  pl.BlockSpec((4, DIFF_DQK), lambda b, h, i: (0, 0)),
                  pl.BlockSpec((1, DIFF_DV), lambda b, h, i: (0, 0)),
                  pl.BlockSpec((1, 5, qb, qb), lambda b, h, i: (h, 0, 0, 0))],
        out_specs=pl.BlockSpec((qb, DIFF_DV), lambda b, h, i: (b * nq + i, h)),
        out_shape=jax.ShapeDtypeStruct((T, H * DIFF_DV), F32),
        scratch_shapes=[pltpu.VMEM((S, LANES), BF16),
                        pltpu.VMEM((S, DIFF_DV), BF16),
                        pltpu.VMEM((2, qb, LANES), F32),
                        pltpu.VMEM((2, qb, LANES), F32),
                        pltpu.VMEM((2, qb, DIFF_DV), F32),
                        pltpu.VMEM((2, 2, qb, DIFF_KB), F32)],
        compiler_params=_params(("parallel", "parallel", "arbitrary")),
        name="diff_attn",
    )(proj, proj, proj, qn2, kn2, lam, subln.reshape(1, DIFF_DV), bias_tiles)


def diff_bias_tiles(table):
    qb = DIFF_QB
    span = 3 * qb - 1
    vec = _rel_bias_heads(table, jnp.arange(-span, span + 1, dtype=I32)) * LOG2E
    w = jnp.stack([vec[:, (d + 2) * qb:(d + 2) * qb + 2 * qb - 1] for d in range(-2, 3)], axis=1)
    return _toeplitz(w, qb, qb, qb - 1)


def _swa_kernel(q_ref, k0_ref, k1_ref, k2_ref, v0_ref, v1_ref, v2_ref, qn_ref, kn_ref, sink_ref,
                bias_ref, o_ref, *, nq, S):
    i = pl.program_id(2)
    qb = SWA_QB
    span = 3 * qb
    kcat = jnp.concatenate([k0_ref[...], k1_ref[...], k2_ref[...]], axis=0)
    kcat = (_rms(kcat) * kn_ref[...]).astype(BF16)
    vcat = jnp.concatenate([v0_ref[...], v1_ref[...], v2_ref[...]], axis=0).astype(BF16)
    row = lax.broadcasted_iota(I32, (qb, span), 0)
    col = lax.broadcasted_iota(I32, (qb, span), 1)
    rel = col - SWA_W - row
    key_pos = i * qb - SWA_W + col
    valid = (jnp.abs(rel) <= SWA_W) & (key_pos >= 0) & (key_pos < S)
    sink_all = sink_ref[0]
    gs = range(SWA_GROUP)
    q = [q_ref[:, g * SWA_HD:(g + 1) * SWA_HD] for g in gs]
    q = [(_rms(t) * qn_ref[...] * (SWA_HD ** -0.5)).astype(BF16) for t in q]
    s = [_dot_nt(q[g], kcat) + bias_ref[g] for g in gs]
    s = [jnp.where(valid, t, -jnp.inf) for t in s]
    sink = [sink_all[g:g + 1, 0:1] for g in gs]
    m = [jnp.maximum(jnp.max(s[g], axis=-1, keepdims=True), sink[g]) for g in gs]
    p = [jnp.exp(s[g] - m[g]) for g in gs]
    den = [jnp.sum(p[g], axis=-1, keepdims=True) + jnp.exp(sink[g] - m[g]) for g in gs]
    p = [(p[g] / den[g]).astype(BF16) for g in gs]
    o = [_dot(p[g], vcat) for g in gs]
    for g in gs:
        o_ref[:, g * SWA_HD:(g + 1) * SWA_HD] = o[g]


def swa_attention(proj, q_norm, k_norm, sink, bias, *, B, S):
    T = B * S
    qb = SWA_QB
    nq = S // qb
    koff = SWA_HEADS
    voff = SWA_HEADS + SWA_KV
    gw = SWA_GROUP * SWA_HD

    def kvspec(off, d):
        return pl.BlockSpec((qb, SWA_HD),
                            lambda b, kv, i: (b * nq + jnp.clip(i + d, 0, nq - 1), off + kv))

    sink_b = jnp.broadcast_to(sink.astype(F32).reshape(SWA_KV, SWA_GROUP, 1), (SWA_KV, SWA_GROUP, LANES))
    return pl.pallas_call(
        functools.partial(_swa_kernel, nq=nq, S=S),
        grid=(B, SWA_KV, nq),
        in_specs=[pl.BlockSpec((qb, gw), lambda b, kv, i: (b * nq + i, kv)),
                  kvspec(koff, -1), kvspec(koff, 0), kvspec(koff, 1),
                  kvspec(voff, -1), kvspec(voff, 0), kvspec(voff, 1),
                  pl.BlockSpec((1, SWA_HD), lambda b, kv, i: (0, 0)),
                  pl.BlockSpec((1, SWA_HD), lambda b, kv, i: (0, 0)),
                  pl.BlockSpec((1, SWA_GROUP, LANES), lambda b, kv, i: (kv, 0, 0)),
                  pl.BlockSpec((SWA_GROUP, qb, 3 * qb), lambda b, kv, i: (kv, 0, 0))],
        out_specs=pl.BlockSpec((qb, gw), lambda b, kv, i: (b * nq + i, kv)),
        out_shape=jax.ShapeDtypeStruct((T, SWA_HEADS * SWA_HD), F32),
        compiler_params=_params(("parallel", "parallel", "parallel")),
        name="swa_attn",
    )(proj, proj, proj, proj, proj, proj, proj, q_norm.reshape(1, SWA_HD), k_norm.reshape(1, SWA_HD),
      sink_b, bias)


def swa_bias(table):
    qb = SWA_QB
    span = 3 * qb
    vec = _rel_bias_heads(table, jnp.arange(-(qb - 1) - SWA_W, span - SWA_W, dtype=I32))
    return _toeplitz(vec, qb, span, qb - 1)


def _router_kernel(h_ref, g_ref, wr_ref, hx_ref, at_ref, *, D):
    hn = _rms(h_ref[...]) * g_ref[...]
    logits = _dot(hn, wr_ref[...], precision=HI)
    lane = lax.broadcasted_iota(I32, logits.shape, 1)
    logits = jnp.where(lane < N_EXPERTS, logits, -jnp.inf)
    m = jnp.max(logits, axis=-1, keepdims=True)
    e = jnp.exp(logits - m)
    aff = e / jnp.sum(e, axis=-1, keepdims=True)
    hx_ref[:, :D] = hn
    hx_ref[:, D:] = aff
    at_ref[0] = jnp.transpose(aff)


def moe_router(h, gain, router, *, B, S, tm=512):
    T, D = h.shape
    tm = min(tm, S)
    ns = S // tm
    wr = jnp.pad(router.astype(F32), ((0, 0), (0, LANES - N_EXPERTS)))
    return pl.pallas_call(
        functools.partial(_router_kernel, D=D),
        grid=(B, ns),
        in_specs=[pl.BlockSpec((tm, D), lambda b, s: (b * ns + s, 0)),
                  pl.BlockSpec((1, D), lambda b, s: (0, 0)),
                  pl.BlockSpec((D, LANES), lambda b, s: (0, 0))],
        out_specs=[pl.BlockSpec((tm, D + LANES), lambda b, s: (b * ns + s, 0)),
                   pl.BlockSpec((1, LANES, tm), lambda b, s: (b, 0, s))],
        out_shape=[jax.ShapeDtypeStruct((T, D + LANES), F32),
                   jax.ShapeDtypeStruct((B, LANES, S), F32)],
        compiler_params=_params(("parallel", "parallel")),
        name="moe_router",
    )(h, gain.reshape(1, D), wr)


def _topk_kernel(aff_ref, idx_ref, pos_ref, *, S, cap):
    E = N_EXPERTS
    v = aff_ref[0]
    bits = pltpu.bitcast(v, I32)

    def search(_, carry):
        lo, hi = carry
        mid = lo + ((hi - lo) >> 1)
        cnt = jnp.sum((bits >= mid).astype(F32), axis=1, keepdims=True)
        ok = cnt >= cap
        return jnp.where(ok, mid, lo), jnp.where(ok, hi, mid)

    lo0 = jnp.zeros((E, 1), I32)
    hi0 = jnp.full((E, 1), 0x7F800001, I32)
    thr, _ = lax.fori_loop(0, 32, search, (lo0, hi0))
    gt = bits > thr
    eq = bits == thr
    need = cap - jnp.sum(gt.astype(F32), axis=1, keepdims=True)

    r = lax.broadcasted_iota(I32, (LANES, LANES), 0)
    c = lax.broadcasted_iota(I32, (LANES, LANES), 1)
    upper = (r < c).astype(BF16)
    run_e = jnp.zeros((E, 1), F32)
    run_s = jnp.zeros((E, 1), F32)
    sub = 512 // LANES
    for t in range(S // LANES):
        sl = slice(t * LANES, (t + 1) * LANES)
        eq_t = eq[:, sl].astype(F32)
        pe = _dot(eq_t.astype(BF16), upper) + run_e
        sel_t = jnp.where(gt[:, sl], 1.0, jnp.where(pe < need, eq_t, 0.0))
        ps = _dot(sel_t.astype(BF16), upper) + run_s
        pos_ref[t // sub, :, (t % sub) * LANES:(t % sub + 1) * LANES] = jnp.where(sel_t > 0, ps, -1.0)
        run_e = run_e + jnp.sum(eq_t, axis=1, keepdims=True)
        run_s = run_s + jnp.sum(sel_t, axis=1, keepdims=True)

    pi = lax.broadcasted_iota(I32, (cap, 512), 0).astype(F32)
    lane = lax.broadcasted_iota(I32, (8, 512), 1)
    rowv = lax.broadcasted_iota(I32, (8, 512), 0)

    def per_expert(e, carry):
        parts = []
        for t in range(S // 512):
            pos = pos_ref[t, pl.ds(e, 1), :]
            onehot = (pi == pos).astype(BF16)
            tok = t * 512 + lane
            vals = jnp.where(rowv == 0, tok >> 6, jnp.where(rowv == 1, tok & 63, 0)).astype(F32).astype(BF16)
            parts.append(_dot_nt(vals, onehot))
        acc = sum(parts)
        idx_ref[0, pl.ds(e, 1), :] = (acc[0:1] * 64.0 + acc[1:2]).astype(I32)
        return carry

    lax.fori_loop(0, E, per_expert, 0)


def moe_topk(aff_t, *, B, S):
    cap = EC_CAPACITY_FACTOR * S // N_EXPERTS
    return pl.pallas_call(
        functools.partial(_topk_kernel, S=S, cap=cap),
        grid=(B,),
        in_specs=[pl.BlockSpec((1, N_EXPERTS, S), lambda b: (b, 0, 0))],
        out_specs=pl.BlockSpec((1, N_EXPERTS, cap), lambda b: (b, 0, 0)),
        out_shape=jax.ShapeDtypeStruct((B, N_EXPERTS, cap), I32),
        scratch_shapes=[pltpu.VMEM((S // 512, N_EXPERTS, 512), F32)],
        compiler_params=_params(("parallel",)),
        name="moe_topk",
    )(aff_t)


def _ffn_kernel(idx0_ref, idx1_ref, idxn_ref, hx_hbm, h_in, w1_ref, w3_ref, w2_ref, h_out,
                xbuf, acc, sem_x, sem_h, sem_s, *, S, D, cap, nj):
    del h_in
    e = pl.program_id(0)
    j = pl.program_id(1)
    first = (e == 0) & (j == 0)
    last = (e == pl.num_programs(0) - 1) & (j == nj - 1)
    base0 = (2 * j) * S
    base1 = base0 + S
    jn = jnp.where(last, j, (j + 1) % nj)
    basen = (2 * jn) * S

    def gather_x(idx_ref, base, slot, r):
        return pltpu.make_async_copy(hx_hbm.at[pl.ds(base + idx_ref[0, 0, r], 1), :],
                                     xbuf.at[slot, pl.ds(r, 1), :], sem_x.at[slot])

    def gather_h(idx_ref, base, slot, r):
        return pltpu.make_async_copy(h_out.at[pl.ds(base + idx_ref[0, 0, r], 1), :],
                                     acc.at[slot, pl.ds(r, 1), :], sem_h.at[slot])

    def scatter_h(idx_ref, base, slot, r):
        return pltpu.make_async_copy(acc.at[slot, pl.ds(r, 1), :],
                                     h_out.at[pl.ds(base + idx_ref[0, 0, r], 1), :], sem_s.at[slot])

    def wait_x(slot):
        pltpu.make_async_copy(hx_hbm.at[pl.ds(0, cap), :], xbuf.at[slot], sem_x.at[slot]).wait()

    def wait_h(slot):
        pltpu.make_async_copy(h_out.at[pl.ds(0, cap), :], acc.at[slot], sem_h.at[slot]).wait()

    def wait_s(slot):
        pltpu.make_async_copy(acc.at[slot], h_out.at[pl.ds(0, cap), :], sem_s.at[slot]).wait()

    def swiglu(slot, starts):
        F = w1_ref.shape[-1]
        ns = FFN_SLABS
        per = -(-len(starts) // (3 * ns))
        pending = list(starts)

        def issue_some():
            for thunk in pending[:per]:
                thunk()
            del pending[:per]

        x = xbuf[slot, :, :D].astype(BF16)
        fs = F // ns
        a, g = [], []
        for s in range(ns):
            issue_some()
            a.append(_dot(x, w1_ref[0, :, s * fs:(s + 1) * fs]))
        for s in range(ns):
            issue_some()
            g.append(_dot(x, w3_ref[0, :, s * fs:(s + 1) * fs]))
        hm = jnp.concatenate([(a[s] * _sigmoid(a[s]) * g[s]).astype(BF16) for s in range(ns)], axis=1)
        ds_ = D // ns
        y = []
        for s in range(ns):
            issue_some()
            y.append(_dot(hm, w2_ref[0, :, s * ds_:(s + 1) * ds_]))
        for thunk in pending:
            thunk()
        return jnp.concatenate(y, axis=1)

    def accumulate(slot, y):
        aff = xbuf[slot, :, D:]
        lane = lax.broadcasted_iota(I32, aff.shape, 1)
        gate = jnp.sum(jnp.where(lane == e, aff, 0.0), axis=1, keepdims=True)
        acc[slot] = acc[slot] + y * gate

    @pl.when(first)
    def _():
        def body(r, carry):
            gather_x(idx0_ref, base0, 0, r).start()
            return carry
        lax.fori_loop(0, cap, body, 0, unroll=8)

    def start(copy_fn, *args):
        return lambda: copy_fn(*args).start()

    wait_x(0)
    starts = []
    for r in range(cap):
        starts.append(start(gather_h, idx0_ref, base0, 0, r))
        starts.append(start(gather_x, idx1_ref, base1, 1, r))
    y = swiglu(0, starts)
    wait_h(0)
    accumulate(0, y)
    wait_x(1)
    starts = []
    for r in range(cap):
        starts.append(start(gather_h, idx1_ref, base1, 1, r))
        starts.append(start(scatter_h, idx0_ref, base0, 0, r))
        starts.append(start(gather_x, idxn_ref, basen, 0, r))
    y = swiglu(1, starts)
    wait_h(1)
    accumulate(1, y)

    def body_out(r, carry):
        scatter_h(idx1_ref, base1, 1, r).start()
        return carry

    lax.fori_loop(0, cap, body_out, 0, unroll=8)
    wait_s(0)
    wait_s(1)

    @pl.when(last)
    def _():
        wait_x(0)


def moe_ffn(idx, hx, h, w1, w3, w2, *, B, S):
    T, D = h.shape
    E = N_EXPERTS
    cap = idx.shape[-1]
    F = w1.shape[-1]
    assert B % 2 == 0
    nj = B // 2
    idx3 = idx.reshape(B * E, 1, cap)

    def idx_spec(fn):
        return pl.BlockSpec((1, 1, cap), fn, memory_space=pltpu.SMEM)

    def nxt(e, j):
        is_last = (e == E - 1) & (j == nj - 1)
        en = jnp.where(is_last, e, e + (j + 1) // nj)
        jn = jnp.where(is_last, j, (j + 1) % nj)
        return (2 * jn * E + en, 0, 0)

    return pl.pallas_call(
        functools.partial(_ffn_kernel, S=S, D=D, cap=cap, nj=nj),
        grid=(E, nj),
        in_specs=[idx_spec(lambda e, j: (2 * j * E + e, 0, 0)),
                  idx_spec(lambda e, j: ((2 * j + 1) * E + e, 0, 0)),
                  idx_spec(nxt),
                  pl.BlockSpec(memory_space=pl.ANY),
                  pl.BlockSpec(memory_space=pl.ANY),
                  pl.BlockSpec((1, D, F), lambda e, j: (e, 0, 0)),
                  pl.BlockSpec((1, D, F), lambda e, j: (e, 0, 0)),
                  pl.BlockSpec((1, F, D), lambda e, j: (e, 0, 0))],
        out_specs=pl.BlockSpec(memory_space=pl.ANY),
        out_shape=jax.ShapeDtypeStruct((T, D), F32),
        scratch_shapes=[pltpu.VMEM((2, cap, D + LANES), F32),
                        pltpu.VMEM((2, cap, D), F32),
                        pltpu.SemaphoreType.DMA((2,)),
                        pltpu.SemaphoreType.DMA((2,)),
                        pltpu.SemaphoreType.DMA((2,))],
        input_output_aliases={4: 0},
        compiler_params=_params(("arbitrary", "arbitrary")),
        name="moe_ffn",
    )(idx3, idx3, idx3, hx, h, w1, w3, w2)


def ec_moe(h, gain, router, w1, w3, w2, *, B, S):
    hx, aff_t = moe_router(h, gain, router, B=B, S=S)
    idx = moe_topk(aff_t, B=B, S=S)
    return moe_ffn(idx, hx, h, w1.astype(BF16), w3.astype(BF16), w2.astype(BF16), B=B, S=S)


def _pad_cols(w, n):
    return jnp.pad(w, ((0, 0), (0, n - w.shape[1])))


def gla_layer(h, norm_gain, w_in, w_gate_up, b_gate, head_norm, w_out, *, B, S):
    nmain = 2 * GLA_HEADS * GLA_DK + 2 * GLA_HEADS * GLA_DV
    proj = norm_matmul(h, norm_gain, w_in[:, :nmain].astype(BF16), name="gla_in")
    glo = norm_matmul(h, norm_gain, _pad_cols(w_in[:, nmain:], LANES).astype(BF16), tn=LANES, name="gla_in_gate")
    wg = jnp.pad(w_gate_up.reshape(2 * GLA_RANK, -1), ((0, LANES - 2 * GLA_RANK), (0, 0))).astype(BF16)
    y = gla_scan(proj, glo, wg, b_gate.astype(F32), head_norm, B=B, S=S)
    return matmul_residual(y, w_out.astype(BF16), h, name="gla_out")


def gdn_layer(h, norm_gain, w_in, conv_w, a_log, dt_bias, head_norm, w_out, *, B, S):
    nconv = conv_w.shape[1]
    nmain = nconv + GDN_V_HEADS * GDN_HD
    proj = norm_matmul(h, norm_gain, w_in[:, :nmain].astype(BF16), name="gdn_in")
    ab = norm_matmul(h, norm_gain, w_in[:, nmain:].astype(BF16), tn=LANES, name="gdn_in_gate")
    qkv = gdn_conv(proj, conv_w, B=B, S=S)
    zeros = jnp.zeros((2 * GDN_V_HEADS,), F32)
    alog_row = jnp.concatenate([a_log.astype(F32).reshape(-1), zeros]).reshape(1, LANES)
    dtb_row = jnp.concatenate([dt_bias.astype(F32).reshape(-1), zeros]).reshape(1, LANES)
    y = gdn_scan(qkv, proj, ab, alog_row, dtb_row, head_norm, B=B, S=S)
    return matmul_residual(y, w_out.astype(BF16), h, name="gdn_out")


def diff_layer(h, norm_gain, w_in, q_norm, k_norm, lam, subln, w_out, bias_tiles, layer_idx, *, B, S):
    proj = norm_matmul(h, norm_gain, w_in.astype(BF16), name="diff_in")
    y = diff_attention(proj, q_norm, k_norm, lam, subln, bias_tiles, B=B, S=S, layer_idx=layer_idx)
    return matmul_residual(y, w_out.astype(BF16), h, name="diff_out")


def swa_layer(h, norm_gain, w_in, q_norm, k_norm, sink, w_out, bias, *, B, S):
    proj = norm_matmul(h, norm_gain, w_in.astype(BF16), name="swa_in")
    y = swa_attention(proj, q_norm, k_norm, sink, bias, B=B, S=S)
    return matmul_residual(y, w_out.astype(BF16), h, name="swa_out")


def kernel(x, rel_bias, norm_mix, norm_ffn, gla_w_in, gla_w_gate_up, gla_b_gate, gla_head_norm, gla_w_out, gdn_w_in, gdn_conv, gdn_a_log, gdn_dt_bias, gdn_head_norm, gdn_w_out, diff_w_in, diff_q_norm, diff_k_norm, diff_lambda, diff_subln, diff_w_out, swa_w_in, swa_q_norm, swa_k_norm, swa_sink, swa_w_out, moe_router, moe_w1, moe_w3, moe_w2):
    B, S, D = x.shape
    depth = norm_mix.shape[0]
    h = x.reshape(B * S, D)
    for i in range(depth):
        m, j = i % 4, i // 4
        if m == 0:
            h = gla_layer(h, norm_mix[i], gla_w_in[j], gla_w_gate_up[j], gla_b_gate[j], gla_head_norm[j],
                          gla_w_out[j], B=B, S=S)
        elif m == 1:
            h = gdn_layer(h, norm_mix[i], gdn_w_in[j], gdn_conv[j], gdn_a_log[j], gdn_dt_bias[j],
                          gdn_head_norm[j], gdn_w_out[j], B=B, S=S)
        elif m == 2:
            h = diff_layer(h, norm_mix[i], diff_w_in[j], diff_q_norm[j], diff_k_norm[j], diff_lambda[j],
                           diff_subln[j], diff_w_out[j], diff_bias_tiles(rel_bias), i, B=B, S=S)
        else:
            h = swa_layer(h, norm_mix[i], swa_w_in[j], swa_q_norm[j], swa_k_norm[j], swa_sink[j],
                          swa_w_out[j], swa_bias(rel_bias), B=B, S=S)
        h = ec_moe(h, norm_ffn[i], moe_router[i], moe_w1[i], moe_w3[i], moe_w2[i], B=B, S=S)
    return h.reshape(B, S, D)
```

```python
import functools
import math

import jax
import jax.numpy as jnp
from jax import lax
from jax.experimental import pallas as pl
from jax.experimental.pallas import tpu as pltpu

F32 = jnp.float32
BF16 = jnp.bfloat16
I32 = jnp.int32
HI = lax.Precision.HIGHEST

RMS_EPS = 1e-6
VMEM_LIMIT_BYTES = 56 * 1024 * 1024
LANES = 128

REL_BUCKETS = 32
REL_MAX_DIST = 128
CHUNK = 64
GLA_HEADS, GLA_DK, GLA_DV, GLA_RANK, GLA_TAU = 4, 256, 512, 16, 16.0
GDN_QK_HEADS, GDN_V_HEADS, GDN_HD, GDN_CONV = 16, 32, 128, 5
GDN_REP = GDN_V_HEADS // GDN_QK_HEADS
TRI_ROWS = 256
DIFF_HEADS, DIFF_DQK, DIFF_DV, DIFF_QB = 16, 64, 128, 256
DIFF_KB = 512
SWA_HEADS, SWA_KV, SWA_GROUP, SWA_HD, SWA_W, SWA_QB = 16, 4, 4, 128, 128, 128
N_EXPERTS = 16
EC_CAPACITY_FACTOR = 2
FFN_SLABS = 4


def _params(sem):
    return pltpu.CompilerParams(dimension_semantics=sem, vmem_limit_bytes=VMEM_LIMIT_BYTES)


def _dot(a, b, **kw):
    return jnp.dot(a, b, preferred_element_type=F32, **kw)


def _dot_nt(a, b, **kw):
    return lax.dot_general(a, b, (((1,), (1,)), ((), ())), preferred_element_type=F32, **kw)


def _dot_tn(a, b, **kw):
    return lax.dot_general(a, b, (((0,), (0,)), ((), ())), preferred_element_type=F32, **kw)


def _rms(x):
    return x * lax.rsqrt(jnp.mean(x * x, axis=-1, keepdims=True) + RMS_EPS)


def _sigmoid(x):
    return 1.0 / (1.0 + jnp.exp(-x))


def _softplus(x):
    return jnp.maximum(x, 0.0) + jnp.log(1.0 + jnp.exp(-jnp.abs(x)))


def _norm_matmul_kernel(x_ref, g_ref, w_ref, o_ref, xn_ref):
    @pl.when(pl.program_id(1) == 0)
    def _():
        xn_ref[...] = (_rms(x_ref[...]) * g_ref[...]).astype(BF16)

    o_ref[...] = _dot(xn_ref[...], w_ref[...])


def norm_matmul(x, gain, w, *, tm=1024, tn=512, name):
    T, D = x.shape
    N = w.shape[1]
    tm, tn = min(tm, T), min(tn, N)
    assert T % tm == 0 and N % tn == 0
    return pl.pallas_call(
        _norm_matmul_kernel,
        grid=(T // tm, N // tn),
        in_specs=[pl.BlockSpec((tm, D), lambda i, j: (i, 0)),
                  pl.BlockSpec((1, D), lambda i, j: (0, 0)),
                  pl.BlockSpec((D, tn), lambda i, j: (0, j))],
        out_specs=pl.BlockSpec((tm, tn), lambda i, j: (i, j)),
        out_shape=jax.ShapeDtypeStruct((T, N), F32),
        scratch_shapes=[pltpu.VMEM((tm, D), BF16)],
        compiler_params=_params(("parallel", "arbitrary")),
        name=name,
    )(x, gain.reshape(1, D), w)


def _matmul_res_kernel(y_ref, w_ref, h_ref, o_ref):
    o_ref[...] = h_ref[...] + _dot(y_ref[...].astype(BF16), w_ref[...])


def matmul_residual(y, w, h, *, tm=256, tn=2048, name):
    T, K = y.shape
    N = w.shape[1]
    tm, tn = min(tm, T), min(tn, N)
    assert T % tm == 0 and N % tn == 0
    return pl.pallas_call(
        _matmul_res_kernel,
        grid=(T // tm, N // tn),
        in_specs=[pl.BlockSpec((tm, K), lambda i, j: (i, 0)),
                  pl.BlockSpec((K, tn), lambda i, j: (0, j)),
                  pl.BlockSpec((tm, tn), lambda i, j: (i, j))],
        out_specs=pl.BlockSpec((tm, tn), lambda i, j: (i, j)),
        out_shape=jax.ShapeDtypeStruct((T, N), F32),
        compiler_params=_params(("parallel", "parallel")),
        name=name,
    )(y, w, h)


def _tri_masks(bwd):
    row = lax.broadcasted_iota(I32, (CHUNK, CHUNK), 0)
    col = lax.broadcasted_iota(I32, (CHUNK, CHUNK), 1)
    incl = jnp.where(bwd, (col >= row).astype(F32), (col <= row).astype(F32))
    strict = jnp.where(bwd, (col > row).astype(F32), (col < row).astype(F32))
    return incl, strict


def _gla_kernel(q_ref, k_ref, v_ref, r_ref, glo_ref, wg_ref, bg_ref, hn_ref, trif_ref, trib_ref, o_ref,
                of_ref, st_ref, *, nb, blk):
    i = pl.program_id(2)
    nc = blk // CHUNK
    cs = range(nc)

    @pl.when((i == 0) | (i == nb))
    def _():
        st_ref[...] = jnp.zeros_like(st_ref)

    def scan_block(bwd):
        sb = (2 * nb - 1 - i) if bwd else i
        lane = lax.broadcasted_iota(I32, (blk, LANES), 1)
        lo = GLA_RANK if bwd else 0
        gsel = jnp.where((lane >= lo) & (lane < lo + GLA_RANK), glo_ref[...], 0.0)
        bg = bg_ref[...]
        gate = _dot(gsel.astype(BF16), wg_ref[...]) + (bg[1:2] if bwd else bg[0:1])
        la = (jnp.minimum(gate, 0.0) - jnp.log(1.0 + jnp.exp(-jnp.abs(gate)))) * (1.0 / GLA_TAU)
        tri = trib_ref[...] if bwd else trif_ref[...]
        grp = tri.shape[0]
        pieces = _split3_bf16(la)
        cum_blk = jnp.concatenate([sum(_dot(tri, pc[r:r + grp]) for pc in pieces) for r in range(0, blk, grp)],
                                  axis=0)
        row = lax.broadcasted_iota(I32, (CHUNK, CHUNK), 0)
        col = lax.broadcasted_iota(I32, (CHUNK, CHUNK), 1)
        incl = ((col >= row) if bwd else (col <= row)).astype(F32)
        r0 = [(nc - 1 - c if bwd else c) * CHUNK for c in cs]
        rows = [slice(r, r + CHUNK) for r in r0]
        cum = [cum_blk[rw] for rw in rows]
        tot = [cum_blk[(r if bwd else r + CHUNK - 1):(r + 1 if bwd else r + CHUNK)] for r in r0]
        q = [q_ref[rw, :] * (GLA_DK ** -0.5) for rw in rows]
        k = [k_ref[rw, :] for rw in rows]
        v = [v_ref[rw, :].astype(BF16) for rw in rows]
        qd = [(q[c] * jnp.exp(cum[c])).astype(BF16) for c in cs]
        kin = [(k[c] * jnp.exp(-cum[c])).astype(BF16) for c in cs]
        kst = [(k[c] * jnp.exp(tot[c] - cum[c])).astype(BF16) for c in cs]
        s = [(_dot_nt(qd[c], kin[c]) * incl).astype(BF16) for c in cs]
        o = [_dot(s[c], v[c]) for c in cs]
        st = st_ref[...]
        upd = _dot_tn(v[0], kst[0])
        for c in cs:
            nxt = _dot_tn(v[c + 1], kst[c + 1]) if c + 1 < nc else None
            o[c] = o[c] + _dot_nt(qd[c], st.astype(BF16))
            st = st * jnp.exp(tot[c]) + upd
            upd = nxt
        st_ref[...] = st
        gain = hn_ref[...]
        for c in cs:
            grow = pl.ds(pl.multiple_of(sb * blk + r0[c], CHUNK), CHUNK)
            if bwd:
                ot = of_ref[grow, :] + o[c]
                r = r_ref[rows[c], :]
                o_ref[rows[c], :] = _rms(ot) * gain * (r * _sigmoid(r))
            else:
                of_ref[grow, :] = o[c]

    @pl.when(i < nb)
    def _():
        scan_block(False)

    @pl.when(i >= nb)
    def _():
        scan_block(True)


def gla_scan(proj, glo, wg, bg, head_norm, *, B, S, blk=512):
    T = B * S
    blk = min(blk, S)
    nb = S // blk
    H = GLA_HEADS
    grp = min(TRI_ROWS, blk)

    def rowblk(b, i):
        return b * nb + jnp.where(i >= nb, 2 * nb - 1 - i, i)

    def outblk(b, i):
        return b * nb + jnp.where(i >= nb, 2 * nb - 1 - i, nb - 1)

    kq = GLA_HEADS * GLA_DK // GLA_DK
    kv = 2 * GLA_HEADS * GLA_DK // GLA_DV
    kr = kv + GLA_HEADS
    return pl.pallas_call(
        functools.partial(_gla_kernel, nb=nb, blk=blk),
        grid=(B, H, 2 * nb),
        in_specs=[pl.BlockSpec((blk, GLA_DK), lambda b, h, i: (rowblk(b, i), h)),
                  pl.BlockSpec((blk, GLA_DK), lambda b, h, i: (rowblk(b, i), kq + h)),
                  pl.BlockSpec((blk, GLA_DV), lambda b, h, i: (rowblk(b, i), kv + h)),
                  pl.BlockSpec((blk, GLA_DV), lambda b, h, i: (outblk(b, i), kr + h)),
                  pl.BlockSpec((blk, LANES), lambda b, h, i: (rowblk(b, i), 0)),
                  pl.BlockSpec((LANES, GLA_DK), lambda b, h, i: (0, h)),
                  pl.BlockSpec((2, GLA_DK), lambda b, h, i: (0, h)),
                  pl.BlockSpec((1, GLA_DV), lambda b, h, i: (0, 0)),
                  pl.BlockSpec((grp, grp), lambda b, h, i: (0, 0)),
                  pl.BlockSpec((grp, grp), lambda b, h, i: (0, 0))],
        out_specs=pl.BlockSpec((blk, GLA_DV), lambda b, h, i: (outblk(b, i), h)),
        out_shape=jax.ShapeDtypeStruct((T, H * GLA_DV), F32),
        scratch_shapes=[pltpu.VMEM((S, GLA_DV), F32),
                        pltpu.VMEM((GLA_DV, GLA_DK), F32)],
        compiler_params=_params(("parallel", "parallel", "arbitrary")),
        name="gla_scan",
    )(proj, proj, proj, proj, glo, wg, bg, head_norm.reshape(1, GLA_DV),
      _block_tri(grp, False), _block_tri(grp, True))


def _gdn_conv_kernel(x_ref, w_ref, o_ref, xp_ref, *, S, rows):
    c = pl.program_id(1)
    pad = 8
    xp_ref[0:pad, :] = jnp.zeros((pad, LANES), F32)
    xp_ref[pad + S:2 * pad + S, :] = jnp.zeros((pad, LANES), F32)
    xp_ref[pad:pad + S, :] = x_ref[...]
    w = w_ref[...]
    win = rows + 2 * pad
    is_qk = c < 2 * GDN_QK_HEADS
    scale = jnp.where(c < GDN_QK_HEADS, GDN_HD ** -0.5, 1.0)

    def conv_silu(t):
        r0 = pl.multiple_of(t * rows, rows)
        xw = xp_ref[pl.ds(r0, win), :]
        acc = jnp.zeros((rows, LANES), F32)
        for j in range(GDN_CONV):
            sh = (GDN_CONV // 2 - j) % win
            xs = xw if sh == 0 else pltpu.roll(xw, sh, 0)
            acc = acc + xs[pad:pad + rows, :] * w[j:j + 1, :]
        return r0, acc * _sigmoid(acc)

    @pl.when(is_qk)
    def _():
        def body(t, carry):
            r0, y = conv_silu(t)
            o_ref[pl.ds(r0, rows), :] = y * (lax.rsqrt(jnp.sum(y * y, axis=-1, keepdims=True) + RMS_EPS) * scale)
            return carry
        lax.fori_loop(0, S // rows, body, 0, unroll=2)

    @pl.when(jnp.logical_not(is_qk))
    def _():
        def body(t, carry):
            r0, y = conv_silu(t)
            o_ref[pl.ds(r0, rows), :] = y
            return carry
        lax.fori_loop(0, S // rows, body, 0, unroll=2)


def gdn_conv(proj, conv_w, *, B, S):
    T = B * S
    nch = conv_w.shape[1] // LANES
    rows = min(256, S)
    return pl.pallas_call(
        functools.partial(_gdn_conv_kernel, S=S, rows=rows),
        grid=(B, nch),
        in_specs=[pl.BlockSpec((S, LANES), lambda b, c: (b, c)),
                  pl.BlockSpec((GDN_CONV, LANES), lambda b, c: (0, c))],
        out_specs=pl.BlockSpec((S, LANES), lambda b, c: (b, c)),
        out_shape=jax.ShapeDtypeStruct((T, nch * LANES), F32),
        scratch_shapes=[pltpu.VMEM((S + 16, LANES), F32)],
        compiler_params=_params(("parallel", "parallel")),
        name="gdn_conv",
    )(proj, conv_w)


def _mm_bf16(a, b):
    return _dot(a.astype(BF16), b.astype(BF16))


def _unit_tri_inverses(Ls):
    row = lax.broadcasted_iota(I32, (CHUNK, CHUNK), 0)
    col = lax.broadcasted_iota(I32, (CHUNK, CHUNK), 1)
    eye = (row == col).astype(F32)
    ps = [eye - L for L in Ls]
    pws = [_mm_bf16(L, L) for L in Ls]
    n = 2
    while True:
        ps = [p + _mm_bf16(p, pw) for p, pw in zip(ps, pws)]
        n *= 2
        if n >= CHUNK:
            return ps
        pws = [_mm_bf16(pw, pw) for pw in pws]


def _split3_bf16(x):
    hi = x.astype(BF16)
    r1 = x - hi.astype(F32)
    mid = r1.astype(BF16)
    lo = (r1 - mid.astype(F32)).astype(BF16)
    return hi, mid, lo


def _gdn_prepare(q_ref, k_ref, v_ref, ab_ref, tri_ref, alog, dtb, *, d, qh, blk):
    bwd = d == 1
    nc = blk // CHUNK
    cs = range(nc)
    hs = range(GDN_REP)
    hc = [(h, c) for h in hs for c in cs]
    row = lax.broadcasted_iota(I32, (CHUNK, CHUNK), 0)
    col = lax.broadcasted_iota(I32, (CHUNK, CHUNK), 1)
    incl = (col >= row) if bwd else (col <= row)
    strict = (col > row) if bwd else (col < row)

    sl = [slice(c * CHUNK, (c + 1) * CHUNK) for c in cs]
    q = [q_ref[s, :] for s in sl]
    k = [k_ref[s, :] for s in sl]
    qbf = [t.astype(BF16) for t in q]
    kbf = [t.astype(BF16) for t in k]
    kk = [_dot_nt(kbf[c], kbf[c]) for c in cs]
    qkr = [_dot_nt(qbf[c], kbf[c]) for c in cs]
    x = ab_ref[...]
    gfull = -jnp.exp(alog) * _softplus(x + dtb)
    bfull = _sigmoid(x)
    lane = lax.broadcasted_iota(I32, (blk, LANES), 1)

    def pick(full, ln):
        return jnp.broadcast_to(jnp.sum(jnp.where(lane == ln, full, 0.0), axis=1, keepdims=True), (blk, LANES))

    lane_g = [d * GDN_V_HEADS + GDN_REP * qh + h for h in hs]
    gb = [pick(gfull, ln) for ln in lane_g]
    bb = [pick(bfull, 2 * GDN_V_HEADS + ln) for ln in lane_g]
    tri = tri_ref[...]
    grp = tri.shape[0]
    pieces = [_split3_bf16(g) for g in gb]
    gc_blk = [jnp.concatenate([sum(_dot(tri, pc[r:r + grp]) for pc in pieces[h]) for r in range(0, blk, grp)], axis=0)
              for h in hs]

    last = [c * CHUNK if bwd else (c + 1) * CHUNK - 1 for c in cs]
    gc = [gc_blk[h][sl[c]] for h, c in hc]
    tot = [gc_blk[h][last[c]:last[c] + 1] for h, c in hc]
    beta = [bb[h][sl[c]] for h, c in hc]
    gamma = [jnp.where(incl, jnp.exp(g[:, :CHUNK] - jnp.transpose(g)[:CHUNK, :]), 0.0) for g in gc]
    tinv =_unit_tri_inverses([jnp.where(strict, kk[c] * beta[j][:, :CHUNK] * gamma[j], 0.0)
                               for j, (h, c) in enumerate(hc)])
    egc = [jnp.exp(g) for g in gc]
    rhs = [jnp.concatenate([v_ref[sl[c], h * GDN_HD:(h + 1) * GDN_HD] * beta[j], k[c] * beta[j] * egc[j]], axis=1)
           for j, (h, c) in enumerate(hc)]
    uw = [_mm_bf16(tinv[j], rhs[j]).astype(BF16) for j in range(len(hc))]
    qk = [(qkr[c] * gamma[j]).astype(BF16) for j, (h, c) in enumerate(hc)]
    kst = [(k[c] * jnp.exp(tot[j] - gc[j])).astype(BF16) for j, (h, c) in enumerate(hc)]
    qd = [q[c] * egc[j] for j, (h, c) in enumerate(hc)]
    return dict(uw=uw, qk=qk, kst=kst, qd=qd, dec=[jnp.exp(t) for t in tot])


def _gdn_chunk_operands(p, j):
    kuw = _dot_tn(p["kst"][j], p["uw"][j])
    quw = _dot(p["qk"][j], p["uw"][j])
    qt = (p["qd"][j] - quw[:, GDN_HD:]).astype(BF16)
    return qt, quw[:, :GDN_HD], kuw[:, GDN_HD:].astype(BF16), kuw[:, :GDN_HD], p["dec"][j]


def _gdn_chunk_step(ops, S):
    qt, qu, kw, ku, dec = ops
    sb16 = S.astype(BF16)
    return _dot(qt, sb16) + qu, S * dec - _dot(kw, sb16) + ku


def _gdn_kernel(qf_ref, kf_ref, vf_ref, abf_ref, qb_ref, kb_ref, vb_ref, abb_ref, trif_ref, trib_ref,
                z_ref, alog_ref, dtb_ref, hn_ref, o_ref, of_ref, ob_ref, st_ref, *, nb, blk, S):
    qh = pl.program_id(1)
    i = pl.program_id(2)
    nc = blk // CHUNK

    @pl.when(i == 0)
    def _():
        st_ref[...] = jnp.zeros_like(st_ref)

    alog = alog_ref[...]
    dtb = dtb_ref[...]
    pf = _gdn_prepare(qf_ref, kf_ref, vf_ref, abf_ref, trif_ref, alog, dtb, d=0, qh=qh, blk=blk)
    pb = _gdn_prepare(qb_ref, kb_ref, vb_ref, abb_ref, trib_ref, alog, dtb, d=1, qh=qh, blk=blk)
    sf = [st_ref[0, h] for h in range(GDN_REP)]
    sb = [st_ref[1, h] for h in range(GDN_REP)]
    rowf = i * blk
    rowb = (nb - 1 - i) * blk
    def operands(c):
        return [(_gdn_chunk_operands(pf, h * nc + c), _gdn_chunk_operands(pb, h * nc + nc - 1 - c))
                for h in range(GDN_REP)]

    nxt = operands(0)
    for c in range(nc):
        cb = nc - 1 - c
        cur = nxt
        if c + 1 < nc:
            nxt = operands(c + 1)
        outs = []
        for h in range(GDN_REP):
            of, sf[h] = _gdn_chunk_step(cur[h][0], sf[h])
            ob, sb[h] = _gdn_chunk_step(cur[h][1], sb[h])
            outs.append((of, ob))
        for h, (of, ob) in enumerate(outs):
            cols = slice(h * GDN_HD, (h + 1) * GDN_HD)
            of_ref[pl.ds(pl.multiple_of(rowf + c * CHUNK, CHUNK), CHUNK), cols] = of
            ob_ref[pl.ds(pl.multiple_of(rowb + cb * CHUNK, CHUNK), CHUNK), cols] = ob
    for h in range(GDN_REP):
        st_ref[0, h] = sf[h]
        st_ref[1, h] = sb[h]

    @pl.when(i == nb - 1)
    def _():
        gain = hn_ref[...]
        rows_e = min(256, S)

        def ebody(t, carry):
            rows = pl.ds(pl.multiple_of(t * rows_e, rows_e), rows_e)
            for h in range(GDN_REP):
                cols = slice(h * GDN_HD, (h + 1) * GDN_HD)
                ot = of_ref[rows, cols] + ob_ref[rows, cols]
                z = z_ref[rows, cols]
                o_ref[rows, cols] = _rms(ot) * gain * (z * _sigmoid(z))
            return carry

        lax.fori_loop(0, S // rows_e, ebody, 0)


def _block_tri(n, bwd):
    r = jnp.arange(n, dtype=I32)[:, None]
    c = jnp.arange(n, dtype=I32)[None, :]
    same = (r // CHUNK) == (c // CHUNK)
    return (same & ((c >= r) if bwd else (c <= r))).astype(BF16)


def gdn_scan(qkv, proj, ab, alog_row, dtb_row, head_norm, *, B, S, blk=512):
    T = B * S
    blk = min(blk, S)
    nb = S // blk
    grp = min(TRI_ROWS, blk)
    vw = GDN_REP * GDN_HD
    voff = 2 * GDN_QK_HEADS * GDN_HD // vw
    zoff = (2 * GDN_QK_HEADS + GDN_V_HEADS) * GDN_HD // vw

    def fwd(b, i):
        return b * nb + i

    def bwd(b, i):
        return b * nb + nb - 1 - i

    def dir_specs(rb):
        return [pl.BlockSpec((blk, GDN_HD), lambda b, h, i: (rb(b, i), h)),
                pl.BlockSpec((blk, GDN_HD), lambda b, h, i: (rb(b, i), GDN_QK_HEADS + h)),
                pl.BlockSpec((blk, vw), lambda b, h, i: (rb(b, i), voff + h)),
                pl.BlockSpec((blk, LANES), lambda b, h, i: (rb(b, i), 0))]

    const2 = lambda b, h, i: (0, 0)
    return pl.pallas_call(
        functools.partial(_gdn_kernel, nb=nb, blk=blk, S=S),
        grid=(B, GDN_QK_HEADS, nb),
        in_specs=dir_specs(fwd) + dir_specs(bwd) + [
            pl.BlockSpec((grp, grp), const2),
            pl.BlockSpec((grp, grp), const2),
            pl.BlockSpec((S, vw), lambda b, h, i: (b, zoff + h)),
            pl.BlockSpec((1, LANES), const2),
            pl.BlockSpec((1, LANES), const2),
            pl.BlockSpec((1, GDN_HD), const2)],
        out_specs=pl.BlockSpec((S, vw), lambda b, h, i: (b, h)),
        out_shape=jax.ShapeDtypeStruct((T, GDN_V_HEADS * GDN_HD), F32),
        scratch_shapes=[pltpu.VMEM((S, vw), F32),
                        pltpu.VMEM((S, vw), F32),
                        pltpu.VMEM((2, GDN_REP, GDN_HD, GDN_HD), F32)],
        compiler_params=_params(("parallel", "parallel", "arbitrary")),
        name="gdn_scan",
    )(qkv, qkv, qkv, ab, qkv, qkv, qkv, ab, _block_tri(grp, False), _block_tri(grp, True),
      proj, alog_row, dtb_row, head_norm.reshape(1, GDN_HD))


def _t5_bucket(rel):
    half = REL_BUCKETS // 2
    max_exact = half // 2
    n = jnp.abs(rel)
    log_ratio = jnp.log(jnp.maximum(n, 1).astype(F32) / max_exact) / math.log(REL_MAX_DIST / max_exact)
    large = jnp.minimum(max_exact + (log_ratio * (half - max_exact)).astype(I32), half - 1)
    return jnp.where(rel > 0, half, 0) + jnp.where(n < max_exact, n, large)


def _rel_bias_heads(table, rel):
    return jnp.moveaxis(table[_t5_bucket(rel)].astype(F32), -1, 0)


def _toeplitz(w, n, m, off):
    lw = w.shape[-1]
    assert lw == n + m - 1 and m <= lw - 1
    w_rot = jnp.roll(w, -off, axis=-1)
    flat = jnp.tile(w_rot, (1,) * (w.ndim - 1) + (n,))[..., :n * (lw - 1)]
    return flat.reshape(w.shape[:-1] + (n, lw - 1))[..., :m]


def _half_rms(x, ones_bd):
    ms = _dot(x * x, ones_bd, precision=HI) * (1.0 / DIFF_DQK)
    return x * lax.rsqrt(ms + RMS_EPS)


LOG2E = math.log2(math.e)


def _diff_kernel(q_ref, k_ref, v_ref, qn_ref, kn_ref, lam_ref, sub_ref, bias_ref, o_ref,
                 kb_ref, vb_ref, m_ref, l_ref, acc_ref, s_ref, *, nk, lambda_init):
    i = pl.program_id(2)
    qb = DIFF_QB
    r = lax.broadcasted_iota(I32, (LANES, LANES), 0) // DIFF_DQK
    c = lax.broadcasted_iota(I32, (LANES, LANES), 1) // DIFF_DQK
    ones_bd = (r == c).astype(F32)

    @pl.when(i == 0)
    def _():
        def kbody(t, carry):
            rows = pl.ds(pl.multiple_of(t * qb, qb), qb)
            kb_ref[rows, :] = (_half_rms(k_ref[rows, :], ones_bd) * kn_ref[...]).astype(BF16)
            vb_ref[rows, :] = v_ref[rows, :].astype(BF16)
            return carry
        lax.fori_loop(0, nk, kbody, 0)

    q = _half_rms(q_ref[...], ones_bd) * qn_ref[...] * (DIFF_DQK ** -0.5 * LOG2E)
    lane = lax.broadcasted_iota(I32, (qb, LANES), 1)
    qs = (jnp.where(lane < DIFF_DQK, q, 0.0).astype(BF16), jnp.where(lane >= DIFF_DQK, q, 0.0).astype(BF16))

    m_ref[...] = jnp.full(m_ref.shape, -jnp.inf, F32)
    l_ref[...] = jnp.zeros(l_ref.shape, F32)
    acc_ref[...] = jnp.zeros(acc_ref.shape, F32)

    kw = DIFF_KB // qb
    maps = range(2)

    nsteps = nk * qb // DIFF_KB

    def key_rows(t):
        return pl.ds(t * DIFF_KB, DIFF_KB)

    def scores(t, slot):
        kc = kb_ref[key_rows(t), :]
        bias = jnp.concatenate([bias_ref[0, jnp.clip(kw * t + u - i, -2, 2) + 2] for u in range(kw)], axis=1)
        for mi in maps:
            s_ref[slot, mi] = _dot_nt(qs[mi], kc) + bias

    def chunk(t, prefetch):
        slot = t % 2
        s = [s_ref[slot, mi] for mi in maps]
        if prefetch:
            scores(t + 1, 1 - slot)
        vc = vb_ref[key_rows(t), :]
        m_cur = [jnp.max(s[mi], axis=1, keepdims=True) for mi in maps]
        m_prev = [m_ref[mi] for mi in maps]
        m_new = [jnp.maximum(m_prev[mi], m_cur[mi]) for mi in maps]
        alpha = [jnp.exp2(m_prev[mi] - m_new[mi]) for mi in maps]
        p = [jnp.exp2(s[mi] - jnp.concatenate([m_new[mi]] * (DIFF_KB // LANES), axis=1)) for mi in maps]
        psum = [jnp.sum(p[mi], axis=1, keepdims=True) for mi in maps]
        pv = [_dot(p[mi].astype(BF16), vc) for mi in maps]
        for mi in maps:
            l_ref[mi] = alpha[mi] * l_ref[mi] + psum[mi]
            acc_ref[mi] = alpha[mi] * acc_ref[mi] + pv[mi]
            m_ref[mi] = m_new[mi]

    scores(0, 0)
    for t in range(nsteps - 1):
        chunk(t, True)
    chunk(nsteps - 1, False)

    lam = lam_ref[...]
    lam_full = (jnp.exp(jnp.sum(lam[0:1] * lam[1:2], axis=-1, keepdims=True))
                - jnp.exp(jnp.sum(lam[2:3] * lam[3:4], axis=-1, keepdims=True)) + lambda_init)
    o = acc_ref[0] / l_ref[0] - lam_full * (acc_ref[1] / l_ref[1])
    o_ref[...] = _rms(o) * sub_ref[...] * (1.0 - lambda_init)


def diff_attention(proj, q_norm, k_norm, lam, subln, bias_tiles, *, B, S, layer_idx):
    T = B * S
    qb = DIFF_QB
    nq = S // qb
    H = DIFF_HEADS
    lambda_init = 0.8 - 0.6 * math.exp(-0.3 * layer_idx)
    qn2 = jnp.concatenate([q_norm, q_norm]).reshape(1, LANES)
    kn2 = jnp.concatenate([k_norm, k_norm]).reshape(1, LANES)
    return pl.pallas_call(
        functools.partial(_diff_kernel, nk=nq, lambda_init=lambda_init),
        grid=(B, H, nq),
        in_specs=[pl.BlockSpec((qb, LANES), lambda b, h, i: (b * nq + i, h)),
                  pl.BlockSpec((S, LANES), lambda b, h, i: (b, H + h)),
                  pl.BlockSpec((S, LANES), lambda b, h, i: (b, 2 * H + h)),
                  pl.BlockSpec((1, LANES), lambda b, h, i: (0, 0)),
                  pl.BlockSpec((1, LANES), lambda b, h, i: (0, 0)),
                  pl.BlockSpec((4, DIFF_DQK), lambda b, h, i: (0, 0)),
                  pl.BlockSpec((1, DIFF_DV), lambda b, h, i: (0, 0)),
                  pl.BlockSpec((1, 5, qb, qb), lambda b, h, i: (h, 0, 0, 0))],
        out_specs=pl.BlockSpec((qb, DIFF_DV), lambda b, h, i: (b * nq + i, h)),
        out_shape=jax.ShapeDtypeStruct((T, H * DIFF_DV), F32),
        scratch_shapes=[pltpu.VMEM((S, LANES), BF16),
                        pltpu.VMEM((S, DIFF_DV), BF16),
                        pltpu.VMEM((2, qb, LANES), F32),
                        pltpu.VMEM((2, qb, LANES), F32),
                        pltpu.VMEM((2, qb, DIFF_DV), F32),
                        pltpu.VMEM((2, 2, qb, DIFF_KB), F32)],
        compiler_params=_params(("parallel", "parallel", "arbitrary")),
        name="diff_attn",
    )(proj, proj, proj, qn2, kn2, lam, subln.reshape(1, DIFF_DV), bias_tiles)


def diff_bias_tiles(table):
    qb = DIFF_QB
    span = 3 * qb - 1
    vec = _rel_bias_heads(table, jnp.arange(-span, span + 1, dtype=I32)) * LOG2E
    w = jnp.stack([vec[:, (d + 2) * qb:(d + 2) * qb + 2 * qb - 1] for d in range(-2, 3)], axis=1)
    return _toeplitz(w, qb, qb, qb - 1)


def _swa_kernel(q_ref, k0_ref, k1_ref, k2_ref, v0_ref, v1_ref, v2_ref, qn_ref, kn_ref, sink_ref,
                bias_ref, o_ref, *, nq, S):
    i = pl.program_id(2)
    qb = SWA_QB
    span = 3 * qb
    kcat = jnp.concatenate([k0_ref[...], k1_ref[...], k2_ref[...]], axis=0)
    kcat = (_rms(kcat) * kn_ref[...]).astype(BF16)
    vcat = jnp.concatenate([v0_ref[...], v1_ref[...], v2_ref[...]], axis=0).astype(BF16)
    row = lax.broadcasted_iota(I32, (qb, span), 0)
    col = lax.broadcasted_iota(I32, (qb, span), 1)
    rel = col - SWA_W - row
    key_pos = i * qb - SWA_W + col
    valid = (jnp.abs(rel) <= SWA_W) & (key_pos >= 0) & (key_pos < S)
    sink_all = sink_ref[0]
    gs = range(SWA_GROUP)
    q = [q_ref[:, g * SWA_HD:(g + 1) * SWA_HD] for g in gs]
    q = [(_rms(t) * qn_ref[...] * (SWA_HD ** -0.5)).astype(BF16) for t in q]
    s = [_dot_nt(q[g], kcat) + bias_ref[g] for g in gs]
    s = [jnp.where(valid, t, -jnp.inf) for t in s]
    sink = [sink_all[g:g + 1, 0:1] for g in gs]
    m = [jnp.maximum(jnp.max(s[g], axis=-1, keepdims=True), sink[g]) for g in gs]
    p = [jnp.exp(s[g] - m[g]) for g in gs]
    den = [jnp.sum(p[g], axis=-1, keepdims=True) + jnp.exp(sink[g] - m[g]) for g in gs]
    p = [(p[g] / den[g]).astype(BF16) for g in gs]
    o = [_dot(p[g], vcat) for g in gs]
    for g in gs:
        o_ref[:, g * SWA_HD:(g + 1) * SWA_HD] = o[g]


def swa_attention(proj, q_norm, k_norm, sink, bias, *, B, S):
    T = B * S
    qb = SWA_QB
    nq = S // qb
    koff = SWA_HEADS
    voff = SWA_HEADS + SWA_KV
    gw = SWA_GROUP * SWA_HD

    def kvspec(off, d):
        return pl.BlockSpec((qb, SWA_HD),
                            lambda b, kv, i: (b * nq + jnp.clip(i + d, 0, nq - 1), off + kv))

    sink_b = jnp.broadcast_to(sink.astype(F32).reshape(SWA_KV, SWA_GROUP, 1), (SWA_KV, SWA_GROUP, LANES))
    return pl.pallas_call(
        functools.partial(_swa_kernel, nq=nq, S=S),
        grid=(B, SWA_KV, nq),
        in_specs=[pl.BlockSpec((qb, gw), lambda b, kv, i: (b * nq + i, kv)),
                  kvspec(koff, -1), kvspec(koff, 0), kvspec(koff, 1),
                  kvspec(voff, -1), kvspec(voff, 0), kvspec(voff, 1),
                  pl.BlockSpec((1, SWA_HD), lambda b, kv, i: (0, 0)),
                  pl.BlockSpec((1, SWA_HD), lambda b, kv, i: (0, 0)),
                  pl.BlockSpec((1, SWA_GROUP, LANES), lambda b, kv, i: (kv, 0, 0)),
                  pl.BlockSpec((SWA_GROUP, qb, 3 * qb), lambda b, kv, i: (kv, 0, 0))],
        out_specs=pl.BlockSpec((qb, gw), lambda b, kv, i: (b * nq + i, kv)),
        out_shape=jax.ShapeDtypeStruct((T, SWA_HEADS * SWA_HD), F32),
        compiler_params=_params(("parallel", "parallel", "parallel")),
        name="swa_attn",
    )(proj, proj, proj, proj, proj, proj, proj, q_norm.reshape(1, SWA_HD), k_norm.reshape(1, SWA_HD),
      sink_b, bias)


def swa_bias(table):
    qb = SWA_QB
    span = 3 * qb
    vec = _rel_bias_heads(table, jnp.arange(-(qb - 1) - SWA_W, span - SWA_W, dtype=I32))
    return _toeplitz(vec, qb, span, qb - 1)


def _router_kernel(h_ref, g_ref, wr_ref, hx_ref, at_ref, *, D):
    hn = _rms(h_ref[...]) * g_ref[...]
    logits = _dot(hn, wr_ref[...], precision=HI)
    lane = lax.broadcasted_iota(I32, logits.shape, 1)
    logits = jnp.where(lane < N_EXPERTS, logits, -jnp.inf)
    m = jnp.max(logits, axis=-1, keepdims=True)
    e = jnp.exp(logits - m)
    aff = e / jnp.sum(e, axis=-1, keepdims=True)
    hx_ref[:, :D] = hn
    hx_ref[:, D:] = aff
    at_ref[0] = jnp.transpose(aff)


def moe_router(h, gain, router, *, B, S, tm=512):
    T, D = h.shape
    tm = min(tm, S)
    ns = S // tm
    wr = jnp.pad(router.astype(F32), ((0, 0), (0, LANES - N_EXPERTS)))
    return pl.pallas_call(
        functools.partial(_router_kernel, D=D),
        grid=(B, ns),
        in_specs=[pl.BlockSpec((tm, D), lambda b, s: (b * ns + s, 0)),
                  pl.BlockSpec((1, D), lambda b, s: (0, 0)),
                  pl.BlockSpec((D, LANES), lambda b, s: (0, 0))],
        out_specs=[pl.BlockSpec((tm, D + LANES), lambda b, s: (b * ns + s, 0)),
                   pl.BlockSpec((1, LANES, tm), lambda b, s: (b, 0, s))],
        out_shape=[jax.ShapeDtypeStruct((T, D + LANES), F32),
                   jax.ShapeDtypeStruct((B, LANES, S), F32)],
        compiler_params=_params(("parallel", "parallel")),
        name="moe_router",
    )(h, gain.reshape(1, D), wr)


def _topk_kernel(aff_ref, idx_ref, pos_ref, *, S, cap):
    E = N_EXPERTS
    v = aff_ref[0]
    bits = pltpu.bitcast(v, I32)

    def search(_, carry):
        lo, hi = carry
        mid = lo + ((hi - lo) >> 1)
        cnt = jnp.sum((bits >= mid).astype(F32), axis=1, keepdims=True)
        ok = cnt >= cap
        return jnp.where(ok, mid, lo), jnp.where(ok, hi, mid)

    lo0 = jnp.zeros((E, 1), I32)
    hi0 = jnp.full((E, 1), 0x7F800001, I32)
    thr, _ = lax.fori_loop(0, 32, search, (lo0, hi0))
    gt = bits > thr
    eq = bits == thr
    need = cap - jnp.sum(gt.astype(F32), axis=1, keepdims=True)

    r = lax.broadcasted_iota(I32, (LANES, LANES), 0)
    c = lax.broadcasted_iota(I32, (LANES, LANES), 1)
    upper = (r < c).astype(BF16)
    run_e = jnp.zeros((E, 1), F32)
    run_s = jnp.zeros((E, 1), F32)
    sub = 512 // LANES
    for t in range(S // LANES):
        sl = slice(t * LANES, (t + 1) * LANES)
        eq_t = eq[:, sl].astype(F32)
        pe = _dot(eq_t.astype(BF16), upper) + run_e
        sel_t = jnp.where(gt[:, sl], 1.0, jnp.where(pe < need, eq_t, 0.0))
        ps = _dot(sel_t.astype(BF16), upper) + run_s
        pos_ref[t // sub, :, (t % sub) * LANES:(t % sub + 1) * LANES] = jnp.where(sel_t > 0, ps, -1.0)
        run_e = run_e + jnp.sum(eq_t, axis=1, keepdims=True)
        run_s = run_s + jnp.sum(sel_t, axis=1, keepdims=True)

    pi = lax.broadcasted_iota(I32, (cap, 512), 0).astype(F32)
    lane = lax.broadcasted_iota(I32, (8, 512), 1)
    rowv = lax.broadcasted_iota(I32, (8, 512), 0)

    def per_expert(e, carry):
        parts = []
        for t in range(S // 512):
            pos = pos_ref[t, pl.ds(e, 1), :]
            onehot = (pi == pos).astype(BF16)
            tok = t * 512 + lane
            vals = jnp.where(rowv == 0, tok >> 6, jnp.where(rowv == 1, tok & 63, 0)).astype(F32).astype(BF16)
            parts.append(_dot_nt(vals, onehot))
        acc = sum(parts)
        idx_ref[0, pl.ds(e, 1), :] = (acc[0:1] * 64.0 + acc[1:2]).astype(I32)
        return carry

    lax.fori_loop(0, E, per_expert, 0)


def moe_topk(aff_t, *, B, S):
    cap = EC_CAPACITY_FACTOR * S // N_EXPERTS
    return pl.pallas_call(
        functools.partial(_topk_kernel, S=S, cap=cap),
        grid=(B,),
        in_specs=[pl.BlockSpec((1, N_EXPERTS, S), lambda b: (b, 0, 0))],
        out_specs=pl.BlockSpec((1, N_EXPERTS, cap), lambda b: (b, 0, 0)),
        out_shape=jax.ShapeDtypeStruct((B, N_EXPERTS, cap), I32),
        scratch_shapes=[pltpu.VMEM((S // 512, N_EXPERTS, 512), F32)],
        compiler_params=_params(("parallel",)),
        name="moe_topk",
    )(aff_t)


def _ffn_kernel(idx0_ref, idx1_ref, idxn_ref, hx_hbm, h_in, w1_ref, w3_ref, w2_ref, h_out,
                xbuf, acc, sem_x, sem_h, sem_s, *, S, D, cap, nj):
    del h_in
    e = pl.program_id(0)
    j = pl.program_id(1)
    first = (e == 0) & (j == 0)
    last = (e == pl.num_programs(0) - 1) & (j == nj - 1)
    base0 = (2 * j) * S
    base1 = base0 + S
    jn = jnp.where(last, j, (j + 1) % nj)
    basen = (2 * jn) * S

    def gather_x(idx_ref, base, slot, r):
        return pltpu.make_async_copy(hx_hbm.at[pl.ds(base + idx_ref[0, 0, r], 1), :],
                                     xbuf.at[slot, pl.ds(r, 1), :], sem_x.at[slot])

    def gather_h(idx_ref, base, slot, r):
        return pltpu.make_async_copy(h_out.at[pl.ds(base + idx_ref[0, 0, r], 1), :],
                                     acc.at[slot, pl.ds(r, 1), :], sem_h.at[slot])

    def scatter_h(idx_ref, base, slot, r):
        return pltpu.make_async_copy(acc.at[slot, pl.ds(r, 1), :],
                                     h_out.at[pl.ds(base + idx_ref[0, 0, r], 1), :], sem_s.at[slot])

    def wait_x(slot):
        pltpu.make_async_copy(hx_hbm.at[pl.ds(0, cap), :], xbuf.at[slot], sem_x.at[slot]).wait()

    def wait_h(slot):
        pltpu.make_async_copy(h_out.at[pl.ds(0, cap), :], acc.at[slot], sem_h.at[slot]).wait()

    def wait_s(slot):
        pltpu.make_async_copy(acc.at[slot], h_out.at[pl.ds(0, cap), :], sem_s.at[slot]).wait()

    def swiglu(slot, starts):
        F = w1_ref.shape[-1]
        ns = FFN_SLABS
        per = -(-len(starts) // (3 * ns))
        pending = list(starts)

        def issue_some():
            for thunk in pending[:per]:
                thunk()
            del pending[:per]

        x = xbuf[slot, :, :D].astype(BF16)
        fs = F // ns
        a, g = [], []
        for s in range(ns):
            issue_some()
            a.append(_dot(x, w1_ref[0, :, s * fs:(s + 1) * fs]))
        for s in range(ns):
            issue_some()
            g.append(_dot(x, w3_ref[0, :, s * fs:(s + 1) * fs]))
        hm = jnp.concatenate([(a[s] * _sigmoid(a[s]) * g[s]).astype(BF16) for s in range(ns)], axis=1)
        ds_ = D // ns
        y = []
        for s in range(ns):
            issue_some()
            y.append(_dot(hm, w2_ref[0, :, s * ds_:(s + 1) * ds_]))
        for thunk in pending:
            thunk()
        return jnp.concatenate(y, axis=1)

    def accumulate(slot, y):
        aff = xbuf[slot, :, D:]
        lane = lax.broadcasted_iota(I32, aff.shape, 1)
        gate = jnp.sum(jnp.where(lane == e, aff, 0.0), axis=1, keepdims=True)
        acc[slot] = acc[slot] + y * gate

    @pl.when(first)
    def _():
        def body(r, carry):
            gather_x(idx0_ref, base0, 0, r).start()
            return carry
        lax.fori_loop(0, cap, body, 0, unroll=8)

    def start(copy_fn, *args):
        return lambda: copy_fn(*args).start()

    wait_x(0)
    starts = []
    for r in range(cap):
        starts.append(start(gather_h, idx0_ref, base0, 0, r))
        starts.append(start(gather_x, idx1_ref, base1, 1, r))
    y = swiglu(0, starts)
    wait_h(0)
    accumulate(0, y)
    wait_x(1)
    starts = []
    for r in range(cap):
        starts.append(start(gather_h, idx1_ref, base1, 1, r))
        starts.append(start(scatter_h, idx0_ref, base0, 0, r))
        starts.append(start(gather_x, idxn_ref, basen, 0, r))
    y = swiglu(1, starts)
    wait_h(1)
    accumulate(1, y)

    def body_out(r, carry):
        scatter_h(idx1_ref, base1, 1, r).start()
        return carry

    lax.fori_loop(0, cap, body_out, 0, unroll=8)
    wait_s(0)
    wait_s(1)

    @pl.when(last)
    def _():
        wait_x(0)


def moe_ffn(idx, hx, h, w1, w3, w2, *, B, S):
    T, D = h.shape
    E = N_EXPERTS
    cap = idx.shape[-1]
    F = w1.shape[-1]
    assert B % 2 == 0
    nj = B // 2
    idx3 = idx.reshape(B * E, 1, cap)

    def idx_spec(fn):
        return pl.BlockSpec((1, 1, cap), fn, memory_space=pltpu.SMEM)

    def nxt(e, j):
        is_last = (e == E - 1) & (j == nj - 1)
        en = jnp.where(is_last, e, e + (j + 1) // nj)
        jn = jnp.where(is_last, j, (j + 1) % nj)
        return (2 * jn * E + en, 0, 0)

    return pl.pallas_call(
        functools.partial(_ffn_kernel, S=S, D=D, cap=cap, nj=nj),
        grid=(E, nj),
        in_specs=[idx_spec(lambda e, j: (2 * j * E + e, 0, 0)),
                  idx_spec(lambda e, j: ((2 * j + 1) * E + e, 0, 0)),
                  idx_spec(nxt),
                  pl.BlockSpec(memory_space=pl.ANY),
                  pl.BlockSpec(memory_space=pl.ANY),
                  pl.BlockSpec((1, D, F), lambda e, j: (e, 0, 0)),
                  pl.BlockSpec((1, D, F), lambda e, j: (e, 0, 0)),
                  pl.BlockSpec((1, F, D), lambda e, j: (e, 0, 0))],
        out_specs=pl.BlockSpec(memory_space=pl.ANY),
        out_shape=jax.ShapeDtypeStruct((T, D), F32),
        scratch_shapes=[pltpu.VMEM((2, cap, D + LANES), F32),
                        pltpu.VMEM((2, cap, D), F32),
                        pltpu.SemaphoreType.DMA((2,)),
                        pltpu.SemaphoreType.DMA((2,)),
                        pltpu.SemaphoreType.DMA((2,))],
        input_output_aliases={4: 0},
        compiler_params=_params(("arbitrary", "arbitrary")),
        name="moe_ffn",
    )(idx3, idx3, idx3, hx, h, w1, w3, w2)


def ec_moe(h, gain, router, w1, w3, w2, *, B, S):
    hx, aff_t = moe_router(h, gain, router, B=B, S=S)
    idx = moe_topk(aff_t, B=B, S=S)
    return moe_ffn(idx, hx, h, w1.astype(BF16), w3.astype(BF16), w2.astype(BF16), B=B, S=S)


def _pad_cols(w, n):
    return jnp.pad(w, ((0, 0), (0, n - w.shape[1])))


def gla_layer(h, norm_gain, w_in, w_gate_up, b_gate, head_norm, w_out, *, B, S):
    nmain = 2 * GLA_HEADS * GLA_DK + 2 * GLA_HEADS * GLA_DV
    proj = norm_matmul(h, norm_gain, w_in[:, :nmain].astype(BF16), name="gla_in")
    glo = norm_matmul(h, norm_gain, _pad_cols(w_in[:, nmain:], LANES).astype(BF16), tn=LANES, name="gla_in_gate")
    wg = jnp.pad(w_gate_up.reshape(2 * GLA_RANK, -1), ((0, LANES - 2 * GLA_RANK), (0, 0))).astype(BF16)
    y = gla_scan(proj, glo, wg, b_gate.astype(F32), head_norm, B=B, S=S)
    return matmul_residual(y, w_out.astype(BF16), h, name="gla_out")


def gdn_layer(h, norm_gain, w_in, conv_w, a_log, dt_bias, head_norm, w_out, *, B, S):
    nconv = conv_w.shape[1]
    nmain = nconv + GDN_V_HEADS * GDN_HD
    proj = norm_matmul(h, norm_gain, w_in[:, :nmain].astype(BF16), name="gdn_in")
    ab = norm_matmul(h, norm_gain, w_in[:, nmain:].astype(BF16), tn=LANES, name="gdn_in_gate")
    qkv = gdn_conv(proj, conv_w, B=B, S=S)
    zeros = jnp.zeros((2 * GDN_V_HEADS,), F32)
    alog_row = jnp.concatenate([a_log.astype(F32).reshape(-1), zeros]).reshape(1, LANES)
    dtb_row = jnp.concatenate([dt_bias.astype(F32).reshape(-1), zeros]).reshape(1, LANES)
    y = gdn_scan(qkv, proj, ab, alog_row, dtb_row, head_norm, B=B, S=S)
    return matmul_residual(y, w_out.astype(BF16), h, name="gdn_out")


def diff_layer(h, norm_gain, w_in, q_norm, k_norm, lam, subln, w_out, bias_tiles, layer_idx, *, B, S):
    proj = norm_matmul(h, norm_gain, w_in.astype(BF16), name="diff_in")
    y = diff_attention(proj, q_norm, k_norm, lam, subln, bias_tiles, B=B, S=S, layer_idx=layer_idx)
    return matmul_residual(y, w_out.astype(BF16), h, name="diff_out")


def swa_layer(h, norm_gain, w_in, q_norm, k_norm, sink, w_out, bias, *, B, S):
    proj = norm_matmul(h, norm_gain, w_in.astype(BF16), name="swa_in")
    y = swa_attention(proj, q_norm, k_norm, sink, bias, B=B, S=S)
    return matmul_residual(y, w_out.astype(BF16), h, name="swa_out")


def kernel(x, rel_bias, norm_mix, norm_ffn, gla_w_in, gla_w_gate_up, gla_b_gate, gla_head_norm, gla_w_out, gdn_w_in, gdn_conv, gdn_a_log, gdn_dt_bias, gdn_head_norm, gdn_w_out, diff_w_in, diff_q_norm, diff_k_norm, diff_lambda, diff_subln, diff_w_out, swa_w_in, swa_q_norm, swa_k_norm, swa_sink, swa_w_out, moe_router, moe_w1, moe_w3, moe_w2):
    B, S, D = x.shape
    depth = norm_mix.shape[0]
    h = x.reshape(B * S, D)
    for i in range(depth):
        m, j = i % 4, i // 4
        if m == 0:
            h = gla_layer(h, norm_mix[i], gla_w_in[j], gla_w_gate_up[j], gla_b_gate[j], gla_head_norm[j],
                          gla_w_out[j], B=B, S=S)
        elif m == 1:
            h = gdn_layer(h, norm_mix[i], gdn_w_in[j], gdn_conv[j], gdn_a_log[j], gdn_dt_bias[j],
                          gdn_head_norm[j], gdn_w_out[j], B=B, S=S)
        elif m == 2:
            h = diff_layer(h, norm_mix[i], diff_w_in[j], diff_q_norm[j], diff_k_norm[j], diff_lambda[j],
                           diff_subln[j], diff_w_out[j], diff_bias_tiles(rel_bias), i, B=B, S=S)
        else:
            h = swa_layer(h, norm_mix[i], swa_w_in[j], swa_q_norm[j], swa_k_norm[j], swa_sink[j],
                          swa_w_out[j], swa_bias(rel_bias), B=B, S=S)
        h = ec_moe(h, norm_ffn[i], moe_router[i], moe_w1[i], moe_w3[i], moe_w2[i], B=B, S=S)
    return h.reshape(B, S, D)
```

```python
import functools
import math

import jax
import jax.numpy as jnp
from jax import lax
from jax.experimental import pallas as pl
from jax.experimental.pallas import tpu as pltpu

F32 = jnp.float32
BF16 = jnp.bfloat16
I32 = jnp.int32
HI = lax.Precision.HIGHEST

RMS_EPS = 1e-6
VMEM_LIMIT_BYTES = 56 * 1024 * 1024
LANES = 128

REL_BUCKETS = 32
REL_MAX_DIST = 128
CHUNK = 64
GLA_HEADS, GLA_DK, GLA_DV, GLA_RANK, GLA_TAU = 4, 256, 512, 16, 16.0
GDN_QK_HEADS, GDN_V_HEADS, GDN_HD, GDN_CONV = 16, 32, 128, 5
GDN_REP = GDN_V_HEADS // GDN_QK_HEADS
TRI_ROWS = 256
DIFF_HEADS, DIFF_DQK, DIFF_DV, DIFF_QB = 16, 64, 128, 256
DIFF_KB = 512
SWA_HEADS, SWA_KV, SWA_GROUP, SWA_HD, SWA_W, SWA_QB = 16, 4, 4, 128, 128, 128
SWA_QPS = 4
N_EXPERTS = 16
EC_CAPACITY_FACTOR = 2
FFN_SLABS = 4


def _params(sem):
    return pltpu.CompilerParams(dimension_semantics=sem, vmem_limit_bytes=VMEM_LIMIT_BYTES)


def _dot(a, b, **kw):
    return jnp.dot(a, b, preferred_element_type=F32, **kw)


def _dot_nt(a, b, **kw):
    return lax.dot_general(a, b, (((1,), (1,)), ((), ())), preferred_element_type=F32, **kw)


def _dot_tn(a, b, **kw):
    return lax.dot_general(a, b, (((0,), (0,)), ((), ())), preferred_element_type=F32, **kw)


def _rms(x):
    return x * lax.rsqrt(jnp.mean(x * x, axis=-1, keepdims=True) + RMS_EPS)


def _sigmoid(x):
    return 1.0 / (1.0 + jnp.exp(-x))


def _softplus(x):
    return jnp.maximum(x, 0.0) + jnp.log(1.0 + jnp.exp(-jnp.abs(x)))


def _norm_matmul_kernel(x_ref, g_ref, w_ref, o_ref, xn_ref):
    @pl.when(pl.program_id(1) == 0)
    def _():
        xn_ref[...] = (_rms(x_ref[...]) * g_ref[...]).astype(BF16)

    o_ref[...] = _dot(xn_ref[...], w_ref[...])


def norm_matmul(x, gain, w, *, tm=1024, tn=512, name):
    T, D = x.shape
    N = w.shape[1]
    tm, tn = min(tm, T), min(tn, N)
    assert T % tm == 0 and N % tn == 0
    return pl.pallas_call(
        _norm_matmul_kernel,
        grid=(T // tm, N // tn),
        in_specs=[pl.BlockSpec((tm, D), lambda i, j: (i, 0)),
                  pl.BlockSpec((1, D), lambda i, j: (0, 0)),
                  pl.BlockSpec((D, tn), lambda i, j: (0, j))],
        out_specs=pl.BlockSpec((tm, tn), lambda i, j: (i, j)),
        out_shape=jax.ShapeDtypeStruct((T, N), F32),
        scratch_shapes=[pltpu.VMEM((tm, D), BF16)],
        compiler_params=_params(("parallel", "arbitrary")),
        name=name,
    )(x, gain.reshape(1, D), w)


def _matmul_res_kernel(y_ref, w_ref, h_ref, o_ref):
    o_ref[...] = h_ref[...] + _dot(y_ref[...].astype(BF16), w_ref[...])


def matmul_residual(y, w, h, *, tm=256, tn=2048, name):
    T, K = y.shape
    N = w.shape[1]
    tm, tn = min(tm, T), min(tn, N)
    assert T % tm == 0 and N % tn == 0
    return pl.pallas_call(
        _matmul_res_kernel,
        grid=(T // tm, N // tn),
        in_specs=[pl.BlockSpec((tm, K), lambda i, j: (i, 0)),
                  pl.BlockSpec((K, tn), lambda i, j: (0, j)),
                  pl.BlockSpec((tm, tn), lambda i, j: (i, j))],
        out_specs=pl.BlockSpec((tm, tn), lambda i, j: (i, j)),
        out_shape=jax.ShapeDtypeStruct((T, N), F32),
        compiler_params=_params(("parallel", "parallel")),
        name=name,
    )(y, w, h)


def _tri_masks(bwd):
    row = lax.broadcasted_iota(I32, (CHUNK, CHUNK), 0)
    col = lax.broadcasted_iota(I32, (CHUNK, CHUNK), 1)
    incl = jnp.where(bwd, (col >= row).astype(F32), (col <= row).astype(F32))
    strict = jnp.where(bwd, (col > row).astype(F32), (col < row).astype(F32))
    return incl, strict


def _gla_kernel(q_ref, k_ref, v_ref, r_ref, glo_ref, wg_ref, bg_ref, hn_ref, trif_ref, trib_ref, o_ref,
                of_ref, st_ref, *, nb, blk):
    i = pl.program_id(2)
    nc = blk // CHUNK
    cs = range(nc)

    @pl.when((i == 0) | (i == nb))
    def _():
        st_ref[...] = jnp.zeros_like(st_ref)

    def scan_block(bwd):
        sb = (2 * nb - 1 - i) if bwd else i
        lane = lax.broadcasted_iota(I32, (blk, LANES), 1)
        lo = GLA_RANK if bwd else 0
        gsel = jnp.where((lane >= lo) & (lane < lo + GLA_RANK), glo_ref[...], 0.0)
        bg = bg_ref[...]
        gate = _dot(gsel.astype(BF16), wg_ref[...]) + (bg[1:2] if bwd else bg[0:1])
        la = (jnp.minimum(gate, 0.0) - jnp.log(1.0 + jnp.exp(-jnp.abs(gate)))) * (1.0 / GLA_TAU)
        tri = trib_ref[...] if bwd else trif_ref[...]
        grp = tri.shape[0]
        pieces = _split3_bf16(la)
        cum_blk = jnp.concatenate([sum(_dot(tri, pc[r:r + grp]) for pc in pieces) for r in range(0, blk, grp)],
                                  axis=0)
        row = lax.broadcasted_iota(I32, (CHUNK, CHUNK), 0)
        col = lax.broadcasted_iota(I32, (CHUNK, CHUNK), 1)
        incl = ((col >= row) if bwd else (col <= row)).astype(F32)
        r0 = [(nc - 1 - c if bwd else c) * CHUNK for c in cs]
        rows = [slice(r, r + CHUNK) for r in r0]
        cum = [cum_blk[rw] for rw in rows]
        tot = [cum_blk[(r if bwd else r + CHUNK - 1):(r + 1 if bwd else r + CHUNK)] for r in r0]
        q = [q_ref[rw, :] * (GLA_DK ** -0.5) for rw in rows]
        k = [k_ref[rw, :] for rw in rows]
        v = [v_ref[rw, :].astype(BF16) for rw in rows]
        qd = [(q[c] * jnp.exp(cum[c])).astype(BF16) for c in cs]
        kin = [(k[c] * jnp.exp(-cum[c])).astype(BF16) for c in cs]
        kst = [(k[c] * jnp.exp(tot[c] - cum[c])).astype(BF16) for c in cs]
        s = [(_dot_nt(qd[c], kin[c]) * incl).astype(BF16) for c in cs]
        o = [_dot(s[c], v[c]) for c in cs]
        st = st_ref[...]
        upd = _dot_tn(v[0], kst[0])
        for c in cs:
            nxt = _dot_tn(v[c + 1], kst[c + 1]) if c + 1 < nc else None
            o[c] = o[c] + _dot_nt(qd[c], st.astype(BF16))
            st = st * jnp.exp(tot[c]) + upd
            upd = nxt
        st_ref[...] = st
        gain = hn_ref[...]
        for c in cs:
            grow = pl.ds(pl.multiple_of(sb * blk + r0[c], CHUNK), CHUNK)
            if bwd:
                ot = of_ref[grow, :] + o[c]
                r = r_ref[rows[c], :]
                o_ref[rows[c], :] = _rms(ot) * gain * (r * _sigmoid(r))
            else:
                of_ref[grow, :] = o[c]

    @pl.when(i < nb)
    def _():
        scan_block(False)

    @pl.when(i >= nb)
    def _():
        scan_block(True)


def gla_scan(proj, glo, wg, bg, head_norm, *, B, S, blk=512):
    T = B * S
    blk = min(blk, S)
    nb = S // blk
    H = GLA_HEADS
    grp = min(TRI_ROWS, blk)

    def rowblk(b, i):
        return b * nb + jnp.where(i >= nb, 2 * nb - 1 - i, i)

    def outblk(b, i):
        return b * nb + jnp.where(i >= nb, 2 * nb - 1 - i, nb - 1)

    kq = GLA_HEADS * GLA_DK // GLA_DK
    kv = 2 * GLA_HEADS * GLA_DK // GLA_DV
    kr = kv + GLA_HEADS
    return pl.pallas_call(
        functools.partial(_gla_kernel, nb=nb, blk=blk),
        grid=(B, H, 2 * nb),
        in_specs=[pl.BlockSpec((blk, GLA_DK), lambda b, h, i: (rowblk(b, i), h)),
                  pl.BlockSpec((blk, GLA_DK), lambda b, h, i: (rowblk(b, i), kq + h)),
                  pl.BlockSpec((blk, GLA_DV), lambda b, h, i: (rowblk(b, i), kv + h)),
                  pl.BlockSpec((blk, GLA_DV), lambda b, h, i: (outblk(b, i), kr + h)),
                  pl.BlockSpec((blk, LANES), lambda b, h, i: (rowblk(b, i), 0)),
                  pl.BlockSpec((LANES, GLA_DK), lambda b, h, i: (0, h)),
                  pl.BlockSpec((2, GLA_DK), lambda b, h, i: (0, h)),
                  pl.BlockSpec((1, GLA_DV), lambda b, h, i: (0, 0)),
                  pl.BlockSpec((grp, grp), lambda b, h, i: (0, 0)),
                  pl.BlockSpec((grp, grp), lambda b, h, i: (0, 0))],
        out_specs=pl.BlockSpec((blk, GLA_DV), lambda b, h, i: (outblk(b, i), h)),
        out_shape=jax.ShapeDtypeStruct((T, H * GLA_DV), F32),
        scratch_shapes=[pltpu.VMEM((S, GLA_DV), F32),
                        pltpu.VMEM((GLA_DV, GLA_DK), F32)],
        compiler_params=_params(("parallel", "parallel", "arbitrary")),
        name="gla_scan",
    )(proj, proj, proj, proj, glo, wg, bg, head_norm.reshape(1, GLA_DV),
      _block_tri(grp, False), _block_tri(grp, True))


def _gdn_conv_kernel(x_ref, w_ref, o_ref, xp_ref, *, S, rows):
    c = pl.program_id(1)
    pad = 8
    xp_ref[0:pad, :] = jnp.zeros((pad, LANES), F32)
    xp_ref[pad + S:2 * pad + S, :] = jnp.zeros((pad, LANES), F32)
    xp_ref[pad:pad + S, :] = x_ref[...]
    w = w_ref[...]
    win = rows + 2 * pad
    is_qk = c < 2 * GDN_QK_HEADS
    scale = jnp.where(c < GDN_QK_HEADS, GDN_HD ** -0.5, 1.0)

    def conv_silu(t):
        r0 = pl.multiple_of(t * rows, rows)
        xw = xp_ref[pl.ds(r0, win), :]
        acc = jnp.zeros((rows, LANES), F32)
        for j in range(GDN_CONV):
            sh = (GDN_CONV // 2 - j) % win
            xs = xw if sh == 0 else pltpu.roll(xw, sh, 0)
            acc = acc + xs[pad:pad + rows, :] * w[j:j + 1, :]
        return r0, acc * _sigmoid(acc)

    @pl.when(is_qk)
    def _():
        def body(t, carry):
            r0, y = conv_silu(t)
            o_ref[pl.ds(r0, rows), :] = y * (lax.rsqrt(jnp.sum(y * y, axis=-1, keepdims=True) + RMS_EPS) * scale)
            return carry
        lax.fori_loop(0, S // rows, body, 0, unroll=2)

    @pl.when(jnp.logical_not(is_qk))
    def _():
        def body(t, carry):
            r0, y = conv_silu(t)
            o_ref[pl.ds(r0, rows), :] = y
            return carry
        lax.fori_loop(0, S // rows, body, 0, unroll=2)


def gdn_conv(proj, conv_w, *, B, S):
    T = B * S
    nch = conv_w.shape[1] // LANES
    rows = min(256, S)
    return pl.pallas_call(
        functools.partial(_gdn_conv_kernel, S=S, rows=rows),
        grid=(B, nch),
        in_specs=[pl.BlockSpec((S, LANES), lambda b, c: (b, c)),
                  pl.BlockSpec((GDN_CONV, LANES), lambda b, c: (0, c))],
        out_specs=pl.BlockSpec((S, LANES), lambda b, c: (b, c)),
        out_shape=jax.ShapeDtypeStruct((T, nch * LANES), F32),
        scratch_shapes=[pltpu.VMEM((S + 16, LANES), F32)],
        compiler_params=_params(("parallel", "parallel")),
        name="gdn_conv",
    )(proj, conv_w)


def _mm_bf16(a, b):
    return _dot(a.astype(BF16), b.astype(BF16))


def _unit_tri_inverses(Ls):
    row = lax.broadcasted_iota(I32, (CHUNK, CHUNK), 0)
    col = lax.broadcasted_iota(I32, (CHUNK, CHUNK), 1)
    eye = (row == col).astype(F32)
    ps = [eye - L for L in Ls]
    pws = [_mm_bf16(L, L) for L in Ls]
    n = 2
    while True:
        ps = [p + _mm_bf16(p, pw) for p, pw in zip(ps, pws)]
        n *= 2
        if n >= CHUNK:
            return ps
        pws = [_mm_bf16(pw, pw) for pw in pws]


def _split3_bf16(x):
    hi = x.astype(BF16)
    r1 = x - hi.astype(F32)
    mid = r1.astype(BF16)
    lo = (r1 - mid.astype(F32)).astype(BF16)
    return hi, mid, lo


def _gdn_prepare(q_ref, k_ref, v_ref, ab_ref, tri_ref, alog, dtb, *, d, qh, blk):
    bwd = d == 1
    nc = blk // CHUNK
    cs = range(nc)
    hs = range(GDN_REP)
    hc = [(h, c) for h in hs for c in cs]
    row = lax.broadcasted_iota(I32, (CHUNK, CHUNK), 0)
    col = lax.broadcasted_iota(I32, (CHUNK, CHUNK), 1)
    incl = (col >= row) if bwd else (col <= row)
    strict = (col > row) if bwd else (col < row)

    sl = [slice(c * CHUNK, (c + 1) * CHUNK) for c in cs]
    q = [q_ref[s, :] for s in sl]
    k = [k_ref[s, :] for s in sl]
    qbf = [t.astype(BF16) for t in q]
    kbf = [t.astype(BF16) for t in k]
    kk = [_dot_nt(kbf[c], kbf[c]) for c in cs]
    qkr = [_dot_nt(qbf[c], kbf[c]) for c in cs]
    x = ab_ref[...]
    gfull = -jnp.exp(alog) * _softplus(x + dtb)
    bfull = _sigmoid(x)
    lane = lax.broadcasted_iota(I32, (blk, LANES), 1)

    def pick(full, ln):
        return jnp.broadcast_to(jnp.sum(jnp.where(lane == ln, full, 0.0), axis=1, keepdims=True), (blk, LANES))

    lane_g = [d * GDN_V_HEADS + GDN_REP * qh + h for h in hs]
    gb = [pick(gfull, ln) for ln in lane_g]
    bb = [pick(bfull, 2 * GDN_V_HEADS + ln) for ln in lane_g]
    tri = tri_ref[...]
    grp = tri.shape[0]
    pieces = [_split3_bf16(g) for g in gb]
    gc_blk = [jnp.concatenate([sum(_dot(tri, pc[r:r + grp]) for pc in pieces[h]) for r in range(0, blk, grp)], axis=0)
              for h in hs]

    last = [c * CHUNK if bwd else (c + 1) * CHUNK - 1 for c in cs]
    gc = [gc_blk[h][sl[c]] for h, c in hc]
    tot = [gc_blk[h][last[c]:last[c] + 1] for h, c in hc]
    beta = [bb[h][sl[c]] for h, c in hc]
    gamma = [jnp.where(incl, jnp.exp(g[:, :CHUNK] - jnp.transpose(g)[:CHUNK, :]), 0.0) for g in gc]
    tinv =_unit_tri_inverses([jnp.where(strict, kk[c] * beta[j][:, :CHUNK] * gamma[j], 0.0)
                               for j, (h, c) in enumerate(hc)])
    egc = [jnp.exp(g) for g in gc]
    rhs = [jnp.concatenate([v_ref[sl[c], h * GDN_HD:(h + 1) * GDN_HD] * beta[j], k[c] * beta[j] * egc[j]], axis=1)
           for j, (h, c) in enumerate(hc)]
    uw = [_mm_bf16(tinv[j], rhs[j]).astype(BF16) for j in range(len(hc))]
    qk = [(qkr[c] * gamma[j]).astype(BF16) for j, (h, c) in enumerate(hc)]
    kst = [(k[c] * jnp.exp(tot[j] - gc[j])).astype(BF16) for j, (h, c) in enumerate(hc)]
    qd = [q[c] * egc[j] for j, (h, c) in enumerate(hc)]
    return dict(uw=uw, qk=qk, kst=kst, qd=qd, dec=[jnp.exp(t) for t in tot])


def _gdn_chunk_operands(p, j):
    kuw = _dot_tn(p["kst"][j], p["uw"][j])
    quw = _dot(p["qk"][j], p["uw"][j])
    qt = (p["qd"][j] - quw[:, GDN_HD:]).astype(BF16)
    return qt, quw[:, :GDN_HD], kuw[:, GDN_HD:].astype(BF16), kuw[:, :GDN_HD], p["dec"][j]


def _gdn_chunk_step(ops, S):
    qt, qu, kw, ku, dec = ops
    sb16 = S.astype(BF16)
    return _dot(qt, sb16) + qu, S * dec - _dot(kw, sb16) + ku


def _gdn_kernel(qf_ref, kf_ref, vf_ref, abf_ref, qb_ref, kb_ref, vb_ref, abb_ref, trif_ref, trib_ref,
                z_ref, alog_ref, dtb_ref, hn_ref, o_ref, of_ref, ob_ref, st_ref, *, nb, blk, S):
    qh = pl.program_id(1)
    i = pl.program_id(2)
    nc = blk // CHUNK

    @pl.when(i == 0)
    def _():
        st_ref[...] = jnp.zeros_like(st_ref)

    alog = alog_ref[...]
    dtb = dtb_ref[...]
    pf = _gdn_prepare(qf_ref, kf_ref, vf_ref, abf_ref, trif_ref, alog, dtb, d=0, qh=qh, blk=blk)
    pb = _gdn_prepare(qb_ref, kb_ref, vb_ref, abb_ref, trib_ref, alog, dtb, d=1, qh=qh, blk=blk)
    sf = [st_ref[0, h] for h in range(GDN_REP)]
    sb = [st_ref[1, h] for h in range(GDN_REP)]
    rowf = i * blk
    rowb = (nb - 1 - i) * blk
    def operands(c):
        return [(_gdn_chunk_operands(pf, h * nc + c), _gdn_chunk_operands(pb, h * nc + nc - 1 - c))
                for h in range(GDN_REP)]

    nxt = operands(0)
    for c in range(nc):
        cb = nc - 1 - c
        cur = nxt
        if c + 1 < nc:
            nxt = operands(c + 1)
        outs = []
        for h in range(GDN_REP):
            of, sf[h] = _gdn_chunk_step(cur[h][0], sf[h])
            ob, sb[h] = _gdn_chunk_step(cur[h][1], sb[h])
            outs.append((of, ob))
        for h, (of, ob) in enumerate(outs):
            cols = slice(h * GDN_HD, (h + 1) * GDN_HD)
            of_ref[pl.ds(pl.multiple_of(rowf + c * CHUNK, CHUNK), CHUNK), cols] = of
            ob_ref[pl.ds(pl.multiple_of(rowb + cb * CHUNK, CHUNK), CHUNK), cols] = ob
    for h in range(GDN_REP):
        st_ref[0, h] = sf[h]
        st_ref[1, h] = sb[h]

    @pl.when(i == nb - 1)
    def _():
        gain = hn_ref[...]
        rows_e = min(256, S)

        def ebody(t, carry):
            rows = pl.ds(pl.multiple_of(t * rows_e, rows_e), rows_e)
            for h in range(GDN_REP):
                cols = slice(h * GDN_HD, (h + 1) * GDN_HD)
                ot = of_ref[rows, cols] + ob_ref[rows, cols]
                z = z_ref[rows, cols]
                o_ref[rows, cols] = _rms(ot) * gain * (z * _sigmoid(z))
            return carry

        lax.fori_loop(0, S // rows_e, ebody, 0)


def _block_tri(n, bwd):
    r = jnp.arange(n, dtype=I32)[:, None]
    c = jnp.arange(n, dtype=I32)[None, :]
    same = (r // CHUNK) == (c // CHUNK)
    return (same & ((c >= r) if bwd else (c <= r))).astype(BF16)


def gdn_scan(qkv, proj, ab, alog_row, dtb_row, head_norm, *, B, S, blk=512):
    T = B * S
    blk = min(blk, S)
    nb = S // blk
    grp = min(TRI_ROWS, blk)
    vw = GDN_REP * GDN_HD
    voff = 2 * GDN_QK_HEADS * GDN_HD // vw
    zoff = (2 * GDN_QK_HEADS + GDN_V_HEADS) * GDN_HD // vw

    def fwd(b, i):
        return b * nb + i

    def bwd(b, i):
        return b * nb + nb - 1 - i

    def dir_specs(rb):
        return [pl.BlockSpec((blk, GDN_HD), lambda b, h, i: (rb(b, i), h)),
                pl.BlockSpec((blk, GDN_HD), lambda b, h, i: (rb(b, i), GDN_QK_HEADS + h)),
                pl.BlockSpec((blk, vw), lambda b, h, i: (rb(b, i), voff + h)),
                pl.BlockSpec((blk, LANES), lambda b, h, i: (rb(b, i), 0))]

    const2 = lambda b, h, i: (0, 0)
    return pl.pallas_call(
        functools.partial(_gdn_kernel, nb=nb, blk=blk, S=S),
        grid=(B, GDN_QK_HEADS, nb),
        in_specs=dir_specs(fwd) + dir_specs(bwd) + [
            pl.BlockSpec((grp, grp), const2),
            pl.BlockSpec((grp, grp), const2),
            pl.BlockSpec((S, vw), lambda b, h, i: (b, zoff + h)),
            pl.BlockSpec((1, LANES), const2),
            pl.BlockSpec((1, LANES), const2),
            pl.BlockSpec((1, GDN_HD), const2)],
        out_specs=pl.BlockSpec((S, vw), lambda b, h, i: (b, h)),
        out_shape=jax.ShapeDtypeStruct((T, GDN_V_HEADS * GDN_HD), F32),
        scratch_shapes=[pltpu.VMEM((S, vw), F32),
                        pltpu.VMEM((S, vw), F32),
                        pltpu.VMEM((2, GDN_REP, GDN_HD, GDN_HD), F32)],
        compiler_params=_params(("parallel", "parallel", "arbitrary")),
        name="gdn_scan",
    )(qkv, qkv, qkv, ab, qkv, qkv, qkv, ab, _block_tri(grp, False), _block_tri(grp, True),
      proj, alog_row, dtb_row, head_norm.reshape(1, GDN_HD))


def _t5_bucket(rel):
    half = REL_BUCKETS // 2
    max_exact = half // 2
    n = jnp.abs(rel)
    log_ratio = jnp.log(jnp.maximum(n, 1).astype(F32) / max_exact) / math.log(REL_MAX_DIST / max_exact)
    large = jnp.minimum(max_exact + (log_ratio * (half - max_exact)).astype(I32), half - 1)
    return jnp.where(rel > 0, half, 0) + jnp.where(n < max_exact, n, large)


def _rel_bias_heads(table, rel):
    return jnp.moveaxis(table[_t5_bucket(rel)].astype(F32), -1, 0)


def _toeplitz(w, n, m, off):
    lw = w.shape[-1]
    assert lw == n + m - 1 and m <= lw - 1
    w_rot = jnp.roll(w, -off, axis=-1)
    flat = jnp.tile(w_rot, (1,) * (w.ndim - 1) + (n,))[..., :n * (lw - 1)]
    return flat.reshape(w.shape[:-1] + (n, lw - 1))[..., :m]


def _half_rms(x, ones_bd):
    ms = _dot(x * x, ones_bd, precision=HI) * (1.0 / DIFF_DQK)
    return x * lax.rsqrt(ms + RMS_EPS)


LOG2E = math.log2(math.e)


def _diff_kernel(q_ref, k_ref, v_ref, qn_ref, kn_ref, lam_ref, sub_ref, bias_ref, o_ref,
                 kb_ref, vb_ref, m_ref, l_ref, acc_ref, s_ref, *, nk, lambda_init):
    i = pl.program_id(2)
    qb = DIFF_QB
    r = lax.broadcasted_iota(I32, (LANES, LANES), 0) // DIFF_DQK
    c = lax.broadcasted_iota(I32, (LANES, LANES), 1) // DIFF_DQK
    ones_bd = (r == c).astype(F32)

    @pl.when(i == 0)
    def _():
        def kbody(t, carry):
            rows = pl.ds(pl.multiple_of(t * qb, qb), qb)
            kb_ref[rows, :] = (_half_rms(k_ref[rows, :], ones_bd) * kn_ref[...]).astype(BF16)
            vb_ref[rows, :] = v_ref[rows, :].astype(BF16)
            return carry
        lax.fori_loop(0, nk, kbody, 0)

    q = _half_rms(q_ref[...], ones_bd) * qn_ref[...] * (DIFF_DQK ** -0.5 * LOG2E)
    lane = lax.broadcasted_iota(I32, (qb, LANES), 1)
    qs = (jnp.where(lane < DIFF_DQK, q, 0.0).astype(BF16), jnp.where(lane >= DIFF_DQK, q, 0.0).astype(BF16))

    m_ref[...] = jnp.full(m_ref.shape, -jnp.inf, F32)
    l_ref[...] = jnp.zeros(l_ref.shape, F32)
    acc_ref[...] = jnp.zeros(acc_ref.shape, F32)

    kw = DIFF_KB // qb
    maps = range(2)

    nsteps = nk * qb // DIFF_KB

    def key_rows(t):
        return pl.ds(t * DIFF_KB, DIFF_KB)

    def scores(t, slot):
        kc = kb_ref[key_rows(t), :]
        bias = jnp.concatenate([bias_ref[0, jnp.clip(kw * t + u - i, -2, 2) + 2] for u in range(kw)], axis=1)
        for mi in maps:
            s_ref[slot, mi] = _dot_nt(qs[mi], kc) + bias

    def chunk(t, prefetch):
        slot = t % 2
        s = [s_ref[slot, mi] for mi in maps]
        if prefetch:
            scores(t + 1, 1 - slot)
        vc = vb_ref[key_rows(t), :]
        m_cur = [jnp.max(s[mi], axis=1, keepdims=True) for mi in maps]
        m_prev = [m_ref[mi] for mi in maps]
        m_new = [jnp.maximum(m_prev[mi], m_cur[mi]) for mi in maps]
        alpha = [jnp.exp2(m_prev[mi] - m_new[mi]) for mi in maps]
        p = [jnp.exp2(s[mi] - jnp.concatenate([m_new[mi]] * (DIFF_KB // LANES), axis=1)) for mi in maps]
        psum = [jnp.sum(p[mi], axis=1, keepdims=True) for mi in maps]
        pv = [_dot(p[mi].astype(BF16), vc) for mi in maps]
        for mi in maps:
            l_ref[mi] = alpha[mi] * l_ref[mi] + psum[mi]
            acc_ref[mi] = alpha[mi] * acc_ref[mi] + pv[mi]
            m_ref[mi] = m_new[mi]

    scores(0, 0)
    for t in range(nsteps - 1):
        chunk(t, True)
    chunk(nsteps - 1, False)

    lam = lam_ref[...]
    lam_full = (jnp.exp(jnp.sum(lam[0:1] * lam[1:2], axis=-1, keepdims=True))
                - jnp.exp(jnp.sum(lam[2:3] * lam[3:4], axis=-1, keepdims=True)) + lambda_init)
    o = acc_ref[0] / l_ref[0] - lam_full * (acc_ref[1] / l_ref[1])
    o_ref[...] = _rms(o) * sub_ref[...] * (1.0 - lambda_init)


def diff_attention(proj, q_norm, k_norm, lam, subln, bias_tiles, *, B, S, layer_idx):
    T = B * S
    qb = DIFF_QB
    nq = S // qb
    H = DIFF_HEADS
    lambda_init = 0.8 - 0.6 * math.exp(-0.3 * layer_idx)
    qn2 = jnp.concatenate([q_norm, q_norm]).reshape(1, LANES)
    kn2 = jnp.concatenate([k_norm, k_norm]).reshape(1, LANES)
    return pl.pallas_call(
        functools.partial(_diff_kernel, nk=nq, lambda_init=lambda_init),
        grid=(B, H, nq),
        in_specs=[pl.BlockSpec((qb, LANES), lambda b, h, i: (b * nq + i, h)),
                  pl.BlockSpec((S, LANES), lambda b, h, i: (b, H + h)),
                  pl.BlockSpec((S, LANES), lambda b, h, i: (b, 2 * H + h)),
                  pl.BlockSpec((1, LANES), lambda b, h, i: (0, 0)),
                  pl.BlockSpec((1, LANES), lambda b, h, i: (0, 0)),
                  pl.BlockSpec((4, DIFF_DQK), lambda b, h, i: (0, 0)),
                  pl.BlockSpec((1, DIFF_DV), lambda b, h, i: (0, 0)),
                  pl.BlockSpec((1, 5, qb, qb), lambda b, h, i: (h, 0, 0, 0))],
        out_specs=pl.BlockSpec((qb, DIFF_DV), lambda b, h, i: (b * nq + i, h)),
        out_shape=jax.ShapeDtypeStruct((T, H * DIFF_DV), F32),
        scratch_shapes=[pltpu.VMEM((S, LANES), BF16),
                        pltpu.VMEM((S, DIFF_DV), BF16),
                        pltpu.VMEM((2, qb, LANES), F32),
                        pltpu.VMEM((2, qb, LANES), F32),
                        pltpu.VMEM((2, qb, DIFF_DV), F32),
                        pltpu.VMEM((2, 2, qb, DIFF_KB), F32)],
        compiler_params=_params(("parallel", "parallel", "arbitrary")),
        name="diff_attn",
    )(proj, proj, proj, qn2, kn2, lam, subln.reshape(1, DIFF_DV), bias_tiles)


def diff_bias_tiles(table):
    qb = DIFF_QB
    span = 3 * qb - 1
    vec = _rel_bias_heads(table, jnp.arange(-span, span + 1, dtype=I32)) * LOG2E
    w = jnp.stack([vec[:, (d + 2) * qb:(d + 2) * qb + 2 * qb - 1] for d in range(-2, 3)], axis=1)
    return _toeplitz(w, qb, qb, qb - 1)


def _swa_kernel(q_ref, *refs, nq, S):
    nkb = SWA_QPS + 2
    k_refs, v_refs = refs[:nkb], refs[nkb:2 * nkb]
    qn_ref, kn_ref, sink_ref, bias_ref, o_ref = refs[2 * nkb:]
    i = pl.program_id(2)
    qb = SWA_QB
    span = 3 * qb
    kcat = jnp.concatenate([r[...] for r in k_refs], axis=0)
    kcat = (_rms(kcat) * kn_ref[...]).astype(BF16)
    vcat = jnp.concatenate([r[...] for r in v_refs], axis=0).astype(BF16)
    row = lax.broadcasted_iota(I32, (qb, span), 0)
    col = lax.broadcasted_iota(I32, (qb, span), 1)
    rel = col - SWA_W - row
    in_window = jnp.abs(rel) <= SWA_W
    sink_all = sink_ref[0]
    pairs = [(a, g) for a in range(SWA_QPS) for g in range(SWA_GROUP)]
    valid = []
    for a in range(SWA_QPS):
        key_pos = (i * SWA_QPS + a) * qb - SWA_W + col
        valid.append(in_window & (key_pos >= 0) & (key_pos < S))
    q = [q_ref[a * qb:(a + 1) * qb, g * SWA_HD:(g + 1) * SWA_HD] for a, g in pairs]
    q = [(_rms(t) * qn_ref[...] * (SWA_HD ** -0.5)).astype(BF16) for t in q]
    s = [_dot_nt(q[j], kcat[a * qb:a * qb + span]) + bias_ref[g] for j, (a, g) in enumerate(pairs)]
    s = [jnp.where(valid[a], s[j], -jnp.inf) for j, (a, g) in enumerate(pairs)]
    sink = [sink_all[g:g + 1, 0:1] for a, g in pairs]
    js = range(len(pairs))
    m = [jnp.maximum(jnp.max(s[j], axis=-1, keepdims=True), sink[j]) for j in js]
    p = [jnp.exp(s[j] - m[j]) for j in js]
    den = [jnp.sum(p[j], axis=-1, keepdims=True) + jnp.exp(sink[j] - m[j]) for j in js]
    o = [_dot(p[j].astype(BF16), vcat[a * qb:a * qb + span]) / den[j] for j, (a, g) in enumerate(pairs)]
    for j, (a, g) in enumerate(pairs):
        o_ref[a * qb:(a + 1) * qb, g * SWA_HD:(g + 1) * SWA_HD] = o[j]


def swa_attention(proj, q_norm, k_norm, sink, bias, *, B, S):
    T = B * S
    qb = SWA_QB
    nq = S // qb
    koff = SWA_HEADS
    voff = SWA_HEADS + SWA_KV
    gw = SWA_GROUP * SWA_HD

    qps = SWA_QPS
    assert nq % qps == 0
    ns = nq // qps

    def kvspec(off, d):
        return pl.BlockSpec((qb, SWA_HD),
                            lambda b, kv, i: (b * nq + jnp.clip(i * qps + d, 0, nq - 1), off + kv))

    kv_specs = [kvspec(off, d) for off in (koff, voff) for d in range(-1, qps + 1)]
    sink_b = jnp.broadcast_to(sink.astype(F32).reshape(SWA_KV, SWA_GROUP, 1), (SWA_KV, SWA_GROUP, LANES))
    return pl.pallas_call(
        functools.partial(_swa_kernel, nq=nq, S=S),
        grid=(B, SWA_KV, ns),
        in_specs=[pl.BlockSpec((qps * qb, gw), lambda b, kv, i: (b * ns + i, kv))] + kv_specs + [
                  pl.BlockSpec((1, SWA_HD), lambda b, kv, i: (0, 0)),
                  pl.BlockSpec((1, SWA_HD), lambda b, kv, i: (0, 0)),
                  pl.BlockSpec((1, SWA_GROUP, LANES), lambda b, kv, i: (kv, 0, 0)),
                  pl.BlockSpec((SWA_GROUP, qb, 3 * qb), lambda b, kv, i: (kv, 0, 0))],
        out_specs=pl.BlockSpec((qps * qb, gw), lambda b, kv, i: (b * ns + i, kv)),
        out_shape=jax.ShapeDtypeStruct((T, SWA_HEADS * SWA_HD), F32),
        compiler_params=_params(("parallel", "parallel", "parallel")),
        name="swa_attn",
    )(*([proj] * (1 + 2 * (qps + 2))), q_norm.reshape(1, SWA_HD), k_norm.reshape(1, SWA_HD), sink_b, bias)


def swa_bias(table):
    qb = SWA_QB
    span = 3 * qb
    vec = _rel_bias_heads(table, jnp.arange(-(qb - 1) - SWA_W, span - SWA_W, dtype=I32))
    return _toeplitz(vec, qb, span, qb - 1)


def _router_kernel(h_ref, g_ref, wr_ref, hx_ref, at_ref, *, D):
    hn = _rms(h_ref[...]) * g_ref[...]
    logits = _dot(hn, wr_ref[...], precision=HI)
    lane = lax.broadcasted_iota(I32, logits.shape, 1)
    logits = jnp.where(lane < N_EXPERTS, logits, -jnp.inf)
    m = jnp.max(logits, axis=-1, keepdims=True)
    e = jnp.exp(logits - m)
    aff = e / jnp.sum(e, axis=-1, keepdims=True)
    hx_ref[:, :D] = hn
    hx_ref[:, D:] = aff
    at_ref[0] = jnp.transpose(aff)


def moe_router(h, gain, router, *, B, S, tm=512):
    T, D = h.shape
    tm = min(tm, S)
    ns = S // tm
    wr = jnp.pad(router.astype(F32), ((0, 0), (0, LANES - N_EXPERTS)))
    return pl.pallas_call(
        functools.partial(_router_kernel, D=D),
        grid=(B, ns),
        in_specs=[pl.BlockSpec((tm, D), lambda b, s: (b * ns + s, 0)),
                  pl.BlockSpec((1, D), lambda b, s: (0, 0)),
                  pl.BlockSpec((D, LANES), lambda b, s: (0, 0))],
        out_specs=[pl.BlockSpec((tm, D + LANES), lambda b, s: (b * ns + s, 0)),
                   pl.BlockSpec((1, LANES, tm), lambda b, s: (b, 0, s))],
        out_shape=[jax.ShapeDtypeStruct((T, D + LANES), F32),
                   jax.ShapeDtypeStruct((B, LANES, S), F32)],
        compiler_params=_params(("parallel", "parallel")),
        name="moe_router",
    )(h, gain.reshape(1, D), wr)


def _topk_kernel(aff_ref, idx_ref, pos_ref, *, S, cap):
    E = N_EXPERTS
    v = aff_ref[0]
    bits = pltpu.bitcast(v, I32)

    def search(_, carry):
        lo, hi = carry
        mid = lo + ((hi - lo) >> 1)
        cnt = jnp.sum((bits >= mid).astype(F32), axis=1, keepdims=True)
        ok = cnt >= cap
        return jnp.where(ok, mid, lo), jnp.where(ok, hi, mid)

    lo0 = jnp.zeros((E, 1), I32)
    hi0 = jnp.full((E, 1), 0x7F800001, I32)
    thr, _ = lax.fori_loop(0, 32, search, (lo0, hi0))
    gt = bits > thr
    eq = bits == thr
    need = cap - jnp.sum(gt.astype(F32), axis=1, keepdims=True)

    r = lax.broadcasted_iota(I32, (LANES, LANES), 0)
    c = lax.broadcasted_iota(I32, (LANES, LANES), 1)
    upper = (r < c).astype(BF16)
    run_e = jnp.zeros((E, 1), F32)
    run_s = jnp.zeros((E, 1), F32)
    sub = 512 // LANES
    for t in range(S // LANES):
        sl = slice(t * LANES, (t + 1) * LANES)
        eq_t = eq[:, sl].astype(F32)
        pe = _dot(eq_t.astype(BF16), upper) + run_e
        sel_t = jnp.where(gt[:, sl], 1.0, jnp.where(pe < need, eq_t, 0.0))
        ps = _dot(sel_t.astype(BF16), upper) + run_s
        pos_ref[t // sub, :, (t % sub) * LANES:(t % sub + 1) * LANES] = jnp.where(sel_t > 0, ps, -1.0)
        run_e = run_e + jnp.sum(eq_t, axis=1, keepdims=True)
        run_s = run_s + jnp.sum(sel_t, axis=1, keepdims=True)

    pi = lax.broadcasted_iota(I32, (cap, 512), 0).astype(F32)
    lane = lax.broadcasted_iota(I32, (8, 512), 1)
    rowv = lax.broadcasted_iota(I32, (8, 512), 0)

    def per_expert(e, carry):
        parts = []
        for t in range(S // 512):
            pos = pos_ref[t, pl.ds(e, 1), :]
            onehot = (pi == pos).astype(BF16)
            tok = t * 512 + lane
            vals = jnp.where(rowv == 0, tok >> 6, jnp.where(rowv == 1, tok & 63, 0)).astype(F32).astype(BF16)
            parts.append(_dot_nt(vals, onehot))
        acc = sum(parts)
        idx_ref[0, pl.ds(e, 1), :] = (acc[0:1] * 64.0 + acc[1:2]).astype(I32)
        return carry

    lax.fori_loop(0, E, per_expert, 0)


def moe_topk(aff_t, *, B, S):
    cap = EC_CAPACITY_FACTOR * S // N_EXPERTS
    return pl.pallas_call(
        functools.partial(_topk_kernel, S=S, cap=cap),
        grid=(B,),
        in_specs=[pl.BlockSpec((1, N_EXPERTS, S), lambda b: (b, 0, 0))],
        out_specs=pl.BlockSpec((1, N_EXPERTS, cap), lambda b: (b, 0, 0)),
        out_shape=jax.ShapeDtypeStruct((B, N_EXPERTS, cap), I32),
        scratch_shapes=[pltpu.VMEM((S // 512, N_EXPERTS, 512), F32)],
        compiler_params=_params(("parallel",)),
        name="moe_topk",
    )(aff_t)


def _ffn_kernel(idx0_ref, idx1_ref, idxn_ref, hx_hbm, h_in, w1_ref, w3_ref, w2_ref, h_out,
                xbuf, acc, sem_x, sem_h, sem_s, *, S, D, cap, nj):
    del h_in
    e = pl.program_id(0)
    j = pl.program_id(1)
    first = (e == 0) & (j == 0)
    last = (e == pl.num_programs(0) - 1) & (j == nj - 1)
    base0 = (2 * j) * S
    base1 = base0 + S
    jn = jnp.where(last, j, (j + 1) % nj)
    basen = (2 * jn) * S

    def gather_x(idx_ref, base, slot, r):
        return pltpu.make_async_copy(hx_hbm.at[pl.ds(base + idx_ref[0, 0, r], 1), :],
                                     xbuf.at[slot, pl.ds(r, 1), :], sem_x.at[slot])

    def gather_h(idx_ref, base, slot, r):
        return pltpu.make_async_copy(h_out.at[pl.ds(base + idx_ref[0, 0, r], 1), :],
                                     acc.at[slot, pl.ds(r, 1), :], sem_h.at[slot])

    def scatter_h(idx_ref, base, slot, r):
        return pltpu.make_async_copy(acc.at[slot, pl.ds(r, 1), :],
                                     h_out.at[pl.ds(base + idx_ref[0, 0, r], 1), :], sem_s.at[slot])

    def wait_x(slot):
        pltpu.make_async_copy(hx_hbm.at[pl.ds(0, cap), :], xbuf.at[slot], sem_x.at[slot]).wait()

    def wait_h(slot):
        pltpu.make_async_copy(h_out.at[pl.ds(0, cap), :], acc.at[slot], sem_h.at[slot]).wait()

    def wait_s(slot):
        pltpu.make_async_copy(acc.at[slot], h_out.at[pl.ds(0, cap), :], sem_s.at[slot]).wait()

    def swiglu(slot, starts):
        F = w1_ref.shape[-1]
        ns = FFN_SLABS
        per = -(-len(starts) // (3 * ns))
        pending = list(starts)

        def issue_some():
            for thunk in pending[:per]:
                thunk()
            del pending[:per]

        x = xbuf[slot, :, :D].astype(BF16)
        fs = F // ns
        a, g = [], []
        for s in range(ns):
            issue_some()
            a.append(_dot(x, w1_ref[0, :, s * fs:(s + 1) * fs]))
        for s in range(ns):
            issue_some()
            g.append(_dot(x, w3_ref[0, :, s * fs:(s + 1) * fs]))
        hm = jnp.concatenate([(a[s] * _sigmoid(a[s]) * g[s]).astype(BF16) for s in range(ns)], axis=1)
        ds_ = D // ns
        y = []
        for s in range(ns):
            issue_some()
            y.append(_dot(hm, w2_ref[0, :, s * ds_:(s + 1) * ds_]))
        for thunk in pending:
            thunk()
        return jnp.concatenate(y, axis=1)

    def accumulate(slot, y):
        aff = xbuf[slot, :, D:]
        lane = lax.broadcasted_iota(I32, aff.shape, 1)
        gate = jnp.sum(jnp.where(lane == e, aff, 0.0), axis=1, keepdims=True)
        acc[slot] = acc[slot] + y * gate

    @pl.when(first)
    def _():
        def body(r, carry):
            gather_x(idx0_ref, base0, 0, r).start()
            return carry
        lax.fori_loop(0, cap, body, 0, unroll=8)

    def start(copy_fn, *args):
        return lambda: copy_fn(*args).start()

    wait_x(0)
    starts = []
    for r in range(cap):
        starts.append(start(gather_h, idx0_ref, base0, 0, r))
        starts.append(start(gather_x, idx1_ref, base1, 1, r))
    y = swiglu(0, starts)
    wait_h(0)
    accumulate(0, y)
    wait_x(1)
    starts = []
    for r in range(cap):
        starts.append(start(gather_h, idx1_ref, base1, 1, r))
        starts.append(start(scatter_h, idx0_ref, base0, 0, r))
        starts.append(start(gather_x, idxn_ref, basen, 0, r))
    y = swiglu(1, starts)
    wait_h(1)
    accumulate(1, y)

    def body_out(r, carry):
        scatter_h(idx1_ref, base1, 1, r).start()
        return carry

    lax.fori_loop(0, cap, body_out, 0, unroll=8)
    wait_s(0)
    wait_s(1)

    @pl.when(last)
    def _():
        wait_x(0)


def moe_ffn(idx, hx, h, w1, w3, w2, *, B, S):
    T, D = h.shape
    E = N_EXPERTS
    cap = idx.shape[-1]
    F = w1.shape[-1]
    assert B % 2 == 0
    nj = B // 2
    idx3 = idx.reshape(B * E, 1, cap)

    def idx_spec(fn):
        return pl.BlockSpec((1, 1, cap), fn, memory_space=pltpu.SMEM)

    def nxt(e, j):
        is_last = (e == E - 1) & (j == nj - 1)
        en = jnp.where(is_last, e, e + (j + 1) // nj)
        jn = jnp.where(is_last, j, (j + 1) % nj)
        return (2 * jn * E + en, 0, 0)

    return pl.pallas_call(
        functools.partial(_ffn_kernel, S=S, D=D, cap=cap, nj=nj),
        grid=(E, nj),
        in_specs=[idx_spec(lambda e, j: (2 * j * E + e, 0, 0)),
                  idx_spec(lambda e, j: ((2 * j + 1) * E + e, 0, 0)),
                  idx_spec(nxt),
                  pl.BlockSpec(memory_space=pl.ANY),
                  pl.BlockSpec(memory_space=pl.ANY),
                  pl.BlockSpec((1, D, F), lambda e, j: (e, 0, 0)),
                  pl.BlockSpec((1, D, F), lambda e, j: (e, 0, 0)),
                  pl.BlockSpec((1, F, D), lambda e, j: (e, 0, 0))],
        out_specs=pl.BlockSpec(memory_space=pl.ANY),
        out_shape=jax.ShapeDtypeStruct((T, D), F32),
        scratch_shapes=[pltpu.VMEM((2, cap, D + LANES), F32),
                        pltpu.VMEM((2, cap, D), F32),
                        pltpu.SemaphoreType.DMA((2,)),
                        pltpu.SemaphoreType.DMA((2,)),
                        pltpu.SemaphoreType.DMA((2,))],
        input_output_aliases={4: 0},
        compiler_params=_params(("arbitrary", "arbitrary")),
        name="moe_ffn",
    )(idx3, idx3, idx3, hx, h, w1, w3, w2)


def ec_moe(h, gain, router, w1, w3, w2, *, B, S):
    hx, aff_t = moe_router(h, gain, router, B=B, S=S)
    idx = moe_topk(aff_t, B=B, S=S)
    return moe_ffn(idx, hx, h, w1.astype(BF16), w3.astype(BF16), w2.astype(BF16), B=B, S=S)


def _pad_cols(w, n):
    return jnp.pad(w, ((0, 0), (0, n - w.shape[1])))


def gla_layer(h, norm_gain, w_in, w_gate_up, b_gate, head_norm, w_out, *, B, S):
    nmain = 2 * GLA_HEADS * GLA_DK + 2 * GLA_HEADS * GLA_DV
    proj = norm_matmul(h, norm_gain, w_in[:, :nmain].astype(BF16), name="gla_in")
    glo = norm_matmul(h, norm_gain, _pad_cols(w_in[:, nmain:], LANES).astype(BF16), tn=LANES, name="gla_in_gate")
    wg = jnp.pad(w_gate_up.reshape(2 * GLA_RANK, -1), ((0, LANES - 2 * GLA_RANK), (0, 0))).astype(BF16)
    y = gla_scan(proj, glo, wg, b_gate.astype(F32), head_norm, B=B, S=S)
    return matmul_residual(y, w_out.astype(BF16), h, name="gla_out")


def gdn_layer(h, norm_gain, w_in, conv_w, a_log, dt_bias, head_norm, w_out, *, B, S):
    nconv = conv_w.shape[1]
    nmain = nconv + GDN_V_HEADS * GDN_HD
    proj = norm_matmul(h, norm_gain, w_in[:, :nmain].astype(BF16), name="gdn_in")
    ab = norm_matmul(h, norm_gain, w_in[:, nmain:].astype(BF16), tn=LANES, name="gdn_in_gate")
    qkv = gdn_conv(proj, conv_w, B=B, S=S)
    zeros = jnp.zeros((2 * GDN_V_HEADS,), F32)
    alog_row = jnp.concatenate([a_log.astype(F32).reshape(-1), zeros]).reshape(1, LANES)
    dtb_row = jnp.concatenate([dt_bias.astype(F32).reshape(-1), zeros]).reshape(1, LANES)
    y = gdn_scan(qkv, proj, ab, alog_row, dtb_row, head_norm, B=B, S=S)
    return matmul_residual(y, w_out.astype(BF16), h, name="gdn_out")


def diff_layer(h, norm_gain, w_in, q_norm, k_norm, lam, subln, w_out, bias_tiles, layer_idx, *, B, S):
    proj = norm_matmul(h, norm_gain, w_in.astype(BF16), name="diff_in")
    y = diff_attention(proj, q_norm, k_norm, lam, subln, bias_tiles, B=B, S=S, layer_idx=layer_idx)
    return matmul_residual(y, w_out.astype(BF16), h, name="diff_out")


def swa_layer(h, norm_gain, w_in, q_norm, k_norm, sink, w_out, bias, *, B, S):
    proj = norm_matmul(h, norm_gain, w_in.astype(BF16), name="swa_in")
    y = swa_attention(proj, q_norm, k_norm, sink, bias, B=B, S=S)
    return matmul_residual(y, w_out.astype(BF16), h, name="swa_out")


def kernel(x, rel_bias, norm_mix, norm_ffn, gla_w_in, gla_w_gate_up, gla_b_gate, gla_head_norm, gla_w_out, gdn_w_in, gdn_conv, gdn_a_log, gdn_dt_bias, gdn_head_norm, gdn_w_out, diff_w_in, diff_q_norm, diff_k_norm, diff_lambda, diff_subln, diff_w_out, swa_w_in, swa_q_norm, swa_k_norm, swa_sink, swa_w_out, moe_router, moe_w1, moe_w3, moe_w2):
    B, S, D = x.shape
    depth = norm_mix.shape[0]
    h = x.reshape(B * S, D)
    for i in range(depth):
        m, j = i % 4, i // 4
        if m == 0:
            h = gla_layer(h, norm_mix[i], gla_w_in[j], gla_w_gate_up[j], gla_b_gate[j], gla_head_norm[j],
                          gla_w_out[j], B=B, S=S)
        elif m == 1:
            h = gdn_layer(h, norm_mix[i], gdn_w_in[j], gdn_conv[j], gdn_a_log[j], gdn_dt_bias[j],
                          gdn_head_norm[j], gdn_w_out[j], B=B, S=S)
        elif m == 2:
            h = diff_layer(h, norm_mix[i], diff_w_in[j], diff_q_norm[j], diff_k_norm[j], diff_lambda[j],
                           diff_subln[j], diff_w_out[j], diff_bias_tiles(rel_bias), i, B=B, S=S)
        else:
            h = swa_layer(h, norm_mix[i], swa_w_in[j], swa_q_norm[j], swa_k_norm[j], swa_sink[j],
                          swa_w_out[j], swa_bias(rel_bias), B=B, S=S)
        h = ec_moe(h, norm_ffn[i], moe_router[i], moe_w1[i], moe_w3[i], moe_w2[i], B=B, S=S)
    return h.reshape(B, S, D)
```

```python
import functools
import math

import jax
import jax.numpy as jnp
from jax import lax
from jax.experimental import pallas as pl
from jax.experimental.pallas import tpu as pltpu

F32 = jnp.float32
BF16 = jnp.bfloat16
I32 = jnp.int32
HI = lax.Precision.HIGHEST

RMS_EPS = 1e-6
VMEM_LIMIT_BYTES = 56 * 1024 * 1024
LANES = 128

REL_BUCKETS = 32
REL_MAX_DIST = 128
CHUNK = 64
GLA_HEADS, GLA_DK, GLA_DV, GLA_RANK, GLA_TAU = 4, 256, 512, 16, 16.0
GDN_QK_HEADS, GDN_V_HEADS, GDN_HD, GDN_CONV = 16, 32, 128, 5
GDN_REP = GDN_V_HEADS // GDN_QK_HEADS
TRI_ROWS = 256
DIFF_HEADS, DIFF_DQK, DIFF_DV, DIFF_QB = 16, 64, 128, 256
DIFF_KB = 512
SWA_HEADS, SWA_KV, SWA_GROUP, SWA_HD, SWA_W, SWA_QB = 16, 4, 4, 128, 128, 128
SWA_QPS = 4
N_EXPERTS = 16
EC_CAPACITY_FACTOR = 2
FFN_SLABS = 4


def _params(sem):
    return pltpu.CompilerParams(dimension_semantics=sem, vmem_limit_bytes=VMEM_LIMIT_BYTES)


def _dot(a, b, **kw):
    return jnp.dot(a, b, preferred_element_type=F32, **kw)


def _dot_nt(a, b, **kw):
    return lax.dot_general(a, b, (((1,), (1,)), ((), ())), preferred_element_type=F32, **kw)


def _dot_tn(a, b, **kw):
    return lax.dot_general(a, b, (((0,), (0,)), ((), ())), preferred_element_type=F32, **kw)


def _rms(x):
    return x * lax.rsqrt(jnp.mean(x * x, axis=-1, keepdims=True) + RMS_EPS)


def _sigmoid(x):
    return 1.0 / (1.0 + jnp.exp(-x))


def _softplus(x):
    return jnp.maximum(x, 0.0) + jnp.log(1.0 + jnp.exp(-jnp.abs(x)))


def _norm_matmul_kernel(x_ref, g_ref, w_ref, o_ref, xn_ref):
    @pl.when(pl.program_id(1) == 0)
    def _():
        xn_ref[...] = (_rms(x_ref[...]) * g_ref[...]).astype(BF16)

    o_ref[...] = _dot(xn_ref[...], w_ref[...])


def norm_matmul(x, gain, w, *, tm=1024, tn=512, name):
    T, D = x.shape
    N = w.shape[1]
    tm, tn = min(tm, T), min(tn, N)
    assert T % tm == 0 and N % tn == 0
    return pl.pallas_call(
        _norm_matmul_kernel,
        grid=(T // tm, N // tn),
        in_specs=[pl.BlockSpec((tm, D), lambda i, j: (i, 0)),
                  pl.BlockSpec((1, D), lambda i, j: (0, 0)),
                  pl.BlockSpec((D, tn), lambda i, j: (0, j))],
        out_specs=pl.BlockSpec((tm, tn), lambda i, j: (i, j)),
        out_shape=jax.ShapeDtypeStruct((T, N), F32),
        scratch_shapes=[pltpu.VMEM((tm, D), BF16)],
        compiler_params=_params(("parallel", "arbitrary")),
        name=name,
    )(x, gain.reshape(1, D), w)


def _matmul_res_kernel(y_ref, w_ref, h_ref, o_ref):
    o_ref[...] = h_ref[...] + _dot(y_ref[...].astype(BF16), w_ref[...])


def matmul_residual(y, w, h, *, tm=256, tn=2048, name):
    T, K = y.shape
    N = w.shape[1]
    tm, tn = min(tm, T), min(tn, N)
    assert T % tm == 0 and N % tn == 0
    return pl.pallas_call(
        _matmul_res_kernel,
        grid=(T // tm, N // tn),
        in_specs=[pl.BlockSpec((tm, K), lambda i, j: (i, 0)),
                  pl.BlockSpec((K, tn), lambda i, j: (0, j)),
                  pl.BlockSpec((tm, tn), lambda i, j: (i, j))],
        out_specs=pl.BlockSpec((tm, tn), lambda i, j: (i, j)),
        out_shape=jax.ShapeDtypeStruct((T, N), F32),
        compiler_params=_params(("parallel", "parallel")),
        name=name,
    )(y, w, h)


def _tri_masks(bwd):
    row = lax.broadcasted_iota(I32, (CHUNK, CHUNK), 0)
    col = lax.broadcasted_iota(I32, (CHUNK, CHUNK), 1)
    incl = jnp.where(bwd, (col >= row).astype(F32), (col <= row).astype(F32))
    strict = jnp.where(bwd, (col > row).astype(F32), (col < row).astype(F32))
    return incl, strict


def _gla_kernel(q_ref, k_ref, v_ref, r_ref, glo_ref, wg_ref, bg_ref, hn_ref, trif_ref, trib_ref, o_ref,
                of_ref, st_ref, *, nb, blk):
    i = pl.program_id(2)
    nc = blk // CHUNK
    cs = range(nc)

    @pl.when((i == 0) | (i == nb))
    def _():
        st_ref[...] = jnp.zeros_like(st_ref)

    def scan_block(bwd):
        sb = (2 * nb - 1 - i) if bwd else i
        lane = lax.broadcasted_iota(I32, (blk, LANES), 1)
        lo = GLA_RANK if bwd else 0
        gsel = jnp.where((lane >= lo) & (lane < lo + GLA_RANK), glo_ref[...], 0.0)
        bg = bg_ref[...]
        gate = _dot(gsel.astype(BF16), wg_ref[...]) + (bg[1:2] if bwd else bg[0:1])
        la = (jnp.minimum(gate, 0.0) - jnp.log(1.0 + jnp.exp(-jnp.abs(gate)))) * (1.0 / GLA_TAU)
        tri = trib_ref[...] if bwd else trif_ref[...]
        grp = tri.shape[0]
        pieces = _split3_bf16(la)
        cum_blk = jnp.concatenate([sum(_dot(tri, pc[r:r + grp]) for pc in pieces) for r in range(0, blk, grp)],
                                  axis=0)
        row = lax.broadcasted_iota(I32, (CHUNK, CHUNK), 0)
        col = lax.broadcasted_iota(I32, (CHUNK, CHUNK), 1)
        incl = ((col >= row) if bwd else (col <= row)).astype(F32)
        r0 = [(nc - 1 - c if bwd else c) * CHUNK for c in cs]
        rows = [slice(r, r + CHUNK) for r in r0]
        cum = [cum_blk[rw] for rw in rows]
        tot = [cum_blk[(r if bwd else r + CHUNK - 1):(r + 1 if bwd else r + CHUNK)] for r in r0]
        q = [q_ref[rw, :] * (GLA_DK ** -0.5) for rw in rows]
        k = [k_ref[rw, :] for rw in rows]
        v = [v_ref[rw, :].astype(BF16) for rw in rows]
        qd = [(q[c] * jnp.exp(cum[c])).astype(BF16) for c in cs]
        kin = [(k[c] * jnp.exp(-cum[c])).astype(BF16) for c in cs]
        kst = [(k[c] * jnp.exp(tot[c] - cum[c])).astype(BF16) for c in cs]
        s = [(_dot_nt(qd[c], kin[c]) * incl).astype(BF16) for c in cs]
        o = [_dot(s[c], v[c]) for c in cs]
        st = st_ref[...]
        upd = _dot_tn(v[0], kst[0])
        for c in cs:
            nxt = _dot_tn(v[c + 1], kst[c + 1]) if c + 1 < nc else None
            o[c] = o[c] + _dot_nt(qd[c], st.astype(BF16))
            st = st * jnp.exp(tot[c]) + upd
            upd = nxt
        st_ref[...] = st
        gain = hn_ref[...]
        for c in cs:
            grow = pl.ds(pl.multiple_of(sb * blk + r0[c], CHUNK), CHUNK)
            if bwd:
                ot = of_ref[grow, :] + o[c]
                r = r_ref[rows[c], :]
                o_ref[rows[c], :] = _rms(ot) * gain * (r * _sigmoid(r))
            else:
                of_ref[grow, :] = o[c]

    @pl.when(i < nb)
    def _():
        scan_block(False)

    @pl.when(i >= nb)
    def _():
        scan_block(True)


def gla_scan(proj, glo, wg, bg, head_norm, *, B, S, blk=512):
    T = B * S
    blk = min(blk, S)
    nb = S // blk
    H = GLA_HEADS
    grp = min(TRI_ROWS, blk)

    def rowblk(b, i):
        return b * nb + jnp.where(i >= nb, 2 * nb - 1 - i, i)

    def outblk(b, i):
        return b * nb + jnp.where(i >= nb, 2 * nb - 1 - i, nb - 1)

    kq = GLA_HEADS * GLA_DK // GLA_DK
    kv = 2 * GLA_HEADS * GLA_DK // GLA_DV
    kr = kv + GLA_HEADS
    return pl.pallas_call(
        functools.partial(_gla_kernel, nb=nb, blk=blk),
        grid=(B, H, 2 * nb),
        in_specs=[pl.BlockSpec((blk, GLA_DK), lambda b, h, i: (rowblk(b, i), h)),
                  pl.BlockSpec((blk, GLA_DK), lambda b, h, i: (rowblk(b, i), kq + h)),
                  pl.BlockSpec((blk, GLA_DV), lambda b, h, i: (rowblk(b, i), kv + h)),
                  pl.BlockSpec((blk, GLA_DV), lambda b, h, i: (outblk(b, i), kr + h)),
                  pl.BlockSpec((blk, LANES), lambda b, h, i: (rowblk(b, i), 0)),
                  pl.BlockSpec((LANES, GLA_DK), lambda b, h, i: (0, h)),
                  pl.BlockSpec((2, GLA_DK), lambda b, h, i: (0, h)),
                  pl.BlockSpec((1, GLA_DV), lambda b, h, i: (0, 0)),
                  pl.BlockSpec((grp, grp), lambda b, h, i: (0, 0)),
                  pl.BlockSpec((grp, grp), lambda b, h, i: (0, 0))],
        out_specs=pl.BlockSpec((blk, GLA_DV), lambda b, h, i: (outblk(b, i), h)),
        out_shape=jax.ShapeDtypeStruct((T, H * GLA_DV), F32),
        scratch_shapes=[pltpu.VMEM((S, GLA_DV), F32),
                        pltpu.VMEM((GLA_DV, GLA_DK), F32)],
        compiler_params=_params(("parallel", "parallel", "arbitrary")),
        name="gla_scan",
    )(proj, proj, proj, proj, glo, wg, bg, head_norm.reshape(1, GLA_DV),
      _block_tri(grp, False), _block_tri(grp, True))


def _gdn_conv_kernel(x_ref, w_ref, o_ref, xp_ref, *, S, rows):
    c = pl.program_id(1)
    pad = 8
    xp_ref[0:pad, :] = jnp.zeros((pad, LANES), F32)
    xp_ref[pad + S:2 * pad + S, :] = jnp.zeros((pad, LANES), F32)
    xp_ref[pad:pad + S, :] = x_ref[...]
    w = w_ref[...]
    win = rows + 2 * pad
    is_qk = c < 2 * GDN_QK_HEADS
    scale = jnp.where(c < GDN_QK_HEADS, GDN_HD ** -0.5, 1.0)

    def conv_silu(t):
        r0 = pl.multiple_of(t * rows, rows)
        xw = xp_ref[pl.ds(r0, win), :]
        acc = jnp.zeros((rows, LANES), F32)
        for j in range(GDN_CONV):
            sh = (GDN_CONV // 2 - j) % win
            xs = xw if sh == 0 else pltpu.roll(xw, sh, 0)
            acc = acc + xs[pad:pad + rows, :] * w[j:j + 1, :]
        return r0, acc * _sigmoid(acc)

    @pl.when(is_qk)
    def _():
        def body(t, carry):
            r0, y = conv_silu(t)
            o_ref[pl.ds(r0, rows), :] = y * (lax.rsqrt(jnp.sum(y * y, axis=-1, keepdims=True) + RMS_EPS) * scale)
            return carry
        lax.fori_loop(0, S // rows, body, 0, unroll=2)

    @pl.when(jnp.logical_not(is_qk))
    def _():
        def body(t, carry):
            r0, y = conv_silu(t)
            o_ref[pl.ds(r0, rows), :] = y
            return carry
        lax.fori_loop(0, S // rows, body, 0, unroll=2)


def gdn_conv(proj, conv_w, *, B, S):
    T = B * S
    nch = conv_w.shape[1] // LANES
    rows = min(256, S)
    return pl.pallas_call(
        functools.partial(_gdn_conv_kernel, S=S, rows=rows),
        grid=(B, nch),
        in_specs=[pl.BlockSpec((S, LANES), lambda b, c: (b, c)),
                  pl.BlockSpec((GDN_CONV, LANES), lambda b, c: (0, c))],
        out_specs=pl.BlockSpec((S, LANES), lambda b, c: (b, c)),
        out_shape=jax.ShapeDtypeStruct((T, nch * LANES), F32),
        scratch_shapes=[pltpu.VMEM((S + 16, LANES), F32)],
        compiler_params=_params(("parallel", "parallel")),
        name="gdn_conv",
    )(proj, conv_w)


def _mm_bf16(a, b):
    return _dot(a.astype(BF16), b.astype(BF16))


def _unit_tri_inverses(Ls):
    row = lax.broadcasted_iota(I32, (CHUNK, CHUNK), 0)
    col = lax.broadcasted_iota(I32, (CHUNK, CHUNK), 1)
    eye = (row == col).astype(F32)
    ps = [eye - L for L in Ls]
    pws = [_mm_bf16(L, L) for L in Ls]
    n = 2
    while True:
        ps = [p + _mm_bf16(p, pw) for p, pw in zip(ps, pws)]
        n *= 2
        if n >= CHUNK:
            return ps
        pws = [_mm_bf16(pw, pw) for pw in pws]


def _split3_bf16(x):
    hi = x.astype(BF16)
    r1 = x - hi.astype(F32)
    mid = r1.astype(BF16)
    lo = (r1 - mid.astype(F32)).astype(BF16)
    return hi, mid, lo


def _gdn_prepare(q_ref, k_ref, v_ref, ab_ref, tri_ref, alog, dtb, *, d, qh, blk):
    bwd = d == 1
    nc = blk // CHUNK
    cs = range(nc)
    hs = range(GDN_REP)
    hc = [(h, c) for h in hs for c in cs]
    row = lax.broadcasted_iota(I32, (CHUNK, CHUNK), 0)
    col = lax.broadcasted_iota(I32, (CHUNK, CHUNK), 1)
    incl = (col >= row) if bwd else (col <= row)
    strict = (col > row) if bwd else (col < row)

    sl = [slice(c * CHUNK, (c + 1) * CHUNK) for c in cs]
    q = [q_ref[s, :] for s in sl]
    k = [k_ref[s, :] for s in sl]
    qbf = [t.astype(BF16) for t in q]
    kbf = [t.astype(BF16) for t in k]
    kk = [_dot_nt(kbf[c], kbf[c]) for c in cs]
    qkr = [_dot_nt(qbf[c], kbf[c]) for c in cs]
    x = ab_ref[...]
    gfull = -jnp.exp(alog) * _softplus(x + dtb)
    bfull = _sigmoid(x)
    lane = lax.broadcasted_iota(I32, (blk, LANES), 1)

    def pick(full, ln):
        return jnp.broadcast_to(jnp.sum(jnp.where(lane == ln, full, 0.0), axis=1, keepdims=True), (blk, LANES))

    lane_g = [d * GDN_V_HEADS + GDN_REP * qh + h for h in hs]
    gb = [pick(gfull, ln) for ln in lane_g]
    bb = [pick(bfull, 2 * GDN_V_HEADS + ln) for ln in lane_g]
    tri = tri_ref[...]
    grp = tri.shape[0]
    pieces = [_split3_bf16(g) for g in gb]
    gc_blk = [jnp.concatenate([sum(_dot(tri, pc[r:r + grp]) for pc in pieces[h]) for r in range(0, blk, grp)], axis=0)
              for h in hs]

    last = [c * CHUNK if bwd else (c + 1) * CHUNK - 1 for c in cs]
    gc = [gc_blk[h][sl[c]] for h, c in hc]
    tot = [gc_blk[h][last[c]:last[c] + 1] for h, c in hc]
    beta = [bb[h][sl[c]] for h, c in hc]
    gamma = [jnp.where(incl, jnp.exp(g[:, :CHUNK] - jnp.transpose(g)[:CHUNK, :]), 0.0) for g in gc]
    tinv =_unit_tri_inverses([jnp.where(strict, kk[c] * beta[j][:, :CHUNK] * gamma[j], 0.0)
                               for j, (h, c) in enumerate(hc)])
    egc = [jnp.exp(g) for g in gc]
    rhs = [jnp.concatenate([v_ref[sl[c], h * GDN_HD:(h + 1) * GDN_HD] * beta[j], k[c] * beta[j] * egc[j]], axis=1)
           for j, (h, c) in enumerate(hc)]
    uw = [_mm_bf16(tinv[j], rhs[j]).astype(BF16) for j in range(len(hc))]
    qk = [(qkr[c] * gamma[j]).astype(BF16) for j, (h, c) in enumerate(hc)]
    kst = [(k[c] * jnp.exp(tot[j] - gc[j])).astype(BF16) for j, (h, c) in enumerate(hc)]
    qd = [q[c] * egc[j] for j, (h, c) in enumerate(hc)]
    return dict(uw=uw, qk=qk, kst=kst, qd=qd, dec=[jnp.exp(t) for t in tot])


def _gdn_chunk_operands(p, j):
    kuw = _dot_tn(p["kst"][j], p["uw"][j])
    quw = _dot(p["qk"][j], p["uw"][j])
    qt = (p["qd"][j] - quw[:, GDN_HD:]).astype(BF16)
    return qt, quw[:, :GDN_HD], kuw[:, GDN_HD:].astype(BF16), kuw[:, :GDN_HD], p["dec"][j]


def _gdn_chunk_step(ops, S):
    qt, qu, kw, ku, dec = ops
    sb16 = S.astype(BF16)
    return _dot(qt, sb16) + qu, S * dec - _dot(kw, sb16) + ku


def _gdn_kernel(qf_ref, kf_ref, vf_ref, abf_ref, qb_ref, kb_ref, vb_ref, abb_ref, trif_ref, trib_ref,
                z_ref, alog_ref, dtb_ref, hn_ref, o_ref, of_ref, ob_ref, st_ref, *, nb, blk, S):
    qh = pl.program_id(1)
    i = pl.program_id(2)
    nc = blk // CHUNK

    @pl.when(i == 0)
    def _():
        st_ref[...] = jnp.zeros_like(st_ref)

    alog = alog_ref[...]
    dtb = dtb_ref[...]
    pf = _gdn_prepare(qf_ref, kf_ref, vf_ref, abf_ref, trif_ref, alog, dtb, d=0, qh=qh, blk=blk)
    pb = _gdn_prepare(qb_ref, kb_ref, vb_ref, abb_ref, trib_ref, alog, dtb, d=1, qh=qh, blk=blk)
    sf = [st_ref[0, h] for h in range(GDN_REP)]
    sb = [st_ref[1, h] for h in range(GDN_REP)]
    rowf = i * blk
    rowb = (nb - 1 - i) * blk
    def operands(c):
        return [(_gdn_chunk_operands(pf, h * nc + c), _gdn_chunk_operands(pb, h * nc + nc - 1 - c))
                for h in range(GDN_REP)]

    nxt = operands(0)
    for c in range(nc):
        cb = nc - 1 - c
        cur = nxt
        if c + 1 < nc:
            nxt = operands(c + 1)
        outs = []
        for h in range(GDN_REP):
            of, sf[h] = _gdn_chunk_step(cur[h][0], sf[h])
            ob, sb[h] = _gdn_chunk_step(cur[h][1], sb[h])
            outs.append((of, ob))
        for h, (of, ob) in enumerate(outs):
            cols = slice(h * GDN_HD, (h + 1) * GDN_HD)
            of_ref[pl.ds(pl.multiple_of(rowf + c * CHUNK, CHUNK), CHUNK), cols] = of
            ob_ref[pl.ds(pl.multiple_of(rowb + cb * CHUNK, CHUNK), CHUNK), cols] = ob
    for h in range(GDN_REP):
        st_ref[0, h] = sf[h]
        st_ref[1, h] = sb[h]

    @pl.when(i == nb - 1)
    def _():
        gain = hn_ref[...]
        rows_e = min(256, S)

        def ebody(t, carry):
            rows = pl.ds(pl.multiple_of(t * rows_e, rows_e), rows_e)
            for h in range(GDN_REP):
                cols = slice(h * GDN_HD, (h + 1) * GDN_HD)
                ot = of_ref[rows, cols] + ob_ref[rows, cols]
                z = z_ref[rows, cols]
                o_ref[rows, cols] = _rms(ot) * gain * (z * _sigmoid(z))
            return carry

        lax.fori_loop(0, S // rows_e, ebody, 0)


def _block_tri(n, bwd):
    r = jnp.arange(n, dtype=I32)[:, None]
    c = jnp.arange(n, dtype=I32)[None, :]
    same = (r // CHUNK) == (c // CHUNK)
    return (same & ((c >= r) if bwd else (c <= r))).astype(BF16)


def gdn_scan(qkv, proj, ab, alog_row, dtb_row, head_norm, *, B, S, blk=512):
    T = B * S
    blk = min(blk, S)
    nb = S // blk
    grp = min(TRI_ROWS, blk)
    vw = GDN_REP * GDN_HD
    voff = 2 * GDN_QK_HEADS * GDN_HD // vw
    zoff = (2 * GDN_QK_HEADS + GDN_V_HEADS) * GDN_HD // vw

    def fwd(b, i):
        return b * nb + i

    def bwd(b, i):
        return b * nb + nb - 1 - i

    def dir_specs(rb):
        return [pl.BlockSpec((blk, GDN_HD), lambda b, h, i: (rb(b, i), h)),
                pl.BlockSpec((blk, GDN_HD), lambda b, h, i: (rb(b, i), GDN_QK_HEADS + h)),
                pl.BlockSpec((blk, vw), lambda b, h, i: (rb(b, i), voff + h)),
                pl.BlockSpec((blk, LANES), lambda b, h, i: (rb(b, i), 0))]

    const2 = lambda b, h, i: (0, 0)
    return pl.pallas_call(
        functools.partial(_gdn_kernel, nb=nb, blk=blk, S=S),
        grid=(B, GDN_QK_HEADS, nb),
        in_specs=dir_specs(fwd) + dir_specs(bwd) + [
            pl.BlockSpec((grp, grp), const2),
            pl.BlockSpec((grp, grp), const2),
            pl.BlockSpec((S, vw), lambda b, h, i: (b, zoff + h)),
            pl.BlockSpec((1, LANES), const2),
            pl.BlockSpec((1, LANES), const2),
            pl.BlockSpec((1, GDN_HD), const2)],
        out_specs=pl.BlockSpec((S, vw), lambda b, h, i: (b, h)),
        out_shape=jax.ShapeDtypeStruct((T, GDN_V_HEADS * GDN_HD), F32),
        scratch_shapes=[pltpu.VMEM((S, vw), F32),
                        pltpu.VMEM((S, vw), F32),
                        pltpu.VMEM((2, GDN_REP, GDN_HD, GDN_HD), F32)],
        compiler_params=_params(("parallel", "parallel", "arbitrary")),
        name="gdn_scan",
    )(qkv, qkv, qkv, ab, qkv, qkv, qkv, ab, _block_tri(grp, False), _block_tri(grp, True),
      proj, alog_row, dtb_row, head_norm.reshape(1, GDN_HD))


def _t5_bucket(rel):
    half = REL_BUCKETS // 2
    max_exact = half // 2
    n = jnp.abs(rel)
    log_ratio = jnp.log(jnp.maximum(n, 1).astype(F32) / max_exact) / math.log(REL_MAX_DIST / max_exact)
    large = jnp.minimum(max_exact + (log_ratio * (half - max_exact)).astype(I32), half - 1)
    return jnp.where(rel > 0, half, 0) + jnp.where(n < max_exact, n, large)


def _rel_bias_heads(table, rel):
    return jnp.moveaxis(table[_t5_bucket(rel)].astype(F32), -1, 0)


def _toeplitz(w, n, m, off):
    lw = w.shape[-1]
    assert lw == n + m - 1 and m <= lw - 1
    w_rot = jnp.roll(w, -off, axis=-1)
    flat = jnp.tile(w_rot, (1,) * (w.ndim - 1) + (n,))[..., :n * (lw - 1)]
    return flat.reshape(w.shape[:-1] + (n, lw - 1))[..., :m]


def _half_rms(x, ones_bd):
    ms = _dot(x * x, ones_bd, precision=HI) * (1.0 / DIFF_DQK)
    return x * lax.rsqrt(ms + RMS_EPS)


LOG2E = math.log2(math.e)


def _diff_kernel(q_ref, k_ref, v_ref, qn_ref, kn_ref, lam_ref, sub_ref, bias_ref, o_ref,
                 kb_ref, vb_ref, m_ref, l_ref, acc_ref, s_ref, *, nk, lambda_init):
    i = pl.program_id(2)
    qb = DIFF_QB
    r = lax.broadcasted_iota(I32, (LANES, LANES), 0) // DIFF_DQK
    c = lax.broadcasted_iota(I32, (LANES, LANES), 1) // DIFF_DQK
    ones_bd = (r == c).astype(F32)

    @pl.when(i == 0)
    def _():
        def kbody(t, carry):
            rows = pl.ds(pl.multiple_of(t * qb, qb), qb)
            kb_ref[rows, :] = (_half_rms(k_ref[rows, :], ones_bd) * kn_ref[...]).astype(BF16)
            vb_ref[rows, :] = v_ref[rows, :].astype(BF16)
            return carry
        lax.fori_loop(0, nk, kbody, 0)

    q = _half_rms(q_ref[...], ones_bd) * qn_ref[...] * (DIFF_DQK ** -0.5 * LOG2E)
    lane = lax.broadcasted_iota(I32, (qb, LANES), 1)
    qs = (jnp.where(lane < DIFF_DQK, q, 0.0).astype(BF16), jnp.where(lane >= DIFF_DQK, q, 0.0).astype(BF16))

    m_ref[...] = jnp.full(m_ref.shape, -jnp.inf, F32)
    l_ref[...] = jnp.zeros(l_ref.shape, F32)
    acc_ref[...] = jnp.zeros(acc_ref.shape, F32)

    kw = DIFF_KB // qb
    maps = range(2)

    nsteps = nk * qb // DIFF_KB

    def key_rows(t):
        return pl.ds(t * DIFF_KB, DIFF_KB)

    def scores(t, slot):
        kc = kb_ref[key_rows(t), :]
        bias = jnp.concatenate([bias_ref[0, jnp.clip(kw * t + u - i, -2, 2) + 2] for u in range(kw)], axis=1)
        for mi in maps:
            s_ref[slot, mi] = _dot_nt(qs[mi], kc) + bias

    def chunk(t, prefetch):
        slot = t % 2
        s = [s_ref[slot, mi] for mi in maps]
        if prefetch:
            scores(t + 1, 1 - slot)
        vc = vb_ref[key_rows(t), :]
        m_cur = [jnp.max(s[mi], axis=1, keepdims=True) for mi in maps]
        m_prev = [m_ref[mi] for mi in maps]
        m_new = [jnp.maximum(m_prev[mi], m_cur[mi]) for mi in maps]
        alpha = [jnp.exp2(m_prev[mi] - m_new[mi]) for mi in maps]
        p = [jnp.exp2(s[mi] - jnp.concatenate([m_new[mi]] * (DIFF_KB // LANES), axis=1)) for mi in maps]
        psum = [jnp.sum(p[mi], axis=1, keepdims=True) for mi in maps]
        pv = [_dot(p[mi].astype(BF16), vc) for mi in maps]
        for mi in maps:
            l_ref[mi] = alpha[mi] * l_ref[mi] + psum[mi]
            acc_ref[mi] = alpha[mi] * acc_ref[mi] + pv[mi]
            m_ref[mi] = m_new[mi]

    scores(0, 0)
    for t in range(nsteps - 1):
        chunk(t, True)
    chunk(nsteps - 1, False)

    lam = lam_ref[...]
    lam_full = (jnp.exp(jnp.sum(lam[0:1] * lam[1:2], axis=-1, keepdims=True))
                - jnp.exp(jnp.sum(lam[2:3] * lam[3:4], axis=-1, keepdims=True)) + lambda_init)
    o = acc_ref[0] / l_ref[0] - lam_full * (acc_ref[1] / l_ref[1])
    o_ref[...] = _rms(o) * sub_ref[...] * (1.0 - lambda_init)


def diff_attention(proj, q_norm, k_norm, lam, subln, bias_tiles, *, B, S, layer_idx):
    T = B * S
    qb = DIFF_QB
    nq = S // qb
    H = DIFF_HEADS
    lambda_init = 0.8 - 0.6 * math.exp(-0.3 * layer_idx)
    qn2 = jnp.concatenate([q_norm, q_norm]).reshape(1, LANES)
    kn2 = jnp.concatenate([k_norm, k_norm]).reshape(1, LANES)
    return pl.pallas_call(
        functools.partial(_diff_kernel, nk=nq, lambda_init=lambda_init),
        grid=(B, H, nq),
        in_specs=[pl.BlockSpec((qb, LANES), lambda b, h, i: (b * nq + i, h)),
                  pl.BlockSpec((S, LANES), lambda b, h, i: (b, H + h)),
                  pl.BlockSpec((S, LANES), lambda b, h, i: (b, 2 * H + h)),
                  pl.BlockSpec((1, LANES), lambda b, h, i: (0, 0)),
                  pl.BlockSpec((1, LANES), lambda b, h, i: (0, 0)),
                  pl.BlockSpec((4, DIFF_DQK), lambda b, h, i: (0, 0)),
                  pl.BlockSpec((1, DIFF_DV), lambda b, h, i: (0, 0)),
                  pl.BlockSpec((1, 5, qb, qb), lambda b, h, i: (h, 0, 0, 0))],
        out_specs=pl.BlockSpec((qb, DIFF_DV), lambda b, h, i: (b * nq + i, h)),
        out_shape=jax.ShapeDtypeStruct((T, H * DIFF_DV), F32),
        scratch_shapes=[pltpu.VMEM((S, LANES), BF16),
                        pltpu.VMEM((S, DIFF_DV), BF16),
                        pltpu.VMEM((2, qb, LANES), F32),
                        pltpu.VMEM((2, qb, LANES), F32),
                        pltpu.VMEM((2, qb, DIFF_DV), F32),
                        pltpu.VMEM((2, 2, qb, DIFF_KB), F32)],
        compiler_params=_params(("parallel", "parallel", "arbitrary")),
        name="diff_attn",
    )(proj, proj, proj, qn2, kn2, lam, subln.reshape(1, DIFF_DV), bias_tiles)


def diff_bias_tiles(table):
    qb = DIFF_QB
    span = 3 * qb - 1
    vec = _rel_bias_heads(table, jnp.arange(-span, span + 1, dtype=I32)) * LOG2E
    w = jnp.stack([vec[:, (d + 2) * qb:(d + 2) * qb + 2 * qb - 1] for d in range(-2, 3)], axis=1)
    return _toeplitz(w, qb, qb, qb - 1)


def _swa_kernel(q_ref, *refs, nq, S):
    nkb = SWA_QPS + 2
    k_refs, v_refs = refs[:nkb], refs[nkb:2 * nkb]
    qn_ref, kn_ref, sink_ref, bias_ref, o_ref = refs[2 * nkb:]
    i = pl.program_id(2)
    qb = SWA_QB
    span = 3 * qb
    kcat = jnp.concatenate([r[...] for r in k_refs], axis=0)
    kcat = (_rms(kcat) * kn_ref[...]).astype(BF16)
    vcat = jnp.concatenate([r[...] for r in v_refs], axis=0).astype(BF16)
    row = lax.broadcasted_iota(I32, (qb, span), 0)
    col = lax.broadcasted_iota(I32, (qb, span), 1)
    rel = col - SWA_W - row
    in_window = jnp.abs(rel) <= SWA_W
    sink_all = sink_ref[0]
    pairs = [(a, g) for a in range(SWA_QPS) for g in range(SWA_GROUP)]
    valid = []
    for a in range(SWA_QPS):
        key_pos = (i * SWA_QPS + a) * qb - SWA_W + col
        valid.append(in_window & (key_pos >= 0) & (key_pos < S))
    q = [q_ref[a * qb:(a + 1) * qb, g * SWA_HD:(g + 1) * SWA_HD] for a, g in pairs]
    q = [(_rms(t) * qn_ref[...] * (SWA_HD ** -0.5)).astype(BF16) for t in q]
    s = [_dot_nt(q[j], kcat[a * qb:a * qb + span]) + bias_ref[g] for j, (a, g) in enumerate(pairs)]
    s = [jnp.where(valid[a], s[j], -jnp.inf) for j, (a, g) in enumerate(pairs)]
    sink = [sink_all[g:g + 1, 0:1] for a, g in pairs]
    js = range(len(pairs))
    m = [jnp.maximum(jnp.max(s[j], axis=-1, keepdims=True), sink[j]) for j in js]
    p = [jnp.exp(s[j] - m[j]) for j in js]
    den = [jnp.sum(p[j], axis=-1, keepdims=True) + jnp.exp(sink[j] - m[j]) for j in js]
    o = [_dot(p[j].astype(BF16), vcat[a * qb:a * qb + span]) / den[j] for j, (a, g) in enumerate(pairs)]
    for j, (a, g) in enumerate(pairs):
        o_ref[a * qb:(a + 1) * qb, g * SWA_HD:(g + 1) * SWA_HD] = o[j]


def swa_attention(proj, q_norm, k_norm, sink, bias, *, B, S):
    T = B * S
    qb = SWA_QB
    nq = S // qb
    koff = SWA_HEADS
    voff = SWA_HEADS + SWA_KV
    gw = SWA_GROUP * SWA_HD

    qps = SWA_QPS
    assert nq % qps == 0
    ns = nq // qps

    def kvspec(off, d):
        return pl.BlockSpec((qb, SWA_HD),
                            lambda b, kv, i: (b * nq + jnp.clip(i * qps + d, 0, nq - 1), off + kv))

    kv_specs = [kvspec(off, d) for off in (koff, voff) for d in range(-1, qps + 1)]
    sink_b = jnp.broadcast_to(sink.astype(F32).reshape(SWA_KV, SWA_GROUP, 1), (SWA_KV, SWA_GROUP, LANES))
    return pl.pallas_call(
        functools.partial(_swa_kernel, nq=nq, S=S),
        grid=(B, SWA_KV, ns),
        in_specs=[pl.BlockSpec((qps * qb, gw), lambda b, kv, i: (b * ns + i, kv))] + kv_specs + [
                  pl.BlockSpec((1, SWA_HD), lambda b, kv, i: (0, 0)),
                  pl.BlockSpec((1, SWA_HD), lambda b, kv, i: (0, 0)),
                  pl.BlockSpec((1, SWA_GROUP, LANES), lambda b, kv, i: (kv, 0, 0)),
                  pl.BlockSpec((SWA_GROUP, qb, 3 * qb), lambda b, kv, i: (kv, 0, 0))],
        out_specs=pl.BlockSpec((qps * qb, gw), lambda b, kv, i: (b * ns + i, kv)),
        out_shape=jax.ShapeDtypeStruct((T, SWA_HEADS * SWA_HD), F32),
        compiler_params=_params(("parallel", "parallel", "parallel")),
        name="swa_attn",
    )(*([proj] * (1 + 2 * (qps + 2))), q_norm.reshape(1, SWA_HD), k_norm.reshape(1, SWA_HD), sink_b, bias)


def swa_bias(table):
    qb = SWA_QB
    span = 3 * qb
    vec = _rel_bias_heads(table, jnp.arange(-(qb - 1) - SWA_W, span - SWA_W, dtype=I32))
    return _toeplitz(vec, qb, span, qb - 1)


def _router_kernel(h_ref, g_ref, wr_ref, hx_ref, at_ref, *, D):
    hn = _rms(h_ref[...]) * g_ref[...]
    x_hi = hn.astype(BF16)
    x_lo = (hn - x_hi.astype(F32)).astype(BF16)
    w = wr_ref[...]
    w_hi = w.astype(BF16)
    w_lo = (w - w_hi.astype(F32)).astype(BF16)
    logits = (_dot(x_hi, w_hi) + _dot(x_hi, w_lo)) + (_dot(x_lo, w_hi) + _dot(x_lo, w_lo))
    lane = lax.broadcasted_iota(I32, logits.shape, 1)
    logits = jnp.where(lane < N_EXPERTS, logits, -jnp.inf)
    m = jnp.max(logits, axis=-1, keepdims=True)
    e = jnp.exp(logits - m)
    aff = e / jnp.sum(e, axis=-1, keepdims=True)
    hx_ref[:, :D] = hn
    hx_ref[:, D:] = aff
    at_ref[0] = jnp.transpose(aff)


def moe_router(h, gain, router, *, B, S, tm=512):
    T, D = h.shape
    tm = min(tm, S)
    ns = S // tm
    wr = jnp.pad(router.astype(F32), ((0, 0), (0, LANES - N_EXPERTS)))
    return pl.pallas_call(
        functools.partial(_router_kernel, D=D),
        grid=(B, ns),
        in_specs=[pl.BlockSpec((tm, D), lambda b, s: (b * ns + s, 0)),
                  pl.BlockSpec((1, D), lambda b, s: (0, 0)),
                  pl.BlockSpec((D, LANES), lambda b, s: (0, 0))],
        out_specs=[pl.BlockSpec((tm, D + LANES), lambda b, s: (b * ns + s, 0)),
                   pl.BlockSpec((1, LANES, tm), lambda b, s: (b, 0, s))],
        out_shape=[jax.ShapeDtypeStruct((T, D + LANES), F32),
                   jax.ShapeDtypeStruct((B, LANES, S), F32)],
        compiler_params=_params(("parallel", "parallel")),
        name="moe_router",
    )(h, gain.reshape(1, D), wr)


def _topk_kernel(aff_ref, idx_ref, pos_ref, *, S, cap):
    E = N_EXPERTS
    v = aff_ref[0]
    bits = pltpu.bitcast(v, I32)

    def search(_, carry):
        lo, hi = carry
        mid = lo + ((hi - lo) >> 1)
        cnt = jnp.sum((bits >= mid).astype(F32), axis=1, keepdims=True)
        ok = cnt >= cap
        return jnp.where(ok, mid, lo), jnp.where(ok, hi, mid)

    lo0 = jnp.zeros((E, 1), I32)
    hi0 = jnp.full((E, 1), 0x7F800001, I32)
    thr, _ = lax.fori_loop(0, 32, search, (lo0, hi0))
    gt = bits > thr
    eq = bits == thr
    need = cap - jnp.sum(gt.astype(F32), axis=1, keepdims=True)

    r = lax.broadcasted_iota(I32, (LANES, LANES), 0)
    c = lax.broadcasted_iota(I32, (LANES, LANES), 1)
    upper = (r < c).astype(BF16)
    run_e = jnp.zeros((E, 1), F32)
    run_s = jnp.zeros((E, 1), F32)
    sub = 512 // LANES
    for t in range(S // LANES):
        sl = slice(t * LANES, (t + 1) * LANES)
        eq_t = eq[:, sl].astype(F32)
        pe = _dot(eq_t.astype(BF16), upper) + run_e
        sel_t = jnp.where(gt[:, sl], 1.0, jnp.where(pe < need, eq_t, 0.0))
        ps = _dot(sel_t.astype(BF16), upper) + run_s
        pos_ref[t // sub, :, (t % sub) * LANES:(t % sub + 1) * LANES] = jnp.where(sel_t > 0, ps, -1.0)
        run_e = run_e + jnp.sum(eq_t, axis=1, keepdims=True)
        run_s = run_s + jnp.sum(sel_t, axis=1, keepdims=True)

    pi = lax.broadcasted_iota(I32, (cap, 512), 0).astype(F32)
    lane = lax.broadcasted_iota(I32, (8, 512), 1)
    rowv = lax.broadcasted_iota(I32, (8, 512), 0)

    def per_expert(e, carry):
        parts = []
        for t in range(S // 512):
            pos = pos_ref[t, pl.ds(e, 1), :]
            onehot = (pi == pos).astype(BF16)
            tok = t * 512 + lane
            vals = jnp.where(rowv == 0, tok >> 6, jnp.where(rowv == 1, tok & 63, 0)).astype(F32).astype(BF16)
            parts.append(_dot_nt(vals, onehot))
        acc = sum(parts)
        idx_ref[0, pl.ds(e, 1), :] = (acc[0:1] * 64.0 + acc[1:2]).astype(I32)
        return carry

    lax.fori_loop(0, E, per_expert, 0)


def moe_topk(aff_t, *, B, S):
    cap = EC_CAPACITY_FACTOR * S // N_EXPERTS
    return pl.pallas_call(
        functools.partial(_topk_kernel, S=S, cap=cap),
        grid=(B,),
        in_specs=[pl.BlockSpec((1, N_EXPERTS, S), lambda b: (b, 0, 0))],
        out_specs=pl.BlockSpec((1, N_EXPERTS, cap), lambda b: (b, 0, 0)),
        out_shape=jax.ShapeDtypeStruct((B, N_EXPERTS, cap), I32),
        scratch_shapes=[pltpu.VMEM((S // 512, N_EXPERTS, 512), F32)],
        compiler_params=_params(("parallel",)),
        name="moe_topk",
    )(aff_t)


def _ffn_kernel(idx0_ref, idx1_ref, idxn_ref, hx_hbm, h_in, w1_ref, w3_ref, w2_ref, h_out,
                xbuf, acc, sem_x, sem_h, sem_s, *, S, D, cap, nj):
    del h_in
    e = pl.program_id(0)
    j = pl.program_id(1)
    first = (e == 0) & (j == 0)
    last = (e == pl.num_programs(0) - 1) & (j == nj - 1)
    base0 = (2 * j) * S
    base1 = base0 + S
    jn = jnp.where(last, j, (j + 1) % nj)
    basen = (2 * jn) * S

    def gather_x(idx_ref, base, slot, r):
        return pltpu.make_async_copy(hx_hbm.at[pl.ds(base + idx_ref[0, 0, r], 1), :],
                                     xbuf.at[slot, pl.ds(r, 1), :], sem_x.at[slot])

    def gather_h(idx_ref, base, slot, r):
        return pltpu.make_async_copy(h_out.at[pl.ds(base + idx_ref[0, 0, r], 1), :],
                                     acc.at[slot, pl.ds(r, 1), :], sem_h.at[slot])

    def scatter_h(idx_ref, base, slot, r):
        return pltpu.make_async_copy(acc.at[slot, pl.ds(r, 1), :],
                                     h_out.at[pl.ds(base + idx_ref[0, 0, r], 1), :], sem_s.at[slot])

    def wait_x(slot):
        pltpu.make_async_copy(hx_hbm.at[pl.ds(0, cap), :], xbuf.at[slot], sem_x.at[slot]).wait()

    def wait_h(slot):
        pltpu.make_async_copy(h_out.at[pl.ds(0, cap), :], acc.at[slot], sem_h.at[slot]).wait()

    def wait_s(slot):
        pltpu.make_async_copy(acc.at[slot], h_out.at[pl.ds(0, cap), :], sem_s.at[slot]).wait()

    def swiglu(slot, starts):
        F = w1_ref.shape[-1]
        ns = FFN_SLABS
        per = -(-len(starts) // (3 * ns))
        pending = list(starts)

        def issue_some():
            for thunk in pending[:per]:
                thunk()
            del pending[:per]

        x = xbuf[slot, :, :D].astype(BF16)
        fs = F // ns
        a, g = [], []
        for s in range(ns):
            issue_some()
            a.append(_dot(x, w1_ref[0, :, s * fs:(s + 1) * fs]))
        for s in range(ns):
            issue_some()
            g.append(_dot(x, w3_ref[0, :, s * fs:(s + 1) * fs]))
        hm = jnp.concatenate([(a[s] * _sigmoid(a[s]) * g[s]).astype(BF16) for s in range(ns)], axis=1)
        ds_ = D // ns
        y = []
        for s in range(ns):
            issue_some()
            y.append(_dot(hm, w2_ref[0, :, s * ds_:(s + 1) * ds_]))
        for thunk in pending:
            thunk()
        return jnp.concatenate(y, axis=1)

    def accumulate(slot, y):
        aff = xbuf[slot, :, D:]
        lane = lax.broadcasted_iota(I32, aff.shape, 1)
        gate = jnp.sum(jnp.where(lane == e, aff, 0.0), axis=1, keepdims=True)
        acc[slot] = acc[slot] + y * gate

    @pl.when(first)
    def _():
        def body(r, carry):
            gather_x(idx0_ref, base0, 0, r).start()
            return carry
        lax.fori_loop(0, cap, body, 0, unroll=8)

    def start(copy_fn, *args):
        return lambda: copy_fn(*args).start()

    wait_x(0)
    starts = []
    for r in range(cap):
        starts.append(start(gather_h, idx0_ref, base0, 0, r))
        starts.append(start(gather_x, idx1_ref, base1, 1, r))
    y = swiglu(0, starts)
    wait_h(0)
    accumulate(0, y)
    wait_x(1)
    starts = []
    for r in range(cap):
        starts.append(start(gather_h, idx1_ref, base1, 1, r))
        starts.append(start(scatter_h, idx0_ref, base0, 0, r))
        starts.append(start(gather_x, idxn_ref, basen, 0, r))
    y = swiglu(1, starts)
    wait_h(1)
    accumulate(1, y)

    def body_out(r, carry):
        scatter_h(idx1_ref, base1, 1, r).start()
        return carry

    lax.fori_loop(0, cap, body_out, 0, unroll=8)
    wait_s(0)
    wait_s(1)

    @pl.when(last)
    def _():
        wait_x(0)


def moe_ffn(idx, hx, h, w1, w3, w2, *, B, S):
    T, D = h.shape
    E = N_EXPERTS
    cap = idx.shape[-1]
    F = w1.shape[-1]
    assert B % 2 == 0
    nj = B // 2
    idx3 = idx.reshape(B * E, 1, cap)

    def idx_spec(fn):
        return pl.BlockSpec((1, 1, cap), fn, memory_space=pltpu.SMEM)

    def nxt(e, j):
        is_last = (e == E - 1) & (j == nj - 1)
        en = jnp.where(is_last, e, e + (j + 1) // nj)
        jn = jnp.where(is_last, j, (j + 1) % nj)
        return (2 * jn * E + en, 0, 0)

    return pl.pallas_call(
        functools.partial(_ffn_kernel, S=S, D=D, cap=cap, nj=nj),
        grid=(E, nj),
        in_specs=[idx_spec(lambda e, j: (2 * j * E + e, 0, 0)),
                  idx_spec(lambda e, j: ((2 * j + 1) * E + e, 0, 0)),
                  idx_spec(nxt),
                  pl.BlockSpec(memory_space=pl.ANY),
                  pl.BlockSpec(memory_space=pl.ANY),
                  pl.BlockSpec((1, D, F), lambda e, j: (e, 0, 0)),
                  pl.BlockSpec((1, D, F), lambda e, j: (e, 0, 0)),
                  pl.BlockSpec((1, F, D), lambda e, j: (e, 0, 0))],
        out_specs=pl.BlockSpec(memory_space=pl.ANY),
        out_shape=jax.ShapeDtypeStruct((T, D), F32),
        scratch_shapes=[pltpu.VMEM((2, cap, D + LANES), F32),
                        pltpu.VMEM((2, cap, D), F32),
                        pltpu.SemaphoreType.DMA((2,)),
                        pltpu.SemaphoreType.DMA((2,)),
                        pltpu.SemaphoreType.DMA((2,))],
        input_output_aliases={4: 0},
        compiler_params=_params(("arbitrary", "arbitrary")),
        name="moe_ffn",
    )(idx3, idx3, idx3, hx, h, w1, w3, w2)


def ec_moe(h, gain, router, w1, w3, w2, *, B, S):
    hx, aff_t = moe_router(h, gain, router, B=B, S=S)
    idx = moe_topk(aff_t, B=B, S=S)
    return moe_ffn(idx, hx, h, w1.astype(BF16), w3.astype(BF16), w2.astype(BF16), B=B, S=S)


def _pad_cols(w, n):
    return jnp.pad(w, ((0, 0), (0, n - w.shape[1])))


def gla_layer(h, norm_gain, w_in, w_gate_up, b_gate, head_norm, w_out, *, B, S):
    nmain = 2 * GLA_HEADS * GLA_DK + 2 * GLA_HEADS * GLA_DV
    proj = norm_matmul(h, norm_gain, w_in[:, :nmain].astype(BF16), name="gla_in")
    glo = norm_matmul(h, norm_gain, _pad_cols(w_in[:, nmain:], LANES).astype(BF16), tn=LANES, name="gla_in_gate")
    wg = jnp.pad(w_gate_up.reshape(2 * GLA_RANK, -1), ((0, LANES - 2 * GLA_RANK), (0, 0))).astype(BF16)
    y = gla_scan(proj, glo, wg, b_gate.astype(F32), head_norm, B=B, S=S)
    return matmul_residual(y, w_out.astype(BF16), h, name="gla_out")


def gdn_layer(h, norm_gain, w_in, conv_w, a_log, dt_bias, head_norm, w_out, *, B, S):
    nconv = conv_w.shape[1]
    nmain = nconv + GDN_V_HEADS * GDN_HD
    proj = norm_matmul(h, norm_gain, w_in[:, :nmain].astype(BF16), name="gdn_in")
    ab = norm_matmul(h, norm_gain, w_in[:, nmain:].astype(BF16), tn=LANES, name="gdn_in_gate")
    qkv = gdn_conv(proj, conv_w, B=B, S=S)
    zeros = jnp.zeros((2 * GDN_V_HEADS,), F32)
    alog_row = jnp.concatenate([a_log.astype(F32).reshape(-1), zeros]).reshape(1, LANES)
    dtb_row = jnp.concatenate([dt_bias.astype(F32).reshape(-1), zeros]).reshape(1, LANES)
    y = gdn_scan(qkv, proj, ab, alog_row, dtb_row, head_norm, B=B, S=S)
    return matmul_residual(y, w_out.astype(BF16), h, name="gdn_out")


def diff_layer(h, norm_gain, w_in, q_norm, k_norm, lam, subln, w_out, bias_tiles, layer_idx, *, B, S):
    proj = norm_matmul(h, norm_gain, w_in.astype(BF16), name="diff_in")
    y = diff_attention(proj, q_norm, k_norm, lam, subln, bias_tiles, B=B, S=S, layer_idx=layer_idx)
    return matmul_residual(y, w_out.astype(BF16), h, name="diff_out")


def swa_layer(h, norm_gain, w_in, q_norm, k_norm, sink, w_out, bias, *, B, S):
    proj = norm_matmul(h, norm_gain, w_in.astype(BF16), name="swa_in")
    y = swa_attention(proj, q_norm, k_norm, sink, bias, B=B, S=S)
    return matmul_residual(y, w_out.astype(BF16), h, name="swa_out")


def kernel(x, rel_bias, norm_mix, norm_ffn, gla_w_in, gla_w_gate_up, gla_b_gate, gla_head_norm, gla_w_out, gdn_w_in, gdn_conv, gdn_a_log, gdn_dt_bias, gdn_head_norm, gdn_w_out, diff_w_in, diff_q_norm, diff_k_norm, diff_lambda, diff_subln, diff_w_out, swa_w_in, swa_q_norm, swa_k_norm, swa_sink, swa_w_out, moe_router, moe_w1, moe_w3, moe_w2):
    B, S, D = x.shape
    depth = norm_mix.shape[0]
    h = x.reshape(B * S, D)
    for i in range(depth):
        m, j = i % 4, i // 4
        if m == 0:
            h = gla_layer(h, norm_mix[i], gla_w_in[j], gla_w_gate_up[j], gla_b_gate[j], gla_head_norm[j],
                          gla_w_out[j], B=B, S=S)
        elif m == 1:
            h = gdn_layer(h, norm_mix[i], gdn_w_in[j], gdn_conv[j], gdn_a_log[j], gdn_dt_bias[j],
                          gdn_head_norm[j], gdn_w_out[j], B=B, S=S)
        elif m == 2:
            h = diff_layer(h, norm_mix[i], diff_w_in[j], diff_q_norm[j], diff_k_norm[j], diff_lambda[j],
                           diff_subln[j], diff_w_out[j], diff_bias_tiles(rel_bias), i, B=B, S=S)
        else:
            h = swa_layer(h, norm_mix[i], swa_w_in[j], swa_q_norm[j], swa_k_norm[j], swa_sink[j],
                          swa_w_out[j], swa_bias(rel_bias), B=B, S=S)
        h = ec_moe(h, norm_ffn[i], moe_router[i], moe_w1[i], moe_w3[i], moe_w2[i], B=B, S=S)
    return h.reshape(B, S, D)
```

```python
import functools
import math

import jax
import jax.numpy as jnp
from jax import lax
from jax.experimental import pallas as pl
from jax.experimental.pallas import tpu as pltpu

F32 = jnp.float32
BF16 = jnp.bfloat16
I32 = jnp.int32
HI = lax.Precision.HIGHEST

RMS_EPS = 1e-6
VMEM_LIMIT_BYTES = 56 * 1024 * 1024
LANES = 128

REL_BUCKETS = 32
REL_MAX_DIST = 128
CHUNK = 64
GLA_HEADS, GLA_DK, GLA_DV, GLA_RANK, GLA_TAU = 4, 256, 512, 16, 16.0
GDN_QK_HEADS, GDN_V_HEADS, GDN_HD, GDN_CONV = 16, 32, 128, 5
GDN_REP = GDN_V_HEADS // GDN_QK_HEADS
TRI_ROWS = 256
DIFF_HEADS, DIFF_DQK, DIFF_DV, DIFF_QB = 16, 64, 128, 256
DIFF_KB = 512
SWA_HEADS, SWA_KV, SWA_GROUP, SWA_HD, SWA_W, SWA_QB = 16, 4, 4, 128, 128, 128
SWA_QPS = 4
N_EXPERTS = 16
EC_CAPACITY_FACTOR = 2
FFN_SLABS = 4


def _params(sem):
    return pltpu.CompilerParams(dimension_semantics=sem, vmem_limit_bytes=VMEM_LIMIT_BYTES)


def _dot(a, b, **kw):
    return jnp.dot(a, b, preferred_element_type=F32, **kw)


def _dot_nt(a, b, **kw):
    return lax.dot_general(a, b, (((1,), (1,)), ((), ())), preferred_element_type=F32, **kw)


def _dot_tn(a, b, **kw):
    return lax.dot_general(a, b, (((0,), (0,)), ((), ())), preferred_element_type=F32, **kw)


def _rms(x):
    return x * lax.rsqrt(jnp.mean(x * x, axis=-1, keepdims=True) + RMS_EPS)


def _sigmoid(x):
    return 1.0 / (1.0 + jnp.exp(-x))


def _softplus(x):
    return jnp.maximum(x, 0.0) + jnp.log(1.0 + jnp.exp(-jnp.abs(x)))


def _norm_matmul_kernel(x_ref, g_ref, w_ref, o_ref, xn_ref):
    @pl.when(pl.program_id(1) == 0)
    def _():
        xn_ref[...] = (_rms(x_ref[...]) * g_ref[...]).astype(BF16)

    o_ref[...] = _dot(xn_ref[...], w_ref[...])


def norm_matmul(x, gain, w, *, tm=1024, tn=512, name):
    T, D = x.shape
    N = w.shape[1]
    tm, tn = min(tm, T), min(tn, N)
    assert T % tm == 0 and N % tn == 0
    return pl.pallas_call(
        _norm_matmul_kernel,
        grid=(T // tm, N // tn),
        in_specs=[pl.BlockSpec((tm, D), lambda i, j: (i, 0)),
                  pl.BlockSpec((1, D), lambda i, j: (0, 0)),
                  pl.BlockSpec((D, tn), lambda i, j: (0, j))],
        out_specs=pl.BlockSpec((tm, tn), lambda i, j: (i, j)),
        out_shape=jax.ShapeDtypeStruct((T, N), F32),
        scratch_shapes=[pltpu.VMEM((tm, D), BF16)],
        compiler_params=_params(("parallel", "arbitrary")),
        name=name,
    )(x, gain.reshape(1, D), w)


def _matmul_res_kernel(y_ref, w_ref, h_ref, o_ref):
    o_ref[...] = h_ref[...] + _dot(y_ref[...].astype(BF16), w_ref[...])


def matmul_residual(y, w, h, *, tm=256, tn=2048, name):
    T, K = y.shape
    N = w.shape[1]
    tm, tn = min(tm, T), min(tn, N)
    assert T % tm == 0 and N % tn == 0
    return pl.pallas_call(
        _matmul_res_kernel,
        grid=(T // tm, N // tn),
        in_specs=[pl.BlockSpec((tm, K), lambda i, j: (i, 0)),
                  pl.BlockSpec((K, tn), lambda i, j: (0, j)),
                  pl.BlockSpec((tm, tn), lambda i, j: (i, j))],
        out_specs=pl.BlockSpec((tm, tn), lambda i, j: (i, j)),
        out_shape=jax.ShapeDtypeStruct((T, N), F32),
        compiler_params=_params(("parallel", "parallel")),
        name=name,
    )(y, w, h)


def _tri_masks(bwd):
    row = lax.broadcasted_iota(I32, (CHUNK, CHUNK), 0)
    col = lax.broadcasted_iota(I32, (CHUNK, CHUNK), 1)
    incl = jnp.where(bwd, (col >= row).astype(F32), (col <= row).astype(F32))
    strict = jnp.where(bwd, (col > row).astype(F32), (col < row).astype(F32))
    return incl, strict


def _gla_kernel(q_ref, k_ref, v_ref, r_ref, glo_ref, wg_ref, bg_ref, hn_ref, trif_ref, trib_ref, o_ref,
                of_ref, st_ref, *, nb, blk):
    i = pl.program_id(2)
    nc = blk // CHUNK
    cs = range(nc)

    @pl.when((i == 0) | (i == nb))
    def _():
        st_ref[...] = jnp.zeros_like(st_ref)

    def scan_block(bwd):
        sb = (2 * nb - 1 - i) if bwd else i
        lane = lax.broadcasted_iota(I32, (blk, LANES), 1)
        lo = GLA_RANK if bwd else 0
        gsel = jnp.where((lane >= lo) & (lane < lo + GLA_RANK), glo_ref[...], 0.0)
        bg = bg_ref[...]
        gate = _dot(gsel.astype(BF16), wg_ref[...]) + (bg[1:2] if bwd else bg[0:1])
        la = (jnp.minimum(gate, 0.0) - jnp.log(1.0 + jnp.exp(-jnp.abs(gate)))) * (1.0 / GLA_TAU)
        tri = trib_ref[...] if bwd else trif_ref[...]
        grp = tri.shape[0]
        pieces = _split3_bf16(la)
        cum_blk = jnp.concatenate([sum(_dot(tri, pc[r:r + grp]) for pc in pieces) for r in range(0, blk, grp)],
                                  axis=0)
        row = lax.broadcasted_iota(I32, (CHUNK, CHUNK), 0)
        col = lax.broadcasted_iota(I32, (CHUNK, CHUNK), 1)
        incl = ((col >= row) if bwd else (col <= row)).astype(F32)
        r0 = [(nc - 1 - c if bwd else c) * CHUNK for c in cs]
        rows = [slice(r, r + CHUNK) for r in r0]
        cum = [cum_blk[rw] for rw in rows]
        tot = [cum_blk[(r if bwd else r + CHUNK - 1):(r + 1 if bwd else r + CHUNK)] for r in r0]
        q = [q_ref[rw, :] * (GLA_DK ** -0.5) for rw in rows]
        k = [k_ref[rw, :] for rw in rows]
        v = [v_ref[rw, :].astype(BF16) for rw in rows]
        qd = [(q[c] * jnp.exp(cum[c])).astype(BF16) for c in cs]
        kin = [(k[c] * jnp.exp(-cum[c])).astype(BF16) for c in cs]
        kst = [(k[c] * jnp.exp(tot[c] - cum[c])).astype(BF16) for c in cs]
        s = [(_dot_nt(qd[c], kin[c]) * incl).astype(BF16) for c in cs]
        o = [_dot(s[c], v[c]) for c in cs]
        st = st_ref[...]
        upd = _dot_tn(v[0], kst[0])
        for c in cs:
            nxt = _dot_tn(v[c + 1], kst[c + 1]) if c + 1 < nc else None
            o[c] = o[c] + _dot_nt(qd[c], st.astype(BF16))
            st = st * jnp.exp(tot[c]) + upd
            upd = nxt
        st_ref[...] = st
        gain = hn_ref[...]
        for c in cs:
            grow = pl.ds(pl.multiple_of(sb * blk + r0[c], CHUNK), CHUNK)
            if bwd:
                ot = of_ref[grow, :] + o[c]
                r = r_ref[rows[c], :]
                o_ref[rows[c], :] = (_rms(ot) * gain * (r * _sigmoid(r))).astype(o_ref.dtype)
            else:
                of_ref[grow, :] = o[c]

    @pl.when(i < nb)
    def _():
        scan_block(False)

    @pl.when(i >= nb)
    def _():
        scan_block(True)


def gla_scan(proj, glo, wg, bg, head_norm, *, B, S, blk=512):
    T = B * S
    blk = min(blk, S)
    nb = S // blk
    H = GLA_HEADS
    grp = min(TRI_ROWS, blk)

    def rowblk(b, i):
        return b * nb + jnp.where(i >= nb, 2 * nb - 1 - i, i)

    def outblk(b, i):
        return b * nb + jnp.where(i >= nb, 2 * nb - 1 - i, nb - 1)

    kq = GLA_HEADS * GLA_DK // GLA_DK
    kv = 2 * GLA_HEADS * GLA_DK // GLA_DV
    kr = kv + GLA_HEADS
    return pl.pallas_call(
        functools.partial(_gla_kernel, nb=nb, blk=blk),
        grid=(B, H, 2 * nb),
        in_specs=[pl.BlockSpec((blk, GLA_DK), lambda b, h, i: (rowblk(b, i), h)),
                  pl.BlockSpec((blk, GLA_DK), lambda b, h, i: (rowblk(b, i), kq + h)),
                  pl.BlockSpec((blk, GLA_DV), lambda b, h, i: (rowblk(b, i), kv + h)),
                  pl.BlockSpec((blk, GLA_DV), lambda b, h, i: (outblk(b, i), kr + h)),
                  pl.BlockSpec((blk, LANES), lambda b, h, i: (rowblk(b, i), 0)),
                  pl.BlockSpec((LANES, GLA_DK), lambda b, h, i: (0, h)),
                  pl.BlockSpec((2, GLA_DK), lambda b, h, i: (0, h)),
                  pl.BlockSpec((1, GLA_DV), lambda b, h, i: (0, 0)),
                  pl.BlockSpec((grp, grp), lambda b, h, i: (0, 0)),
                  pl.BlockSpec((grp, grp), lambda b, h, i: (0, 0))],
        out_specs=pl.BlockSpec((blk, GLA_DV), lambda b, h, i: (outblk(b, i), h)),
        out_shape=jax.ShapeDtypeStruct((T, H * GLA_DV), BF16),
        scratch_shapes=[pltpu.VMEM((S, GLA_DV), F32),
                        pltpu.VMEM((GLA_DV, GLA_DK), F32)],
        compiler_params=_params(("parallel", "parallel", "arbitrary")),
        name="gla_scan",
    )(proj, proj, proj, proj, glo, wg, bg, head_norm.reshape(1, GLA_DV),
      _block_tri(grp, False), _block_tri(grp, True))


def _gdn_conv_kernel(x_ref, w_ref, o_ref, xp_ref, *, S, rows):
    c = pl.program_id(1)
    pad = 8
    xp_ref[0:pad, :] = jnp.zeros((pad, LANES), F32)
    xp_ref[pad + S:2 * pad + S, :] = jnp.zeros((pad, LANES), F32)
    xp_ref[pad:pad + S, :] = x_ref[...]
    w = w_ref[...]
    win = rows + 2 * pad
    is_qk = c < 2 * GDN_QK_HEADS
    scale = jnp.where(c < GDN_QK_HEADS, GDN_HD ** -0.5, 1.0)

    def conv_silu(t):
        r0 = pl.multiple_of(t * rows, rows)
        xw = xp_ref[pl.ds(r0, win), :]
        acc = jnp.zeros((rows, LANES), F32)
        for j in range(GDN_CONV):
            sh = (GDN_CONV // 2 - j) % win
            xs = xw if sh == 0 else pltpu.roll(xw, sh, 0)
            acc = acc + xs[pad:pad + rows, :] * w[j:j + 1, :]
        return r0, acc * _sigmoid(acc)

    @pl.when(is_qk)
    def _():
        def body(t, carry):
            r0, y = conv_silu(t)
            o_ref[pl.ds(r0, rows), :] = y * (lax.rsqrt(jnp.sum(y * y, axis=-1, keepdims=True) + RMS_EPS) * scale)
            return carry
        lax.fori_loop(0, S // rows, body, 0, unroll=2)

    @pl.when(jnp.logical_not(is_qk))
    def _():
        def body(t, carry):
            r0, y = conv_silu(t)
            o_ref[pl.ds(r0, rows), :] = y
            return carry
        lax.fori_loop(0, S // rows, body, 0, unroll=2)


def gdn_conv(proj, conv_w, *, B, S):
    T = B * S
    nch = conv_w.shape[1] // LANES
    rows = min(256, S)
    return pl.pallas_call(
        functools.partial(_gdn_conv_kernel, S=S, rows=rows),
        grid=(B, nch),
        in_specs=[pl.BlockSpec((S, LANES), lambda b, c: (b, c)),
                  pl.BlockSpec((GDN_CONV, LANES), lambda b, c: (0, c))],
        out_specs=pl.BlockSpec((S, LANES), lambda b, c: (b, c)),
        out_shape=jax.ShapeDtypeStruct((T, nch * LANES), F32),
        scratch_shapes=[pltpu.VMEM((S + 16, LANES), F32)],
        compiler_params=_params(("parallel", "parallel")),
        name="gdn_conv",
    )(proj, conv_w)


def _mm_bf16(a, b):
    return _dot(a.astype(BF16), b.astype(BF16))


def _unit_tri_inverses(Ls):
    row = lax.broadcasted_iota(I32, (CHUNK, CHUNK), 0)
    col = lax.broadcasted_iota(I32, (CHUNK, CHUNK), 1)
    eye = (row == col).astype(F32)
    ps = [eye - L for L in Ls]
    pws = [_mm_bf16(L, L) for L in Ls]
    n = 2
    while True:
        ps = [p + _mm_bf16(p, pw) for p, pw in zip(ps, pws)]
        n *= 2
        if n >= CHUNK:
            return ps
        pws = [_mm_bf16(pw, pw) for pw in pws]


def _chunk_cumsum(x, bwd):
    n = x.shape[0]
    pos = lax.broadcasted_iota(I32, x.shape, 0) % CHUNK
    s = 1
    while s < CHUNK:
        if bwd:
            x = x + jnp.where(pos < CHUNK - s, pltpu.roll(x, n - s, 0), 0.0)
        else:
            x = x + jnp.where(pos >= s, pltpu.roll(x, s, 0), 0.0)
        s *= 2
    return x


def _split3_bf16(x):
    hi = x.astype(BF16)
    r1 = x - hi.astype(F32)
    mid = r1.astype(BF16)
    lo = (r1 - mid.astype(F32)).astype(BF16)
    return hi, mid, lo


def _gdn_prepare(q_ref, k_ref, v_ref, ab_ref, tri_ref, alog, dtb, *, d, qh, blk):
    bwd = d == 1
    nc = blk // CHUNK
    cs = range(nc)
    hs = range(GDN_REP)
    hc = [(h, c) for h in hs for c in cs]
    row = lax.broadcasted_iota(I32, (CHUNK, CHUNK), 0)
    col = lax.broadcasted_iota(I32, (CHUNK, CHUNK), 1)
    incl = (col >= row) if bwd else (col <= row)
    strict = (col > row) if bwd else (col < row)

    sl = [slice(c * CHUNK, (c + 1) * CHUNK) for c in cs]
    q = [q_ref[s, :] for s in sl]
    k = [k_ref[s, :] for s in sl]
    qbf = [t.astype(BF16) for t in q]
    kbf = [t.astype(BF16) for t in k]
    kk = [_dot_nt(kbf[c], kbf[c]) for c in cs]
    qkr = [_dot_nt(qbf[c], kbf[c]) for c in cs]
    x = ab_ref[...]
    gfull = -jnp.exp(alog) * _softplus(x + dtb)
    bfull = _sigmoid(x)
    lane = lax.broadcasted_iota(I32, (blk, LANES), 1)

    def pick(full, ln):
        return jnp.broadcast_to(jnp.sum(jnp.where(lane == ln, full, 0.0), axis=1, keepdims=True), (blk, LANES))

    lane_g = [d * GDN_V_HEADS + GDN_REP * qh + h for h in hs]
    gb = [pick(gfull, ln) for ln in lane_g]
    bb = [pick(bfull, 2 * GDN_V_HEADS + ln) for ln in lane_g]
    del tri_ref
    gc_blk = [_chunk_cumsum(g, bwd) for g in gb]

    last = [c * CHUNK if bwd else (c + 1) * CHUNK - 1 for c in cs]
    gc = [gc_blk[h][sl[c]] for h, c in hc]
    tot = [gc_blk[h][last[c]:last[c] + 1] for h, c in hc]
    beta = [bb[h][sl[c]] for h, c in hc]
    gamma = [jnp.where(incl, jnp.exp(g[:, :CHUNK] - jnp.transpose(g)[:CHUNK, :]), 0.0) for g in gc]
    tinv =_unit_tri_inverses([jnp.where(strict, kk[c] * beta[j][:, :CHUNK] * gamma[j], 0.0)
                               for j, (h, c) in enumerate(hc)])
    egc = [jnp.exp(g) for g in gc]
    rhs = [jnp.concatenate([v_ref[sl[c], h * GDN_HD:(h + 1) * GDN_HD] * beta[j], k[c] * beta[j] * egc[j]], axis=1)
           for j, (h, c) in enumerate(hc)]
    uw = [_mm_bf16(tinv[j], rhs[j]).astype(BF16) for j in range(len(hc))]
    qk = [(qkr[c] * gamma[j]).astype(BF16) for j, (h, c) in enumerate(hc)]
    kst = [(k[c] * jnp.exp(tot[j] - gc[j])).astype(BF16) for j, (h, c) in enumerate(hc)]
    qd = [q[c] * egc[j] for j, (h, c) in enumerate(hc)]
    return dict(uw=uw, qk=qk, kst=kst, qd=qd, dec=[jnp.exp(t) for t in tot])


def _gdn_chunk_operands(p, j):
    kuw = _dot_tn(p["kst"][j], p["uw"][j])
    quw = _dot(p["qk"][j], p["uw"][j])
    qt = (p["qd"][j] - quw[:, GDN_HD:]).astype(BF16)
    return qt, quw[:, :GDN_HD], kuw[:, GDN_HD:].astype(BF16), kuw[:, :GDN_HD], p["dec"][j]


def _gdn_chunk_step(ops, S):
    qt, qu, kw, ku, dec = ops
    sb16 = S.astype(BF16)
    return _dot(qt, sb16) + qu, S * dec - _dot(kw, sb16) + ku


def _gdn_kernel(qf_ref, kf_ref, vf_ref, abf_ref, qb_ref, kb_ref, vb_ref, abb_ref, trif_ref, trib_ref,
                z_ref, alog_ref, dtb_ref, hn_ref, o_ref, of_ref, ob_ref, st_ref, *, nb, blk, S):
    qh = pl.program_id(1)
    i = pl.program_id(2)
    nc = blk // CHUNK

    @pl.when(i == 0)
    def _():
        st_ref[...] = jnp.zeros_like(st_ref)

    alog = alog_ref[...]
    dtb = dtb_ref[...]
    pf = _gdn_prepare(qf_ref, kf_ref, vf_ref, abf_ref, trif_ref, alog, dtb, d=0, qh=qh, blk=blk)
    pb = _gdn_prepare(qb_ref, kb_ref, vb_ref, abb_ref, trib_ref, alog, dtb, d=1, qh=qh, blk=blk)
    sf = [st_ref[0, h] for h in range(GDN_REP)]
    sb = [st_ref[1, h] for h in range(GDN_REP)]
    rowf = i * blk
    rowb = (nb - 1 - i) * blk
    def operands(c):
        return [(_gdn_chunk_operands(pf, h * nc + c), _gdn_chunk_operands(pb, h * nc + nc - 1 - c))
                for h in range(GDN_REP)]

    nxt = operands(0)
    for c in range(nc):
        cb = nc - 1 - c
        cur = nxt
        if c + 1 < nc:
            nxt = operands(c + 1)
        outs = []
        for h in range(GDN_REP):
            of, sf[h] = _gdn_chunk_step(cur[h][0], sf[h])
            ob, sb[h] = _gdn_chunk_step(cur[h][1], sb[h])
            outs.append((of, ob))
        for h, (of, ob) in enumerate(outs):
            cols = slice(h * GDN_HD, (h + 1) * GDN_HD)
            of_ref[pl.ds(pl.multiple_of(rowf + c * CHUNK, CHUNK), CHUNK), cols] = of
            ob_ref[pl.ds(pl.multiple_of(rowb + cb * CHUNK, CHUNK), CHUNK), cols] = ob
    for h in range(GDN_REP):
        st_ref[0, h] = sf[h]
        st_ref[1, h] = sb[h]

    @pl.when(i == nb - 1)
    def _():
        gain = hn_ref[...]
        rows_e = min(256, S)

        def ebody(t, carry):
            rows = pl.ds(pl.multiple_of(t * rows_e, rows_e), rows_e)
            for h in range(GDN_REP):
                cols = slice(h * GDN_HD, (h + 1) * GDN_HD)
                ot = of_ref[rows, cols] + ob_ref[rows, cols]
                z = z_ref[rows, cols]
                o_ref[rows, cols] = (_rms(ot) * gain * (z * _sigmoid(z))).astype(o_ref.dtype)
            return carry

        lax.fori_loop(0, S // rows_e, ebody, 0)


def _block_tri(n, bwd):
    r = jnp.arange(n, dtype=I32)[:, None]
    c = jnp.arange(n, dtype=I32)[None, :]
    same = (r // CHUNK) == (c // CHUNK)
    return (same & ((c >= r) if bwd else (c <= r))).astype(BF16)


def gdn_scan(qkv, proj, ab, alog_row, dtb_row, head_norm, *, B, S, blk=512):
    T = B * S
    blk = min(blk, S)
    nb = S // blk
    grp = min(TRI_ROWS, blk)
    vw = GDN_REP * GDN_HD
    voff = 2 * GDN_QK_HEADS * GDN_HD // vw
    zoff = (2 * GDN_QK_HEADS + GDN_V_HEADS) * GDN_HD // vw

    def fwd(b, i):
        return b * nb + i

    def bwd(b, i):
        return b * nb + nb - 1 - i

    def dir_specs(rb):
        return [pl.BlockSpec((blk, GDN_HD), lambda b, h, i: (rb(b, i), h)),
                pl.BlockSpec((blk, GDN_HD), lambda b, h, i: (rb(b, i), GDN_QK_HEADS + h)),
                pl.BlockSpec((blk, vw), lambda b, h, i: (rb(b, i), voff + h)),
                pl.BlockSpec((blk, LANES), lambda b, h, i: (rb(b, i), 0))]

    const2 = lambda b, h, i: (0, 0)
    return pl.pallas_call(
        functools.partial(_gdn_kernel, nb=nb, blk=blk, S=S),
        grid=(B, GDN_QK_HEADS, nb),
        in_specs=dir_specs(fwd) + dir_specs(bwd) + [
            pl.BlockSpec((grp, grp), const2),
            pl.BlockSpec((grp, grp), const2),
            pl.BlockSpec((S, vw), lambda b, h, i: (b, zoff + h)),
            pl.BlockSpec((1, LANES), const2),
            pl.BlockSpec((1, LANES), const2),
            pl.BlockSpec((1, GDN_HD), const2)],
        out_specs=pl.BlockSpec((S, vw), lambda b, h, i: (b, h)),
        out_shape=jax.ShapeDtypeStruct((T, GDN_V_HEADS * GDN_HD), BF16),
        scratch_shapes=[pltpu.VMEM((S, vw), F32),
                        pltpu.VMEM((S, vw), F32),
                        pltpu.VMEM((2, GDN_REP, GDN_HD, GDN_HD), F32)],
        compiler_params=_params(("parallel", "parallel", "arbitrary")),
        name="gdn_scan",
    )(qkv, qkv, qkv, ab, qkv, qkv, qkv, ab, _block_tri(grp, False), _block_tri(grp, True),
      proj, alog_row, dtb_row, head_norm.reshape(1, GDN_HD))


def _t5_bucket(rel):
    half = REL_BUCKETS // 2
    max_exact = half // 2
    n = jnp.abs(rel)
    log_ratio = jnp.log(jnp.maximum(n, 1).astype(F32) / max_exact) / math.log(REL_MAX_DIST / max_exact)
    large = jnp.minimum(max_exact + (log_ratio * (half - max_exact)).astype(I32), half - 1)
    return jnp.where(rel > 0, half, 0) + jnp.where(n < max_exact, n, large)


def _rel_bias_heads(table, rel):
    return jnp.moveaxis(table[_t5_bucket(rel)].astype(F32), -1, 0)


def _toeplitz(w, n, m, off):
    lw = w.shape[-1]
    assert lw == n + m - 1 and m <= lw - 1
    w_rot = jnp.roll(w, -off, axis=-1)
    flat = jnp.tile(w_rot, (1,) * (w.ndim - 1) + (n,))[..., :n * (lw - 1)]
    return flat.reshape(w.shape[:-1] + (n, lw - 1))[..., :m]


def _half_rms(x):
    lane = lax.broadcasted_iota(I32, x.shape, 1)
    lo = lane < DIFF_DQK
    xx = x * x
    ms_lo = jnp.sum(jnp.where(lo, xx, 0.0), axis=1, keepdims=True)
    ms_hi = jnp.sum(jnp.where(lo, 0.0, xx), axis=1, keepdims=True)
    ms = jnp.where(lo, ms_lo, ms_hi) * (1.0 / DIFF_DQK)
    return x * lax.rsqrt(ms + RMS_EPS)


LOG2E = math.log2(math.e)


def _diff_kernel(q_ref, k_ref, v_ref, qn_ref, kn_ref, lam_ref, sub_ref, bias_ref, o_ref,
                 kb_ref, vb_ref, m_ref, l_ref, acc_ref, s_ref, *, nk, lambda_init):
    i = pl.program_id(2)
    qb = DIFF_QB

    @pl.when(i == 0)
    def _():
        def kbody(t, carry):
            rows = pl.ds(pl.multiple_of(t * qb, qb), qb)
            kb_ref[rows, :] = (_half_rms(k_ref[rows, :]) * kn_ref[...]).astype(BF16)
            vb_ref[rows, :] = v_ref[rows, :].astype(BF16)
            return carry
        lax.fori_loop(0, nk, kbody, 0, unroll=2)

    q = _half_rms(q_ref[...]) * qn_ref[...] * (DIFF_DQK ** -0.5 * LOG2E)
    lane = lax.broadcasted_iota(I32, (qb, LANES), 1)
    qs = (jnp.where(lane < DIFF_DQK, q, 0.0).astype(BF16), jnp.where(lane >= DIFF_DQK, q, 0.0).astype(BF16))

    m_ref[...] = jnp.full(m_ref.shape, -jnp.inf, F32)
    l_ref[...] = jnp.zeros(l_ref.shape, F32)
    acc_ref[...] = jnp.zeros(acc_ref.shape, F32)

    kw = DIFF_KB // qb
    maps = range(2)

    nsteps = nk * qb // DIFF_KB

    def key_rows(t):
        return pl.ds(t * DIFF_KB, DIFF_KB)

    def scores(t, slot):
        kc = kb_ref[key_rows(t), :]
        bias = jnp.concatenate([bias_ref[0, jnp.clip(kw * t + u - i, -2, 2) + 2] for u in range(kw)], axis=1)
        for mi in maps:
            s_ref[slot, mi] = _dot_nt(qs[mi], kc) + bias

    def chunk(t, prefetch):
        slot = t % 2
        s = [s_ref[slot, mi] for mi in maps]
        if prefetch:
            scores(t + 1, 1 - slot)
        vc = vb_ref[key_rows(t), :]
        m_cur = [jnp.max(s[mi], axis=1, keepdims=True) for mi in maps]
        m_prev = [m_ref[mi] for mi in maps]
        m_new = [jnp.maximum(m_prev[mi], m_cur[mi]) for mi in maps]
        alpha = [jnp.exp2(m_prev[mi] - m_new[mi]) for mi in maps]
        p = [jnp.exp2(s[mi] - jnp.concatenate([m_new[mi]] * (DIFF_KB // LANES), axis=1)) for mi in maps]
        psum = [jnp.sum(p[mi], axis=1, keepdims=True) for mi in maps]
        pv = [_dot(p[mi].astype(BF16), vc) for mi in maps]
        for mi in maps:
            l_ref[mi] = alpha[mi] * l_ref[mi] + psum[mi]
            acc_ref[mi] = alpha[mi] * acc_ref[mi] + pv[mi]
            m_ref[mi] = m_new[mi]

    scores(0, 0)
    for t in range(nsteps - 1):
        chunk(t, True)
    chunk(nsteps - 1, False)

    lam = lam_ref[...]
    lam_full = (jnp.exp(jnp.sum(lam[0:1] * lam[1:2], axis=-1, keepdims=True))
                - jnp.exp(jnp.sum(lam[2:3] * lam[3:4], axis=-1, keepdims=True)) + lambda_init)
    o = acc_ref[0] / l_ref[0] - lam_full * (acc_ref[1] / l_ref[1])
    o_ref[...] = (_rms(o) * sub_ref[...] * (1.0 - lambda_init)).astype(o_ref.dtype)


def diff_attention(proj, q_norm, k_norm, lam, subln, bias_tiles, *, B, S, layer_idx):
    T = B * S
    qb = DIFF_QB
    nq = S // qb
    H = DIFF_HEADS
    lambda_init = 0.8 - 0.6 * math.exp(-0.3 * layer_idx)
    qn2 = jnp.concatenate([q_norm, q_norm]).reshape(1, LANES)
    kn2 = jnp.concatenate([k_norm, k_norm]).reshape(1, LANES)
    return pl.pallas_call(
        functools.partial(_diff_kernel, nk=nq, lambda_init=lambda_init),
        grid=(B, H, nq),
        in_specs=[pl.BlockSpec((qb, LANES), lambda b, h, i: (b * nq + i, h)),
                  pl.BlockSpec((S, LANES), lambda b, h, i: (b, H + h)),
                  pl.BlockSpec((S, LANES), lambda b, h, i: (b, 2 * H + h)),
                  pl.BlockSpec((1, LANES), lambda b, h, i: (0, 0)),
                  pl.BlockSpec((1, LANES), lambda b, h, i: (0, 0)),
                  pl.BlockSpec((4, DIFF_DQK), lambda b, h, i: (0, 0)),
                  pl.BlockSpec((1, DIFF_DV), lambda b, h, i: (0, 0)),
                  pl.BlockSpec((1, 5, qb, qb), lambda b, h, i: (h, 0, 0, 0))],
        out_specs=pl.BlockSpec((qb, DIFF_DV), lambda b, h, i: (b * nq + i, h)),
        out_shape=jax.ShapeDtypeStruct((T, H * DIFF_DV), BF16),
        scratch_shapes=[pltpu.VMEM((S, LANES), BF16),
                        pltpu.VMEM((S, DIFF_DV), BF16),
                        pltpu.VMEM((2, qb, LANES), F32),
                        pltpu.VMEM((2, qb, LANES), F32),
                        pltpu.VMEM((2, qb, DIFF_DV), F32),
                        pltpu.VMEM((2, 2, qb, DIFF_KB), F32)],
        compiler_params=_params(("parallel", "parallel", "arbitrary")),
        name="diff_attn",
    )(proj, proj, proj, qn2, kn2, lam, subln.reshape(1, DIFF_DV), bias_tiles)


def diff_bias_tiles(table):
    qb = DIFF_QB
    span = 3 * qb - 1
    vec = _rel_bias_heads(table, jnp.arange(-span, span + 1, dtype=I32)) * LOG2E
    w = jnp.stack([vec[:, (d + 2) * qb:(d + 2) * qb + 2 * qb - 1] for d in range(-2, 3)], axis=1)
    return _toeplitz(w, qb, qb, qb - 1)


def _swa_kernel(q_ref, *refs, nq, S):
    nkb = SWA_QPS + 2
    k_refs, v_refs = refs[:nkb], refs[nkb:2 * nkb]
    qn_ref, kn_ref, sink_ref, bias_ref, o_ref = refs[2 * nkb:]
    i = pl.program_id(2)
    qb = SWA_QB
    span = 3 * qb
    kcat = jnp.concatenate([r[...] for r in k_refs], axis=0)
    kcat = (_rms(kcat) * kn_ref[...]).astype(BF16)
    vcat = jnp.concatenate([r[...] for r in v_refs], axis=0).astype(BF16)
    row = lax.broadcasted_iota(I32, (qb, span), 0)
    col = lax.broadcasted_iota(I32, (qb, span), 1)
    rel = col - SWA_W - row
    in_window = jnp.abs(rel) <= SWA_W
    sink_all = sink_ref[0]
    pairs = [(a, g) for a in range(SWA_QPS) for g in range(SWA_GROUP)]
    valid = []
    for a in range(SWA_QPS):
        key_pos = (i * SWA_QPS + a) * qb - SWA_W + col
        valid.append(in_window & (key_pos >= 0) & (key_pos < S))
    q = [q_ref[a * qb:(a + 1) * qb, g * SWA_HD:(g + 1) * SWA_HD] for a, g in pairs]
    q = [(_rms(t) * qn_ref[...] * (SWA_HD ** -0.5)).astype(BF16) for t in q]
    s = [_dot_nt(q[j], kcat[a * qb:a * qb + span]) + bias_ref[g] for j, (a, g) in enumerate(pairs)]
    s = [jnp.where(valid[a], s[j], -jnp.inf) for j, (a, g) in enumerate(pairs)]
    sink = [sink_all[g:g + 1, 0:1] for a, g in pairs]
    js = range(len(pairs))
    m = [jnp.maximum(jnp.max(s[j], axis=-1, keepdims=True), sink[j]) for j in js]
    p = [jnp.exp(s[j] - m[j]) for j in js]
    den = [jnp.sum(p[j], axis=-1, keepdims=True) + jnp.exp(sink[j] - m[j]) for j in js]
    o = [_dot(p[j].astype(BF16), vcat[a * qb:a * qb + span]) / den[j] for j, (a, g) in enumerate(pairs)]
    for j, (a, g) in enumerate(pairs):
        o_ref[a * qb:(a + 1) * qb, g * SWA_HD:(g + 1) * SWA_HD] = o[j].astype(o_ref.dtype)


def swa_attention(proj, q_norm, k_norm, sink, bias, *, B, S):
    T = B * S
    qb = SWA_QB
    nq = S // qb
    koff = SWA_HEADS
    voff = SWA_HEADS + SWA_KV
    gw = SWA_GROUP * SWA_HD

    qps = SWA_QPS
    assert nq % qps == 0
    ns = nq // qps

    def kvspec(off, d):
        return pl.BlockSpec((qb, SWA_HD),
                            lambda b, kv, i: (b * nq + jnp.clip(i * qps + d, 0, nq - 1), off + kv))

    kv_specs = [kvspec(off, d) for off in (koff, voff) for d in range(-1, qps + 1)]
    sink_b = jnp.broadcast_to(sink.astype(F32).reshape(SWA_KV, SWA_GROUP, 1), (SWA_KV, SWA_GROUP, LANES))
    return pl.pallas_call(
        functools.partial(_swa_kernel, nq=nq, S=S),
        grid=(B, SWA_KV, ns),
        in_specs=[pl.BlockSpec((qps * qb, gw), lambda b, kv, i: (b * ns + i, kv))] + kv_specs + [
                  pl.BlockSpec((1, SWA_HD), lambda b, kv, i: (0, 0)),
                  pl.BlockSpec((1, SWA_HD), lambda b, kv, i: (0, 0)),
                  pl.BlockSpec((1, SWA_GROUP, LANES), lambda b, kv, i: (kv, 0, 0)),
                  pl.BlockSpec((SWA_GROUP, qb, 3 * qb), lambda b, kv, i: (kv, 0, 0))],
        out_specs=pl.BlockSpec((qps * qb, gw), lambda b, kv, i: (b * ns + i, kv)),
        out_shape=jax.ShapeDtypeStruct((T, SWA_HEADS * SWA_HD), BF16),
        compiler_params=_params(("parallel", "parallel", "parallel")),
        name="swa_attn",
    )(*([proj] * (1 + 2 * (qps + 2))), q_norm.reshape(1, SWA_HD), k_norm.reshape(1, SWA_HD), sink_b, bias)


def swa_bias(table):
    qb = SWA_QB
    span = 3 * qb
    vec = _rel_bias_heads(table, jnp.arange(-(qb - 1) - SWA_W, span - SWA_W, dtype=I32))
    return _toeplitz(vec, qb, span, qb - 1)


def _router_kernel(h_ref, g_ref, wr_ref, hx_ref, at_ref, *, D):
    hn = _rms(h_ref[...]) * g_ref[...]
    x_hi = hn.astype(BF16)
    x_lo = (hn - x_hi.astype(F32)).astype(BF16)
    w = wr_ref[...]
    w_hi = w.astype(BF16)
    w_lo = (w - w_hi.astype(F32)).astype(BF16)
    logits = (_dot(x_hi, w_hi) + _dot(x_hi, w_lo)) + (_dot(x_lo, w_hi) + _dot(x_lo, w_lo))
    lane = lax.broadcasted_iota(I32, logits.shape, 1)
    logits = jnp.where(lane < N_EXPERTS, logits, -jnp.inf)
    m = jnp.max(logits, axis=-1, keepdims=True)
    e = jnp.exp(logits - m)
    aff = e / jnp.sum(e, axis=-1, keepdims=True)
    hx_ref[:, :D] = hn
    hx_ref[:, D:] = aff
    at_ref[0] = jnp.transpose(aff)


def moe_router(h, gain, router, *, B, S, tm=512):
    T, D = h.shape
    tm = min(tm, S)
    ns = S // tm
    wr = jnp.pad(router.astype(F32), ((0, 0), (0, LANES - N_EXPERTS)))
    return pl.pallas_call(
        functools.partial(_router_kernel, D=D),
        grid=(B, ns),
        in_specs=[pl.BlockSpec((tm, D), lambda b, s: (b * ns + s, 0)),
                  pl.BlockSpec((1, D), lambda b, s: (0, 0)),
                  pl.BlockSpec((D, LANES), lambda b, s: (0, 0))],
        out_specs=[pl.BlockSpec((tm, D + LANES), lambda b, s: (b * ns + s, 0)),
                   pl.BlockSpec((1, LANES, tm), lambda b, s: (b, 0, s))],
        out_shape=[jax.ShapeDtypeStruct((T, D + LANES), F32),
                   jax.ShapeDtypeStruct((B, LANES, S), F32)],
        compiler_params=_params(("parallel", "parallel")),
        name="moe_router",
    )(h, gain.reshape(1, D), wr)


def _topk_kernel(aff_ref, idx_ref, pos_ref, *, S, cap):
    E = N_EXPERTS
    v = aff_ref[0]
    bits = pltpu.bitcast(v, I32)

    def search(_, carry):
        lo, hi = carry
        mid = lo + ((hi - lo) >> 1)
        cnt = jnp.sum((bits >= mid).astype(F32), axis=1, keepdims=True)
        ok = cnt >= cap
        return jnp.where(ok, mid, lo), jnp.where(ok, hi, mid)

    lo0 = jnp.zeros((E, 1), I32)
    hi0 = jnp.full((E, 1), 0x7F800001, I32)
    thr, _ = lax.fori_loop(0, 32, search, (lo0, hi0))
    gt = bits > thr
    eq = bits == thr
    need = cap - jnp.sum(gt.astype(F32), axis=1, keepdims=True)

    r = lax.broadcasted_iota(I32, (LANES, LANES), 0)
    c = lax.broadcasted_iota(I32, (LANES, LANES), 1)
    upper = (r < c).astype(BF16)
    run_e = jnp.zeros((E, 1), F32)
    run_s = jnp.zeros((E, 1), F32)
    sub = 512 // LANES
    for t in range(S // LANES):
        sl = slice(t * LANES, (t + 1) * LANES)
        eq_t = eq[:, sl].astype(F32)
        pe = _dot(eq_t.astype(BF16), upper) + run_e
        sel_t = jnp.where(gt[:, sl], 1.0, jnp.where(pe < need, eq_t, 0.0))
        ps = _dot(sel_t.astype(BF16), upper) + run_s
        pos_ref[t // sub, :, (t % sub) * LANES:(t % sub + 1) * LANES] = jnp.where(sel_t > 0, ps, -1.0)
        run_e = run_e + jnp.sum(eq_t, axis=1, keepdims=True)
        run_s = run_s + jnp.sum(sel_t, axis=1, keepdims=True)

    pi = lax.broadcasted_iota(I32, (cap, 512), 0).astype(F32)
    lane = lax.broadcasted_iota(I32, (8, 512), 1)
    rowv = lax.broadcasted_iota(I32, (8, 512), 0)

    def per_expert(e, carry):
        parts = []
        for t in range(S // 512):
            pos = pos_ref[t, pl.ds(e, 1), :]
            onehot = (pi == pos).astype(BF16)
            tok = t * 512 + lane
            vals = jnp.where(rowv == 0, tok >> 6, jnp.where(rowv == 1, tok & 63, 0)).astype(F32).astype(BF16)
            parts.append(_dot_nt(vals, onehot))
        acc = sum(parts)
        idx_ref[0, pl.ds(e, 1), :] = (acc[0:1] * 64.0 + acc[1:2]).astype(I32)
        return carry

    lax.fori_loop(0, E, per_expert, 0)


def moe_topk(aff_t, *, B, S):
    cap = EC_CAPACITY_FACTOR * S // N_EXPERTS
    return pl.pallas_call(
        functools.partial(_topk_kernel, S=S, cap=cap),
        grid=(B,),
        in_specs=[pl.BlockSpec((1, N_EXPERTS, S), lambda b: (b, 0, 0))],
        out_specs=pl.BlockSpec((1, N_EXPERTS, cap), lambda b: (b, 0, 0)),
        out_shape=jax.ShapeDtypeStruct((B, N_EXPERTS, cap), I32),
        scratch_shapes=[pltpu.VMEM((S // 512, N_EXPERTS, 512), F32)],
        compiler_params=_params(("parallel",)),
        name="moe_topk",
    )(aff_t)


def _ffn_kernel(idx0_ref, idx1_ref, idxn_ref, hx_hbm, h_in, w1_ref, w3_ref, w2_ref, h_out,
                xbuf, acc, sem_x, sem_h, sem_s, *, S, D, cap, nj):
    del h_in
    e = pl.program_id(0)
    j = pl.program_id(1)
    first = (e == 0) & (j == 0)
    last = (e == pl.num_programs(0) - 1) & (j == nj - 1)
    base0 = (2 * j) * S
    base1 = base0 + S
    jn = jnp.where(last, j, (j + 1) % nj)
    basen = (2 * jn) * S

    def gather_x(idx_ref, base, slot, r):
        return pltpu.make_async_copy(hx_hbm.at[pl.ds(base + idx_ref[0, 0, r], 1), :],
                                     xbuf.at[slot, pl.ds(r, 1), :], sem_x.at[slot])

    def gather_h(idx_ref, base, slot, r):
        return pltpu.make_async_copy(h_out.at[pl.ds(base + idx_ref[0, 0, r], 1), :],
                                     acc.at[slot, pl.ds(r, 1), :], sem_h.at[slot])

    def scatter_h(idx_ref, base, slot, r):
        return pltpu.make_async_copy(acc.at[slot, pl.ds(r, 1), :],
                                     h_out.at[pl.ds(base + idx_ref[0, 0, r], 1), :], sem_s.at[slot])

    def wait_x(slot):
        pltpu.make_async_copy(hx_hbm.at[pl.ds(0, cap), :], xbuf.at[slot], sem_x.at[slot]).wait()

    def wait_h(slot):
        pltpu.make_async_copy(h_out.at[pl.ds(0, cap), :], acc.at[slot], sem_h.at[slot]).wait()

    def wait_s(slot):
        pltpu.make_async_copy(acc.at[slot], h_out.at[pl.ds(0, cap), :], sem_s.at[slot]).wait()

    def swiglu(slot, starts):
        F = w1_ref.shape[-1]
        ns = FFN_SLABS
        per = -(-len(starts) // (3 * ns))
        pending = list(starts)

        def issue_some():
            for thunk in pending[:per]:
                thunk()
            del pending[:per]

        x = xbuf[slot, :, :D].astype(BF16)
        fs = F // ns
        a, g = [], []
        for s in range(ns):
            issue_some()
            a.append(_dot(x, w1_ref[0, :, s * fs:(s + 1) * fs]))
        for s in range(ns):
            issue_some()
            g.append(_dot(x, w3_ref[0, :, s * fs:(s + 1) * fs]))
        hm = jnp.concatenate([(a[s] * _sigmoid(a[s]) * g[s]).astype(BF16) for s in range(ns)], axis=1)
        ds_ = D // ns
        y = []
        for s in range(ns):
            issue_some()
            y.append(_dot(hm, w2_ref[0, :, s * ds_:(s + 1) * ds_]))
        for thunk in pending:
            thunk()
        return jnp.concatenate(y, axis=1)

    def accumulate(slot, y):
        aff = xbuf[slot, :, D:]
        lane = lax.broadcasted_iota(I32, aff.shape, 1)
        gate = jnp.sum(jnp.where(lane == e, aff, 0.0), axis=1, keepdims=True)
        acc[slot] = acc[slot] + y * gate

    @pl.when(first)
    def _():
        def body(r, carry):
            gather_x(idx0_ref, base0, 0, r).start()
            return carry
        lax.fori_loop(0, cap, body, 0, unroll=8)

    def start(copy_fn, *args):
        return lambda: copy_fn(*args).start()

    wait_x(0)
    starts = []
    for r in range(cap):
        starts.append(start(gather_h, idx0_ref, base0, 0, r))
        starts.append(start(gather_x, idx1_ref, base1, 1, r))
    y = swiglu(0, starts)
    wait_h(0)
    accumulate(0, y)
    wait_x(1)
    starts = []
    for r in range(cap):
        starts.append(start(gather_h, idx1_ref, base1, 1, r))
        starts.append(start(scatter_h, idx0_ref, base0, 0, r))
        starts.append(start(gather_x, idxn_ref, basen, 0, r))
    y = swiglu(1, starts)
    wait_h(1)
    accumulate(1, y)

    def body_out(r, carry):
        scatter_h(idx1_ref, base1, 1, r).start()
        return carry

    lax.fori_loop(0, cap, body_out, 0, unroll=8)
    wait_s(0)
    wait_s(1)

    @pl.when(last)
    def _():
        wait_x(0)


def moe_ffn(idx, hx, h, w1, w3, w2, *, B, S):
    T, D = h.shape
    E = N_EXPERTS
    cap = idx.shape[-1]
    F = w1.shape[-1]
    assert B % 2 == 0
    nj = B // 2
    idx3 = idx.reshape(B * E, 1, cap)

    def idx_spec(fn):
        return pl.BlockSpec((1, 1, cap), fn, memory_space=pltpu.SMEM)

    def nxt(e, j):
        is_last = (e == E - 1) & (j == nj - 1)
        en = jnp.where(is_last, e, e + (j + 1) // nj)
        jn = jnp.where(is_last, j, (j + 1) % nj)
        return (2 * jn * E + en, 0, 0)

    return pl.pallas_call(
        functools.partial(_ffn_kernel, S=S, D=D, cap=cap, nj=nj),
        grid=(E, nj),
        in_specs=[idx_spec(lambda e, j: (2 * j * E + e, 0, 0)),
                  idx_spec(lambda e, j: ((2 * j + 1) * E + e, 0, 0)),
                  idx_spec(nxt),
                  pl.BlockSpec(memory_space=pl.ANY),
                  pl.BlockSpec(memory_space=pl.ANY),
                  pl.BlockSpec((1, D, F), lambda e, j: (e, 0, 0)),
                  pl.BlockSpec((1, D, F), lambda e, j: (e, 0, 0)),
                  pl.BlockSpec((1, F, D), lambda e, j: (e, 0, 0))],
        out_specs=pl.BlockSpec(memory_space=pl.ANY),
        out_shape=jax.ShapeDtypeStruct((T, D), F32),
        scratch_shapes=[pltpu.VMEM((2, cap, D + LANES), F32),
                        pltpu.VMEM((2, cap, D), F32),
                        pltpu.SemaphoreType.DMA((2,)),
                        pltpu.SemaphoreType.DMA((2,)),
                        pltpu.SemaphoreType.DMA((2,))],
        input_output_aliases={4: 0},
        compiler_params=_params(("arbitrary", "arbitrary")),
        name="moe_ffn",
    )(idx3, idx3, idx3, hx, h, w1, w3, w2)


def ec_moe(h, gain, router, w1, w3, w2, *, B, S):
    hx, aff_t = moe_router(h, gain, router, B=B, S=S)
    idx = moe_topk(aff_t, B=B, S=S)
    return moe_ffn(idx, hx, h, w1.astype(BF16), w3.astype(BF16), w2.astype(BF16), B=B, S=S)


def _pad_cols(w, n):
    return jnp.pad(w, ((0, 0), (0, n - w.shape[1])))


def gla_layer(h, norm_gain, w_in, w_gate_up, b_gate, head_norm, w_out, *, B, S):
    nmain = 2 * GLA_HEADS * GLA_DK + 2 * GLA_HEADS * GLA_DV
    proj = norm_matmul(h, norm_gain, w_in[:, :nmain].astype(BF16), name="gla_in")
    glo = norm_matmul(h, norm_gain, _pad_cols(w_in[:, nmain:], LANES).astype(BF16), tn=LANES, name="gla_in_gate")
    wg = jnp.pad(w_gate_up.reshape(2 * GLA_RANK, -1), ((0, LANES - 2 * GLA_RANK), (0, 0))).astype(BF16)
    y = gla_scan(proj, glo, wg, b_gate.astype(F32), head_norm, B=B, S=S)
    return matmul_residual(y, w_out.astype(BF16), h, name="gla_out")


def gdn_layer(h, norm_gain, w_in, conv_w, a_log, dt_bias, head_norm, w_out, *, B, S):
    nconv = conv_w.shape[1]
    nmain = nconv + GDN_V_HEADS * GDN_HD
    proj = norm_matmul(h, norm_gain, w_in[:, :nmain].astype(BF16), name="gdn_in")
    ab = norm_matmul(h, norm_gain, w_in[:, nmain:].astype(BF16), tn=LANES, name="gdn_in_gate")
    qkv = gdn_conv(proj, conv_w, B=B, S=S)
    zeros = jnp.zeros((2 * GDN_V_HEADS,), F32)
    alog_row = jnp.concatenate([a_log.astype(F32).reshape(-1), zeros]).reshape(1, LANES)
    dtb_row = jnp.concatenate([dt_bias.astype(F32).reshape(-1), zeros]).reshape(1, LANES)
    y = gdn_scan(qkv, proj, ab, alog_row, dtb_row, head_norm, B=B, S=S)
    return matmul_residual(y, w_out.astype(BF16), h, name="gdn_out")


def diff_layer(h, norm_gain, w_in, q_norm, k_norm, lam, subln, w_out, bias_tiles, layer_idx, *, B, S):
    proj = norm_matmul(h, norm_gain, w_in.astype(BF16), name="diff_in")
    y = diff_attention(proj, q_norm, k_norm, lam, subln, bias_tiles, B=B, S=S, layer_idx=layer_idx)
    return matmul_residual(y, w_out.astype(BF16), h, name="diff_out")


def swa_layer(h, norm_gain, w_in, q_norm, k_norm, sink, w_out, bias, *, B, S):
    proj = norm_matmul(h, norm_gain, w_in.astype(BF16), name="swa_in")
    y = swa_attention(proj, q_norm, k_norm, sink, bias, B=B, S=S)
    return matmul_residual(y, w_out.astype(BF16), h, name="swa_out")


def kernel(x, rel_bias, norm_mix, norm_ffn, gla_w_in, gla_w_gate_up, gla_b_gate, gla_head_norm, gla_w_out, gdn_w_in, gdn_conv, gdn_a_log, gdn_dt_bias, gdn_head_norm, gdn_w_out, diff_w_in, diff_q_norm, diff_k_norm, diff_lambda, diff_subln, diff_w_out, swa_w_in, swa_q_norm, swa_k_norm, swa_sink, swa_w_out, moe_router, moe_w1, moe_w3, moe_w2):
    B, S, D = x.shape
    depth = norm_mix.shape[0]
    h = x.reshape(B * S, D)
    for i in range(depth):
        m, j = i % 4, i // 4
        if m == 0:
            h = gla_layer(h, norm_mix[i], gla_w_in[j], gla_w_gate_up[j], gla_b_gate[j], gla_head_norm[j],
                          gla_w_out[j], B=B, S=S)
        elif m == 1:
            h = gdn_layer(h, norm_mix[i], gdn_w_in[j], gdn_conv[j], gdn_a_log[j], gdn_dt_bias[j],
                          gdn_head_norm[j], gdn_w_out[j], B=B, S=S)
        elif m == 2:
            h = diff_layer(h, norm_mix[i], diff_w_in[j], diff_q_norm[j], diff_k_norm[j], diff_lambda[j],
                           diff_subln[j], diff_w_out[j], diff_bias_tiles(rel_bias), i, B=B, S=S)
        else:
            h = swa_layer(h, norm_mix[i], swa_w_in[j], swa_q_norm[j], swa_k_norm[j], swa_sink[j],
                          swa_w_out[j], swa_bias(rel_bias), B=B, S=S)
        h = ec_moe(h, norm_ffn[i], moe_router[i], moe_w1[i], moe_w3[i], moe_w2[i], B=B, S=S)
    return h.reshape(B, S, D)
```

```python
import functools
import math

import jax
import jax.numpy as jnp
from jax import lax
from jax.experimental import pallas as pl
from jax.experimental.pallas import tpu as pltpu

F32 = jnp.float32
BF16 = jnp.bfloat16
I32 = jnp.int32

RMS_EPS = 1e-6
VMEM_LIMIT_BYTES = 56 * 1024 * 1024
LANES = 128

REL_BUCKETS = 32
REL_MAX_DIST = 128
CHUNK = 64
GLA_HEADS, GLA_DK, GLA_DV, GLA_RANK, GLA_TAU = 4, 256, 512, 16, 16.0
GDN_QK_HEADS, GDN_V_HEADS, GDN_HD, GDN_CONV = 16, 32, 128, 5
GDN_REP = GDN_V_HEADS // GDN_QK_HEADS
DIFF_HEADS, DIFF_DQK, DIFF_DV, DIFF_QB = 16, 64, 128, 256
DIFF_KB = 512
SWA_HEADS, SWA_KV, SWA_GROUP, SWA_HD, SWA_W, SWA_QB = 16, 4, 4, 128, 128, 128
SWA_QPS = 4
N_EXPERTS = 16
EC_CAPACITY_FACTOR = 2
FFN_SLABS = 4


def _params(sem):
    return pltpu.CompilerParams(dimension_semantics=sem, vmem_limit_bytes=VMEM_LIMIT_BYTES)


def _dot(a, b, **kw):
    return jnp.dot(a, b, preferred_element_type=F32, **kw)


def _dot_nt(a, b, **kw):
    return lax.dot_general(a, b, (((1,), (1,)), ((), ())), preferred_element_type=F32, **kw)


def _dot_tn(a, b, **kw):
    return lax.dot_general(a, b, (((0,), (0,)), ((), ())), preferred_element_type=F32, **kw)


def _rms(x):
    return x * lax.rsqrt(jnp.mean(x * x, axis=-1, keepdims=True) + RMS_EPS)


def _sigmoid(x):
    return 1.0 / (1.0 + jnp.exp(-x))


def _softplus(x):
    return jnp.maximum(x, 0.0) + jnp.log(1.0 + jnp.exp(-jnp.abs(x)))


def _norm_matmul_kernel(x_ref, g_ref, w_ref, o_ref, xn_ref):
    @pl.when(pl.program_id(1) == 0)
    def _():
        xn_ref[...] = (_rms(x_ref[...]) * g_ref[...]).astype(BF16)

    o_ref[...] = _dot(xn_ref[...], w_ref[...])


def norm_matmul(x, gain, w, *, tm=1024, tn=512, name):
    T, D = x.shape
    N = w.shape[1]
    tm, tn = min(tm, T), min(tn, N)
    assert T % tm == 0 and N % tn == 0
    return pl.pallas_call(
        _norm_matmul_kernel,
        grid=(T // tm, N // tn),
        in_specs=[pl.BlockSpec((tm, D), lambda i, j: (i, 0)),
                  pl.BlockSpec((1, D), lambda i, j: (0, 0)),
                  pl.BlockSpec((D, tn), lambda i, j: (0, j))],
        out_specs=pl.BlockSpec((tm, tn), lambda i, j: (i, j)),
        out_shape=jax.ShapeDtypeStruct((T, N), F32),
        scratch_shapes=[pltpu.VMEM((tm, D), BF16)],
        compiler_params=_params(("parallel", "arbitrary")),
        name=name,
    )(x, gain.reshape(1, D), w)


def _matmul_res_kernel(y_ref, w_ref, h_ref, o_ref):
    o_ref[...] = h_ref[...] + _dot(y_ref[...].astype(BF16), w_ref[...])


def matmul_residual(y, w, h, *, tm=256, tn=2048, name):
    T, K = y.shape
    N = w.shape[1]
    tm, tn = min(tm, T), min(tn, N)
    assert T % tm == 0 and N % tn == 0
    return pl.pallas_call(
        _matmul_res_kernel,
        grid=(T // tm, N // tn),
        in_specs=[pl.BlockSpec((tm, K), lambda i, j: (i, 0)),
                  pl.BlockSpec((K, tn), lambda i, j: (0, j)),
                  pl.BlockSpec((tm, tn), lambda i, j: (i, j))],
        out_specs=pl.BlockSpec((tm, tn), lambda i, j: (i, j)),
        out_shape=jax.ShapeDtypeStruct((T, N), F32),
        compiler_params=_params(("parallel", "parallel")),
        name=name,
    )(y, w, h)


def _gla_kernel(q_ref, k_ref, v_ref, r_ref, glo_ref, wg_ref, bg_ref, hn_ref, o_ref,
                of_ref, st_ref, *, nb, blk):
    i = pl.program_id(2)
    nc = blk // CHUNK
    cs = range(nc)

    @pl.when((i == 0) | (i == nb))
    def _():
        st_ref[...] = jnp.zeros_like(st_ref)

    def scan_block(bwd):
        sb = (2 * nb - 1 - i) if bwd else i
        lane = lax.broadcasted_iota(I32, (blk, LANES), 1)
        lo = GLA_RANK if bwd else 0
        gsel = jnp.where((lane >= lo) & (lane < lo + GLA_RANK), glo_ref[...], 0.0)
        bg = bg_ref[...]
        gate = _dot(gsel.astype(BF16), wg_ref[...]) + (bg[1:2] if bwd else bg[0:1])
        la = (jnp.minimum(gate, 0.0) - jnp.log(1.0 + jnp.exp(-jnp.abs(gate)))) * (1.0 / GLA_TAU)
        cum_blk = _chunk_cumsum(la, bwd)
        row = lax.broadcasted_iota(I32, (CHUNK, CHUNK), 0)
        col = lax.broadcasted_iota(I32, (CHUNK, CHUNK), 1)
        incl = ((col >= row) if bwd else (col <= row)).astype(F32)
        r0 = [(nc - 1 - c if bwd else c) * CHUNK for c in cs]
        rows = [slice(r, r + CHUNK) for r in r0]
        cum = [cum_blk[rw] for rw in rows]
        tot = [cum_blk[(r if bwd else r + CHUNK - 1):(r + 1 if bwd else r + CHUNK)] for r in r0]
        q = [q_ref[rw, :] * (GLA_DK ** -0.5) for rw in rows]
        k = [k_ref[rw, :] for rw in rows]
        v = [v_ref[rw, :].astype(BF16) for rw in rows]
        qd = [(q[c] * jnp.exp(cum[c])).astype(BF16) for c in cs]
        kin = [(k[c] * jnp.exp(-cum[c])).astype(BF16) for c in cs]
        kst = [(k[c] * jnp.exp(tot[c] - cum[c])).astype(BF16) for c in cs]
        s = [(_dot_nt(qd[c], kin[c]) * incl).astype(BF16) for c in cs]
        o = [_dot(s[c], v[c]) for c in cs]
        st = st_ref[...]
        upd = _dot_tn(v[0], kst[0])
        for c in cs:
            nxt = _dot_tn(v[c + 1], kst[c + 1]) if c + 1 < nc else None
            o[c] = o[c] + _dot_nt(qd[c], st.astype(BF16))
            st = st * jnp.exp(tot[c]) + upd
            upd = nxt
        st_ref[...] = st
        gain = hn_ref[...]
        for c in cs:
            grow = pl.ds(pl.multiple_of(sb * blk + r0[c], CHUNK), CHUNK)
            if bwd:
                ot = of_ref[grow, :] + o[c]
                r = r_ref[rows[c], :]
                o_ref[rows[c], :] = (_rms(ot) * gain * (r * _sigmoid(r))).astype(o_ref.dtype)
            else:
                of_ref[grow, :] = o[c]

    @pl.when(i < nb)
    def _():
        scan_block(False)

    @pl.when(i >= nb)
    def _():
        scan_block(True)


def gla_scan(proj, glo, wg, bg, head_norm, *, B, S, blk=512):
    T = B * S
    blk = min(blk, S)
    nb = S // blk
    H = GLA_HEADS

    def rowblk(b, i):
        return b * nb + jnp.where(i >= nb, 2 * nb - 1 - i, i)

    def outblk(b, i):
        return b * nb + jnp.where(i >= nb, 2 * nb - 1 - i, nb - 1)

    kq = GLA_HEADS * GLA_DK // GLA_DK
    kv = 2 * GLA_HEADS * GLA_DK // GLA_DV
    kr = kv + GLA_HEADS
    return pl.pallas_call(
        functools.partial(_gla_kernel, nb=nb, blk=blk),
        grid=(B, H, 2 * nb),
        in_specs=[pl.BlockSpec((blk, GLA_DK), lambda b, h, i: (rowblk(b, i), h)),
                  pl.BlockSpec((blk, GLA_DK), lambda b, h, i: (rowblk(b, i), kq + h)),
                  pl.BlockSpec((blk, GLA_DV), lambda b, h, i: (rowblk(b, i), kv + h)),
                  pl.BlockSpec((blk, GLA_DV), lambda b, h, i: (outblk(b, i), kr + h)),
                  pl.BlockSpec((blk, LANES), lambda b, h, i: (rowblk(b, i), 0)),
                  pl.BlockSpec((LANES, GLA_DK), lambda b, h, i: (0, h)),
                  pl.BlockSpec((2, GLA_DK), lambda b, h, i: (0, h)),
                  pl.BlockSpec((1, GLA_DV), lambda b, h, i: (0, 0))],
        out_specs=pl.BlockSpec((blk, GLA_DV), lambda b, h, i: (outblk(b, i), h)),
        out_shape=jax.ShapeDtypeStruct((T, H * GLA_DV), BF16),
        scratch_shapes=[pltpu.VMEM((S, GLA_DV), F32),
                        pltpu.VMEM((GLA_DV, GLA_DK), F32)],
        compiler_params=_params(("parallel", "parallel", "arbitrary")),
        name="gla_scan",
    )(proj, proj, proj, proj, glo, wg, bg, head_norm.reshape(1, GLA_DV))


def _gdn_conv_kernel(x_ref, w_ref, o_ref, xp_ref, *, S, rows):
    c = pl.program_id(1)
    pad = 8
    xp_ref[0:pad, :] = jnp.zeros((pad, LANES), F32)
    xp_ref[pad + S:2 * pad + S, :] = jnp.zeros((pad, LANES), F32)
    xp_ref[pad:pad + S, :] = x_ref[...]
    w = w_ref[...]
    win = rows + 2 * pad
    is_qk = c < 2 * GDN_QK_HEADS
    scale = jnp.where(c < GDN_QK_HEADS, GDN_HD ** -0.5, 1.0)

    def conv_silu(t):
        r0 = pl.multiple_of(t * rows, rows)
        xw = xp_ref[pl.ds(r0, win), :]
        acc = jnp.zeros((rows, LANES), F32)
        for j in range(GDN_CONV):
            sh = (GDN_CONV // 2 - j) % win
            xs = xw if sh == 0 else pltpu.roll(xw, sh, 0)
            acc = acc + xs[pad:pad + rows, :] * w[j:j + 1, :]
        return r0, acc * _sigmoid(acc)

    @pl.when(is_qk)
    def _():
        def body(t, carry):
            r0, y = conv_silu(t)
            o_ref[pl.ds(r0, rows), :] = y * (lax.rsqrt(jnp.sum(y * y, axis=-1, keepdims=True) + RMS_EPS) * scale)
            return carry
        lax.fori_loop(0, S // rows, body, 0, unroll=2)

    @pl.when(jnp.logical_not(is_qk))
    def _():
        def body(t, carry):
            r0, y = conv_silu(t)
            o_ref[pl.ds(r0, rows), :] = y
            return carry
        lax.fori_loop(0, S // rows, body, 0, unroll=2)


def gdn_conv(proj, conv_w, *, B, S):
    T = B * S
    nch = conv_w.shape[1] // LANES
    rows = min(256, S)
    return pl.pallas_call(
        functools.partial(_gdn_conv_kernel, S=S, rows=rows),
        grid=(B, nch),
        in_specs=[pl.BlockSpec((S, LANES), lambda b, c: (b, c)),
                  pl.BlockSpec((GDN_CONV, LANES), lambda b, c: (0, c))],
        out_specs=pl.BlockSpec((S, LANES), lambda b, c: (b, c)),
        out_shape=jax.ShapeDtypeStruct((T, nch * LANES), F32),
        scratch_shapes=[pltpu.VMEM((S + 16, LANES), F32)],
        compiler_params=_params(("parallel", "parallel")),
        name="gdn_conv",
    )(proj, conv_w)


def _mm_bf16(a, b):
    return _dot(a.astype(BF16), b.astype(BF16))


def _unit_tri_inverses(Ls):
    row = lax.broadcasted_iota(I32, (CHUNK, CHUNK), 0)
    col = lax.broadcasted_iota(I32, (CHUNK, CHUNK), 1)
    eye = (row == col).astype(F32)
    ps = [eye - L for L in Ls]
    pws = [_mm_bf16(L, L) for L in Ls]
    n = 2
    while True:
        ps = [p + _mm_bf16(p, pw) for p, pw in zip(ps, pws)]
        n *= 2
        if n >= CHUNK:
            return ps
        pws = [_mm_bf16(pw, pw) for pw in pws]


def _chunk_cumsum(x, bwd):
    n = x.shape[0]
    pos = lax.broadcasted_iota(I32, x.shape, 0) % CHUNK
    s = 1
    while s < CHUNK:
        if bwd:
            x = x + jnp.where(pos < CHUNK - s, pltpu.roll(x, n - s, 0), 0.0)
        else:
            x = x + jnp.where(pos >= s, pltpu.roll(x, s, 0), 0.0)
        s *= 2
    return x


def _gdn_prepare(q_ref, k_ref, v_ref, ab_ref, alog, dtb, *, d, qh, blk):
    bwd = d == 1
    nc = blk // CHUNK
    cs = range(nc)
    hs = range(GDN_REP)
    hc = [(h, c) for h in hs for c in cs]
    row = lax.broadcasted_iota(I32, (CHUNK, CHUNK), 0)
    col = lax.broadcasted_iota(I32, (CHUNK, CHUNK), 1)
    incl = (col >= row) if bwd else (col <= row)
    strict = (col > row) if bwd else (col < row)

    sl = [slice(c * CHUNK, (c + 1) * CHUNK) for c in cs]
    q = [q_ref[s, :] for s in sl]
    k = [k_ref[s, :] for s in sl]
    qbf = [t.astype(BF16) for t in q]
    kbf = [t.astype(BF16) for t in k]
    kk = [_dot_nt(kbf[c], kbf[c]) for c in cs]
    qkr = [_dot_nt(qbf[c], kbf[c]) for c in cs]
    x = ab_ref[...]
    gfull = -jnp.exp(alog) * _softplus(x + dtb)
    bfull = _sigmoid(x)
    lane = lax.broadcasted_iota(I32, (blk, LANES), 1)

    def pick(full, ln):
        return jnp.broadcast_to(jnp.sum(jnp.where(lane == ln, full, 0.0), axis=1, keepdims=True), (blk, LANES))

    lane_g = [d * GDN_V_HEADS + GDN_REP * qh + h for h in hs]
    gb = [pick(gfull, ln) for ln in lane_g]
    bb = [pick(bfull, 2 * GDN_V_HEADS + ln) for ln in lane_g]
    gc_blk =[_chunk_cumsum(g, bwd) for g in gb]

    last = [c * CHUNK if bwd else (c + 1) * CHUNK - 1 for c in cs]
    gc = [gc_blk[h][sl[c]] for h, c in hc]
    tot = [gc_blk[h][last[c]:last[c] + 1] for h, c in hc]
    beta = [bb[h][sl[c]] for h, c in hc]
    gamma = [jnp.where(incl, jnp.exp(g[:, :CHUNK] - jnp.transpose(g)[:CHUNK, :]), 0.0) for g in gc]
    tinv =_unit_tri_inverses([jnp.where(strict, kk[c] * beta[j][:, :CHUNK] * gamma[j], 0.0)
                               for j, (h, c) in enumerate(hc)])
    egc = [jnp.exp(g) for g in gc]
    rhs = [jnp.concatenate([v_ref[sl[c], h * GDN_HD:(h + 1) * GDN_HD] * beta[j], k[c] * beta[j] * egc[j]], axis=1)
           for j, (h, c) in enumerate(hc)]
    uw = [_mm_bf16(tinv[j], rhs[j]).astype(BF16) for j in range(len(hc))]
    qk = [(qkr[c] * gamma[j]).astype(BF16) for j, (h, c) in enumerate(hc)]
    kst = [(k[c] * jnp.exp(tot[j] - gc[j])).astype(BF16) for j, (h, c) in enumerate(hc)]
    qd = [q[c] * egc[j] for j, (h, c) in enumerate(hc)]
    return dict(uw=uw, qk=qk, kst=kst, qd=qd, dec=[jnp.exp(t) for t in tot])


def _gdn_chunk_operands(p, j):
    kuw = _dot_tn(p["kst"][j], p["uw"][j])
    quw = _dot(p["qk"][j], p["uw"][j])
    qt = (p["qd"][j] - quw[:, GDN_HD:]).astype(BF16)
    return qt, quw[:, :GDN_HD], kuw[:, GDN_HD:].astype(BF16), kuw[:, :GDN_HD], p["dec"][j]


def _gdn_chunk_step(ops, S):
    qt, qu, kw, ku, dec = ops
    sb16 = S.astype(BF16)
    return _dot(qt, sb16) + qu, S * dec - _dot(kw, sb16) + ku


def _gdn_kernel(qf_ref, kf_ref, vf_ref, abf_ref, qb_ref, kb_ref, vb_ref, abb_ref,
                z_ref, alog_ref, dtb_ref, hn_ref, o_ref, of_ref, ob_ref, st_ref, *, nb, blk, S):
    qh = pl.program_id(1)
    i = pl.program_id(2)
    nc = blk // CHUNK

    @pl.when(i == 0)
    def _():
        st_ref[...] = jnp.zeros_like(st_ref)

    alog = alog_ref[...]
    dtb = dtb_ref[...]
    pf = _gdn_prepare(qf_ref, kf_ref, vf_ref, abf_ref, alog, dtb, d=0, qh=qh, blk=blk)
    pb = _gdn_prepare(qb_ref, kb_ref, vb_ref, abb_ref, alog, dtb, d=1, qh=qh, blk=blk)
    sf = [st_ref[0, h] for h in range(GDN_REP)]
    sb = [st_ref[1, h] for h in range(GDN_REP)]
    rowf = i * blk
    rowb = (nb - 1 - i) * blk
    def operands(c):
        return [(_gdn_chunk_operands(pf, h * nc + c), _gdn_chunk_operands(pb, h * nc + nc - 1 - c))
                for h in range(GDN_REP)]

    nxt = operands(0)
    for c in range(nc):
        cb = nc - 1 - c
        cur = nxt
        if c + 1 < nc:
            nxt = operands(c + 1)
        outs = []
        for h in range(GDN_REP):
            of, sf[h] = _gdn_chunk_step(cur[h][0], sf[h])
            ob, sb[h] = _gdn_chunk_step(cur[h][1], sb[h])
            outs.append((of, ob))
        for h, (of, ob) in enumerate(outs):
            cols = slice(h * GDN_HD, (h + 1) * GDN_HD)
            of_ref[pl.ds(pl.multiple_of(rowf + c * CHUNK, CHUNK), CHUNK), cols] = of
            ob_ref[pl.ds(pl.multiple_of(rowb + cb * CHUNK, CHUNK), CHUNK), cols] = ob
    for h in range(GDN_REP):
        st_ref[0, h] = sf[h]
        st_ref[1, h] = sb[h]

    @pl.when(i == nb - 1)
    def _():
        gain = hn_ref[...]
        rows_e = min(256, S)

        def ebody(t, carry):
            rows = pl.ds(pl.multiple_of(t * rows_e, rows_e), rows_e)
            for h in range(GDN_REP):
                cols = slice(h * GDN_HD, (h + 1) * GDN_HD)
                ot = of_ref[rows, cols] + ob_ref[rows, cols]
                z = z_ref[rows, cols]
                o_ref[rows, cols] = (_rms(ot) * gain * (z * _sigmoid(z))).astype(o_ref.dtype)
            return carry

        lax.fori_loop(0, S // rows_e, ebody, 0)


def gdn_scan(qkv, proj, ab, alog_row, dtb_row, head_norm, *, B, S, blk=512):
    T = B * S
    blk = min(blk, S)
    nb = S // blk
    vw = GDN_REP * GDN_HD
    voff = 2 * GDN_QK_HEADS * GDN_HD // vw
    zoff = (2 * GDN_QK_HEADS + GDN_V_HEADS) * GDN_HD // vw

    def fwd(b, i):
        return b * nb + i

    def bwd(b, i):
        return b * nb + nb - 1 - i

    def dir_specs(rb):
        return [pl.BlockSpec((blk, GDN_HD), lambda b, h, i: (rb(b, i), h)),
                pl.BlockSpec((blk, GDN_HD), lambda b, h, i: (rb(b, i), GDN_QK_HEADS + h)),
                pl.BlockSpec((blk, vw), lambda b, h, i: (rb(b, i), voff + h)),
                pl.BlockSpec((blk, LANES), lambda b, h, i: (rb(b, i), 0))]

    const2 = lambda b, h, i: (0, 0)
    return pl.pallas_call(
        functools.partial(_gdn_kernel, nb=nb, blk=blk, S=S),
        grid=(B, GDN_QK_HEADS, nb),
        in_specs=dir_specs(fwd) + dir_specs(bwd) + [
            pl.BlockSpec((S, vw), lambda b, h, i: (b, zoff + h)),
            pl.BlockSpec((1, LANES), const2),
            pl.BlockSpec((1, LANES), const2),
            pl.BlockSpec((1, GDN_HD), const2)],
        out_specs=pl.BlockSpec((S, vw), lambda b, h, i: (b, h)),
        out_shape=jax.ShapeDtypeStruct((T, GDN_V_HEADS * GDN_HD), BF16),
        scratch_shapes=[pltpu.VMEM((S, vw), F32),
                        pltpu.VMEM((S, vw), F32),
                        pltpu.VMEM((2, GDN_REP, GDN_HD, GDN_HD), F32)],
        compiler_params=_params(("parallel", "parallel", "arbitrary")),
        name="gdn_scan",
    )(qkv, qkv, qkv, ab, qkv, qkv, qkv, ab, proj, alog_row, dtb_row, head_norm.reshape(1, GDN_HD))


def _t5_bucket(rel):
    half = REL_BUCKETS // 2
    max_exact = half // 2
    n = jnp.abs(rel)
    log_ratio = jnp.log(jnp.maximum(n, 1).astype(F32) / max_exact) / math.log(REL_MAX_DIST / max_exact)
    large = jnp.minimum(max_exact + (log_ratio * (half - max_exact)).astype(I32), half - 1)
    return jnp.where(rel > 0, half, 0) + jnp.where(n < max_exact, n, large)


def _rel_bias_heads(table, rel):
    return jnp.moveaxis(table[_t5_bucket(rel)].astype(F32), -1, 0)


def _toeplitz(w, n, m, off):
    lw = w.shape[-1]
    assert lw == n + m - 1 and m <= lw - 1
    w_rot = jnp.roll(w, -off, axis=-1)
    flat = jnp.tile(w_rot, (1,) * (w.ndim - 1) + (n,))[..., :n * (lw - 1)]
    return flat.reshape(w.shape[:-1] + (n, lw - 1))[..., :m]


def _half_rms(x):
    lane = lax.broadcasted_iota(I32, x.shape, 1)
    lo = lane < DIFF_DQK
    xx = x * x
    ms_lo = jnp.sum(jnp.where(lo, xx, 0.0), axis=1, keepdims=True)
    ms_hi = jnp.sum(jnp.where(lo, 0.0, xx), axis=1, keepdims=True)
    ms = jnp.where(lo, ms_lo, ms_hi) * (1.0 / DIFF_DQK)
    return x * lax.rsqrt(ms + RMS_EPS)


LOG2E = math.log2(math.e)


def _diff_kernel(q_ref, k_ref, v_ref, qn_ref, kn_ref, lam_ref, sub_ref, bias_ref, o_ref,
                 kb_ref, vb_ref, m_ref, l_ref, acc_ref, s_ref, *, nk, lambda_init):
    i = pl.program_id(2)
    qb = DIFF_QB

    @pl.when(i == 0)
    def _():
        def kbody(t, carry):
            rows = pl.ds(pl.multiple_of(t * qb, qb), qb)
            kb_ref[rows, :] = (_half_rms(k_ref[rows, :]) * kn_ref[...]).astype(BF16)
            vb_ref[rows, :] = v_ref[rows, :].astype(BF16)
            return carry
        lax.fori_loop(0, nk, kbody, 0, unroll=2)

    q = _half_rms(q_ref[...]) * qn_ref[...] * (DIFF_DQK ** -0.5 * LOG2E)
    lane = lax.broadcasted_iota(I32, (qb, LANES), 1)
    qs = (jnp.where(lane < DIFF_DQK, q, 0.0).astype(BF16), jnp.where(lane >= DIFF_DQK, q, 0.0).astype(BF16))

    m_ref[...] = jnp.full(m_ref.shape, -jnp.inf, F32)
    l_ref[...] = jnp.zeros(l_ref.shape, F32)
    acc_ref[...] = jnp.zeros(acc_ref.shape, F32)

    kw = DIFF_KB // qb
    maps = range(2)

    nsteps = nk * qb // DIFF_KB

    def key_rows(t):
        return pl.ds(t * DIFF_KB, DIFF_KB)

    def scores(t, slot):
        kc = kb_ref[key_rows(t), :]
        bias = jnp.concatenate([bias_ref[0, jnp.clip(kw * t + u - i, -2, 2) + 2] for u in range(kw)], axis=1)
        for mi in maps:
            s_ref[slot, mi] = _dot_nt(qs[mi], kc) + bias

    def chunk(t, prefetch):
        slot = t % 2
        s = [s_ref[slot, mi] for mi in maps]
        if prefetch:
            scores(t + 1, 1 - slot)
        vc = vb_ref[key_rows(t), :]
        m_cur = [jnp.max(s[mi], axis=1, keepdims=True) for mi in maps]
        m_prev = [m_ref[mi] for mi in maps]
        m_new = [jnp.maximum(m_prev[mi], m_cur[mi]) for mi in maps]
        alpha = [jnp.exp2(m_prev[mi] - m_new[mi]) for mi in maps]
        p = [jnp.exp2(s[mi] - jnp.concatenate([m_new[mi]] * (DIFF_KB // LANES), axis=1)) for mi in maps]
        psum = [jnp.sum(p[mi], axis=1, keepdims=True) for mi in maps]
        pv = [_dot(p[mi].astype(BF16), vc) for mi in maps]
        for mi in maps:
            l_ref[mi] = alpha[mi] * l_ref[mi] + psum[mi]
            acc_ref[mi] = alpha[mi] * acc_ref[mi] + pv[mi]
            m_ref[mi] = m_new[mi]

    scores(0, 0)
    for t in range(nsteps - 1):
        chunk(t, True)
    chunk(nsteps - 1, False)

    lam = lam_ref[...]
    lam_full = (jnp.exp(jnp.sum(lam[0:1] * lam[1:2], axis=-1, keepdims=True))
                - jnp.exp(jnp.sum(lam[2:3] * lam[3:4], axis=-1, keepdims=True)) + lambda_init)
    o = acc_ref[0] / l_ref[0] - lam_full * (acc_ref[1] / l_ref[1])
    o_ref[...] = (_rms(o) * sub_ref[...] * (1.0 - lambda_init)).astype(o_ref.dtype)


def diff_attention(proj, q_norm, k_norm, lam, subln, bias_tiles, *, B, S, layer_idx):
    T = B * S
    qb = DIFF_QB
    nq = S // qb
    H = DIFF_HEADS
    lambda_init = 0.8 - 0.6 * math.exp(-0.3 * layer_idx)
    qn2 = jnp.concatenate([q_norm, q_norm]).reshape(1, LANES)
    kn2 = jnp.concatenate([k_norm, k_norm]).reshape(1, LANES)
    return pl.pallas_call(
        functools.partial(_diff_kernel, nk=nq, lambda_init=lambda_init),
        grid=(B, H, nq),
        in_specs=[pl.BlockSpec((qb, LANES), lambda b, h, i: (b * nq + i, h)),
                  pl.BlockSpec((S, LANES), lambda b, h, i: (b, H + h)),
                  pl.BlockSpec((S, LANES), lambda b, h, i: (b, 2 * H + h)),
                  pl.BlockSpec((1, LANES), lambda b, h, i: (0, 0)),
                  pl.BlockSpec((1, LANES), lambda b, h, i: (0, 0)),
                  pl.BlockSpec((4, DIFF_DQK), lambda b, h, i: (0, 0)),
                  pl.BlockSpec((1, DIFF_DV), lambda b, h, i: (0, 0)),
                  pl.BlockSpec((1, 5, qb, qb), lambda b, h, i: (h, 0, 0, 0))],
        out_specs=pl.BlockSpec((qb, DIFF_DV), lambda b, h, i: (b * nq + i, h)),
        out_shape=jax.ShapeDtypeStruct((T, H * DIFF_DV), BF16),
        scratch_shapes=[pltpu.VMEM((S, LANES), BF16),
                        pltpu.VMEM((S, DIFF_DV), BF16),
                        pltpu.VMEM((2, qb, LANES), F32),
                        pltpu.VMEM((2, qb, LANES), F32),
                        pltpu.VMEM((2, qb, DIFF_DV), F32),
                        pltpu.VMEM((2, 2, qb, DIFF_KB), F32)],
        compiler_params=_params(("parallel", "parallel", "arbitrary")),
        name="diff_attn",
    )(proj, proj, proj, qn2, kn2, lam, subln.reshape(1, DIFF_DV), bias_tiles)


def diff_bias_tiles(table):
    qb = DIFF_QB
    span = 3 * qb - 1
    vec = _rel_bias_heads(table, jnp.arange(-span, span + 1, dtype=I32)) * LOG2E
    w = jnp.stack([vec[:, (d + 2) * qb:(d + 2) * qb + 2 * qb - 1] for d in range(-2, 3)], axis=1)
    return _toeplitz(w, qb, qb, qb - 1)


def _swa_kernel(q_ref, *refs, nq, S):
    nkb = SWA_QPS + 2
    k_refs, v_refs = refs[:nkb], refs[nkb:2 * nkb]
    qn_ref, kn_ref, sink_ref, bias_ref, o_ref = refs[2 * nkb:]
    i = pl.program_id(2)
    qb = SWA_QB
    span = 3 * qb
    kcat = jnp.concatenate([r[...] for r in k_refs], axis=0)
    kcat = (_rms(kcat) * kn_ref[...]).astype(BF16)
    vcat = jnp.concatenate([r[...] for r in v_refs], axis=0).astype(BF16)
    row = lax.broadcasted_iota(I32, (qb, span), 0)
    col = lax.broadcasted_iota(I32, (qb, span), 1)
    rel = col - SWA_W - row
    in_window = jnp.abs(rel) <= SWA_W
    sink_all = sink_ref[0]
    pairs = [(a, g) for a in range(SWA_QPS) for g in range(SWA_GROUP)]
    valid = []
    for a in range(SWA_QPS):
        key_pos = (i * SWA_QPS + a) * qb - SWA_W + col
        valid.append(in_window & (key_pos >= 0) & (key_pos < S))
    q = [q_ref[a * qb:(a + 1) * qb, g * SWA_HD:(g + 1) * SWA_HD] for a, g in pairs]
    q = [(_rms(t) * qn_ref[...] * (SWA_HD ** -0.5)).astype(BF16) for t in q]
    s = [_dot_nt(q[j], kcat[a * qb:a * qb + span]) + bias_ref[g] for j, (a, g) in enumerate(pairs)]
    s = [jnp.where(valid[a], s[j], -jnp.inf) for j, (a, g) in enumerate(pairs)]
    sink = [sink_all[g:g + 1, 0:1] for a, g in pairs]
    js = range(len(pairs))
    m = [jnp.maximum(jnp.max(s[j], axis=-1, keepdims=True), sink[j]) for j in js]
    p = [jnp.exp(s[j] - m[j]) for j in js]
    den = [jnp.sum(p[j], axis=-1, keepdims=True) + jnp.exp(sink[j] - m[j]) for j in js]
    o = [_dot(p[j].astype(BF16), vcat[a * qb:a * qb + span]) / den[j] for j, (a, g) in enumerate(pairs)]
    for j, (a, g) in enumerate(pairs):
        o_ref[a * qb:(a + 1) * qb, g * SWA_HD:(g + 1) * SWA_HD] = o[j].astype(o_ref.dtype)


def swa_attention(proj, q_norm, k_norm, sink, bias, *, B, S):
    T = B * S
    qb = SWA_QB
    nq = S // qb
    koff = SWA_HEADS
    voff = SWA_HEADS + SWA_KV
    gw = SWA_GROUP * SWA_HD

    qps = SWA_QPS
    assert nq % qps == 0
    ns = nq // qps

    def kvspec(off, d):
        return pl.BlockSpec((qb, SWA_HD),
                            lambda b, kv, i: (b * nq + jnp.clip(i * qps + d, 0, nq - 1), off + kv))

    kv_specs = [kvspec(off, d) for off in (koff, voff) for d in range(-1, qps + 1)]
    sink_b = jnp.broadcast_to(sink.astype(F32).reshape(SWA_KV, SWA_GROUP, 1), (SWA_KV, SWA_GROUP, LANES))
    return pl.pallas_call(
        functools.partial(_swa_kernel, nq=nq, S=S),
        grid=(B, SWA_KV, ns),
        in_specs=[pl.BlockSpec((qps * qb, gw), lambda b, kv, i: (b * ns + i, kv))] + kv_specs + [
                  pl.BlockSpec((1, SWA_HD), lambda b, kv, i: (0, 0)),
                  pl.BlockSpec((1, SWA_HD), lambda b, kv, i: (0, 0)),
                  pl.BlockSpec((1, SWA_GROUP, LANES), lambda b, kv, i: (kv, 0, 0)),
                  pl.BlockSpec((SWA_GROUP, qb, 3 * qb), lambda b, kv, i: (kv, 0, 0))],
        out_specs=pl.BlockSpec((qps * qb, gw), lambda b, kv, i: (b * ns + i, kv)),
        out_shape=jax.ShapeDtypeStruct((T, SWA_HEADS * SWA_HD), BF16),
        compiler_params=_params(("parallel", "parallel", "parallel")),
        name="swa_attn",
    )(*([proj] * (1 + 2 * (qps + 2))), q_norm.reshape(1, SWA_HD), k_norm.reshape(1, SWA_HD), sink_b, bias)


def swa_bias(table):
    qb = SWA_QB
    span = 3 * qb
    vec = _rel_bias_heads(table, jnp.arange(-(qb - 1) - SWA_W, span - SWA_W, dtype=I32))
    return _toeplitz(vec, qb, span, qb - 1)


def _router_kernel(h_ref, g_ref, wr_ref, hx_ref, at_ref, *, D):
    hn = _rms(h_ref[...]) * g_ref[...]
    x_hi = hn.astype(BF16)
    x_lo = (hn - x_hi.astype(F32)).astype(BF16)
    w = wr_ref[...]
    w_hi = w.astype(BF16)
    w_lo = (w - w_hi.astype(F32)).astype(BF16)
    logits = (_dot(x_hi, w_hi) + _dot(x_hi, w_lo)) + (_dot(x_lo, w_hi) + _dot(x_lo, w_lo))
    lane = lax.broadcasted_iota(I32, logits.shape, 1)
    logits = jnp.where(lane < N_EXPERTS, logits, -jnp.inf)
    m = jnp.max(logits, axis=-1, keepdims=True)
    e = jnp.exp(logits - m)
    aff = e / jnp.sum(e, axis=-1, keepdims=True)
    hx_ref[:, :D] = hn
    hx_ref[:, D:] = aff
    at_ref[0] = jnp.transpose(aff)


def moe_router(h, gain, router, *, B, S, tm=512):
    T, D = h.shape
    tm = min(tm, S)
    ns = S // tm
    wr = jnp.pad(router.astype(F32), ((0, 0), (0, LANES - N_EXPERTS)))
    return pl.pallas_call(
        functools.partial(_router_kernel, D=D),
        grid=(B, ns),
        in_specs=[pl.BlockSpec((tm, D), lambda b, s: (b * ns + s, 0)),
                  pl.BlockSpec((1, D), lambda b, s: (0, 0)),
                  pl.BlockSpec((D, LANES), lambda b, s: (0, 0))],
        out_specs=[pl.BlockSpec((tm, D + LANES), lambda b, s: (b * ns + s, 0)),
                   pl.BlockSpec((1, LANES, tm), lambda b, s: (b, 0, s))],
        out_shape=[jax.ShapeDtypeStruct((T, D + LANES), F32),
                   jax.ShapeDtypeStruct((B, LANES, S), F32)],
        compiler_params=_params(("parallel", "parallel")),
        name="moe_router",
    )(h, gain.reshape(1, D), wr)


def _topk_kernel(aff_ref, idx_ref, pos_ref, *, S, cap):
    E = N_EXPERTS
    v = aff_ref[0]
    bits = pltpu.bitcast(v, I32)

    def search(_, carry):
        lo, hi = carry
        mid = lo + ((hi - lo) >> 1)
        cnt = jnp.sum((bits >= mid).astype(F32), axis=1, keepdims=True)
        ok = cnt >= cap
        return jnp.where(ok, mid, lo), jnp.where(ok, hi, mid)

    lo0 = jnp.zeros((E, 1), I32)
    hi0 = jnp.full((E, 1), 0x7F800001, I32)
    thr, _ = lax.fori_loop(0, 32, search, (lo0, hi0))
    gt = bits > thr
    eq = bits == thr
    need = cap - jnp.sum(gt.astype(F32), axis=1, keepdims=True)

    r = lax.broadcasted_iota(I32, (LANES, LANES), 0)
    c = lax.broadcasted_iota(I32, (LANES, LANES), 1)
    upper = (r < c).astype(BF16)
    run_e = jnp.zeros((E, 1), F32)
    run_s = jnp.zeros((E, 1), F32)
    sub = 512 // LANES
    for t in range(S // LANES):
        sl = slice(t * LANES, (t + 1) * LANES)
        eq_t = eq[:, sl].astype(F32)
        pe = _dot(eq_t.astype(BF16), upper) + run_e
        sel_t = jnp.where(gt[:, sl], 1.0, jnp.where(pe < need, eq_t, 0.0))
        ps = _dot(sel_t.astype(BF16), upper) + run_s
        pos_ref[t // sub, :, (t % sub) * LANES:(t % sub + 1) * LANES] = jnp.where(sel_t > 0, ps, -1.0)
        run_e = run_e + jnp.sum(eq_t, axis=1, keepdims=True)
        run_s = run_s + jnp.sum(sel_t, axis=1, keepdims=True)

    pi = lax.broadcasted_iota(I32, (cap, 512), 0).astype(F32)
    lane = lax.broadcasted_iota(I32, (8, 512), 1)
    rowv = lax.broadcasted_iota(I32, (8, 512), 0)

    def per_expert(e, carry):
        parts = []
        for t in range(S // 512):
            pos = pos_ref[t, pl.ds(e, 1), :]
            onehot = (pi == pos).astype(BF16)
            tok = t * 512 + lane
            vals = jnp.where(rowv == 0, tok >> 6, jnp.where(rowv == 1, tok & 63, 0)).astype(F32).astype(BF16)
            parts.append(_dot_nt(vals, onehot))
        acc = sum(parts)
        idx_ref[0, pl.ds(e, 1), :] = (acc[0:1] * 64.0 + acc[1:2]).astype(I32)
        return carry

    lax.fori_loop(0, E, per_expert, 0)


def moe_topk(aff_t, *, B, S):
    cap = EC_CAPACITY_FACTOR * S // N_EXPERTS
    return pl.pallas_call(
        functools.partial(_topk_kernel, S=S, cap=cap),
        grid=(B,),
        in_specs=[pl.BlockSpec((1, N_EXPERTS, S), lambda b: (b, 0, 0))],
        out_specs=pl.BlockSpec((1, N_EXPERTS, cap), lambda b: (b, 0, 0)),
        out_shape=jax.ShapeDtypeStruct((B, N_EXPERTS, cap), I32),
        scratch_shapes=[pltpu.VMEM((S // 512, N_EXPERTS, 512), F32)],
        compiler_params=_params(("parallel",)),
        name="moe_topk",
    )(aff_t)


def _ffn_kernel(idx0_ref, idx1_ref, idxn_ref, hx_hbm, h_in, w1_ref, w3_ref, w2_ref, h_out,
                xbuf, acc, sem_x, sem_h, sem_s, *, S, D, cap, nj):
    del h_in
    e = pl.program_id(0)
    j = pl.program_id(1)
    first = (e == 0) & (j == 0)
    last = (e == pl.num_programs(0) - 1) & (j == nj - 1)
    base0 = (2 * j) * S
    base1 = base0 + S
    jn = jnp.where(last, j, (j + 1) % nj)
    basen = (2 * jn) * S

    def gather_x(idx_ref, base, slot, r):
        return pltpu.make_async_copy(hx_hbm.at[pl.ds(base + idx_ref[0, 0, r], 1), :],
                                     xbuf.at[slot, pl.ds(r, 1), :], sem_x.at[slot])

    def gather_h(idx_ref, base, slot, r):
        return pltpu.make_async_copy(h_out.at[pl.ds(base + idx_ref[0, 0, r], 1), :],
                                     acc.at[slot, pl.ds(r, 1), :], sem_h.at[slot])

    def scatter_h(idx_ref, base, slot, r):
        return pltpu.make_async_copy(acc.at[slot, pl.ds(r, 1), :],
                                     h_out.at[pl.ds(base + idx_ref[0, 0, r], 1), :], sem_s.at[slot])

    def wait_x(slot):
        pltpu.make_async_copy(hx_hbm.at[pl.ds(0, cap), :], xbuf.at[slot], sem_x.at[slot]).wait()

    def wait_h(slot):
        pltpu.make_async_copy(h_out.at[pl.ds(0, cap), :], acc.at[slot], sem_h.at[slot]).wait()

    def wait_s(slot):
        pltpu.make_async_copy(acc.at[slot], h_out.at[pl.ds(0, cap), :], sem_s.at[slot]).wait()

    def swiglu(slot, starts):
        F = w1_ref.shape[-1]
        ns = FFN_SLABS
        per = -(-len(starts) // (3 * ns))
        pending = list(starts)

        def issue_some():
            for thunk in pending[:per]:
                thunk()
            del pending[:per]

        x = xbuf[slot, :, :D].astype(BF16)
        fs = F // ns
        a, g = [], []
        for s in range(ns):
            issue_some()
            a.append(_dot(x, w1_ref[0, :, s * fs:(s + 1) * fs]))
        for s in range(ns):
            issue_some()
            g.append(_dot(x, w3_ref[0, :, s * fs:(s + 1) * fs]))
        hm = jnp.concatenate([(a[s] * _sigmoid(a[s]) * g[s]).astype(BF16) for s in range(ns)], axis=1)
        ds_ = D // ns
        y = []
        for s in range(ns):
            issue_some()
            y.append(_dot(hm, w2_ref[0, :, s * ds_:(s + 1) * ds_]))
        for thunk in pending:
            thunk()
        return jnp.concatenate(y, axis=1)

    def accumulate(slot, y):
        aff = xbuf[slot, :, D:]
        lane = lax.broadcasted_iota(I32, aff.shape, 1)
        gate = jnp.sum(jnp.where(lane == e, aff, 0.0), axis=1, keepdims=True)
        acc[slot] = acc[slot] + y * gate

    @pl.when(first)
    def _():
        def body(r, carry):
            gather_x(idx0_ref, base0, 0, r).start()
            return carry
        lax.fori_loop(0, cap, body, 0, unroll=8)

    def start(copy_fn, *args):
        return lambda: copy_fn(*args).start()

    wait_x(0)
    starts = []
    for r in range(cap):
        starts.append(start(gather_h, idx0_ref, base0, 0, r))
        starts.append(start(gather_x, idx1_ref, base1, 1, r))
    y = swiglu(0, starts)
    wait_h(0)
    accumulate(0, y)
    wait_x(1)
    starts = []
    for r in range(cap):
        starts.append(start(gather_h, idx1_ref, base1, 1, r))
        starts.append(start(scatter_h, idx0_ref, base0, 0, r))
        starts.append(start(gather_x, idxn_ref, basen, 0, r))
    y = swiglu(1, starts)
    wait_h(1)
    accumulate(1, y)

    def body_out(r, carry):
        scatter_h(idx1_ref, base1, 1, r).start()
        return carry

    lax.fori_loop(0, cap, body_out, 0, unroll=8)
    wait_s(0)
    wait_s(1)

    @pl.when(last)
    def _():
        wait_x(0)


def moe_ffn(idx, hx, h, w1, w3, w2, *, B, S):
    T, D = h.shape
    E = N_EXPERTS
    cap = idx.shape[-1]
    F = w1.shape[-1]
    assert B % 2 == 0
    nj = B // 2
    idx3 = idx.reshape(B * E, 1, cap)

    def idx_spec(fn):
        return pl.BlockSpec((1, 1, cap), fn, memory_space=pltpu.SMEM)

    def nxt(e, j):
        is_last = (e == E - 1) & (j == nj - 1)
        en = jnp.where(is_last, e, e + (j + 1) // nj)
        jn = jnp.where(is_last, j, (j + 1) % nj)
        return (2 * jn * E + en, 0, 0)

    return pl.pallas_call(
        functools.partial(_ffn_kernel, S=S, D=D, cap=cap, nj=nj),
        grid=(E, nj),
        in_specs=[idx_spec(lambda e, j: (2 * j * E + e, 0, 0)),
                  idx_spec(lambda e, j: ((2 * j + 1) * E + e, 0, 0)),
                  idx_spec(nxt),
                  pl.BlockSpec(memory_space=pl.ANY),
                  pl.BlockSpec(memory_space=pl.ANY),
                  pl.BlockSpec((1, D, F), lambda e, j: (e, 0, 0)),
                  pl.BlockSpec((1, D, F), lambda e, j: (e, 0, 0)),
                  pl.BlockSpec((1, F, D), lambda e, j: (e, 0, 0))],
        out_specs=pl.BlockSpec(memory_space=pl.ANY),
        out_shape=jax.ShapeDtypeStruct((T, D), F32),
        scratch_shapes=[pltpu.VMEM((2, cap, D + LANES), F32),
                        pltpu.VMEM((2, cap, D), F32),
                        pltpu.SemaphoreType.DMA((2,)),
                        pltpu.SemaphoreType.DMA((2,)),
                        pltpu.SemaphoreType.DMA((2,))],
        input_output_aliases={4: 0},
        compiler_params=_params(("arbitrary", "arbitrary")),
        name="moe_ffn",
    )(idx3, idx3, idx3, hx, h, w1, w3, w2)


def ec_moe(h, gain, router, w1, w3, w2, *, B, S):
    hx, aff_t = moe_router(h, gain, router, B=B, S=S)
    idx = moe_topk(aff_t, B=B, S=S)
    return moe_ffn(idx, hx, h, w1.astype(BF16), w3.astype(BF16), w2.astype(BF16), B=B, S=S)


def _pad_cols(w, n):
    return jnp.pad(w, ((0, 0), (0, n - w.shape[1])))


def gla_layer(h, norm_gain, w_in, w_gate_up, b_gate, head_norm, w_out, *, B, S):
    nmain = 2 * GLA_HEADS * GLA_DK + 2 * GLA_HEADS * GLA_DV
    proj = norm_matmul(h, norm_gain, w_in[:, :nmain].astype(BF16), name="gla_in")
    glo = norm_matmul(h, norm_gain, _pad_cols(w_in[:, nmain:], LANES).astype(BF16), tn=LANES, name="gla_in_gate")
    wg = jnp.pad(w_gate_up.reshape(2 * GLA_RANK, -1), ((0, LANES - 2 * GLA_RANK), (0, 0))).astype(BF16)
    y = gla_scan(proj, glo, wg, b_gate.astype(F32), head_norm, B=B, S=S)
    return matmul_residual(y, w_out.astype(BF16), h, name="gla_out")


def gdn_layer(h, norm_gain, w_in, conv_w, a_log, dt_bias, head_norm, w_out, *, B, S):
    nconv = conv_w.shape[1]
    nmain = nconv + GDN_V_HEADS * GDN_HD
    proj = norm_matmul(h, norm_gain, w_in[:, :nmain].astype(BF16), name="gdn_in")
    ab = norm_matmul(h, norm_gain, w_in[:, nmain:].astype(BF16), tn=LANES, name="gdn_in_gate")
    qkv = gdn_conv(proj, conv_w, B=B, S=S)
    zeros = jnp.zeros((2 * GDN_V_HEADS,), F32)
    alog_row = jnp.concatenate([a_log.astype(F32).reshape(-1), zeros]).reshape(1, LANES)
    dtb_row = jnp.concatenate([dt_bias.astype(F32).reshape(-1), zeros]).reshape(1, LANES)
    y = gdn_scan(qkv, proj, ab, alog_row, dtb_row, head_norm, B=B, S=S)
    return matmul_residual(y, w_out.astype(BF16), h, name="gdn_out")


def diff_layer(h, norm_gain, w_in, q_norm, k_norm, lam, subln, w_out, bias_tiles, layer_idx, *, B, S):
    proj = norm_matmul(h, norm_gain, w_in.astype(BF16), name="diff_in")
    y = diff_attention(proj, q_norm, k_norm, lam, subln, bias_tiles, B=B, S=S, layer_idx=layer_idx)
    return matmul_residual(y, w_out.astype(BF16), h, name="diff_out")


def swa_layer(h, norm_gain, w_in, q_norm, k_norm, sink, w_out, bias, *, B, S):
    proj = norm_matmul(h, norm_gain, w_in.astype(BF16), name="swa_in")
    y = swa_attention(proj, q_norm, k_norm, sink, bias, B=B, S=S)
    return matmul_residual(y, w_out.astype(BF16), h, name="swa_out")


def kernel(x, rel_bias, norm_mix, norm_ffn, gla_w_in, gla_w_gate_up, gla_b_gate, gla_head_norm, gla_w_out, gdn_w_in, gdn_conv, gdn_a_log, gdn_dt_bias, gdn_head_norm, gdn_w_out, diff_w_in, diff_q_norm, diff_k_norm, diff_lambda, diff_subln, diff_w_out, swa_w_in, swa_q_norm, swa_k_norm, swa_sink, swa_w_out, moe_router, moe_w1, moe_w3, moe_w2):
    B, S, D = x.shape
    depth = norm_mix.shape[0]
    h = x.reshape(B * S, D)
    for i in range(depth):
        m, j = i % 4, i // 4
        if m == 0:
            h = gla_layer(h, norm_mix[i], gla_w_in[j], gla_w_gate_up[j], gla_b_gate[j], gla_head_norm[j],
                          gla_w_out[j], B=B, S=S)
        elif m == 1:
            h = gdn_layer(h, norm_mix[i], gdn_w_in[j], gdn_conv[j], gdn_a_log[j], gdn_dt_bias[j],
                          gdn_head_norm[j], gdn_w_out[j], B=B, S=S)
        elif m == 2:
            h = diff_layer(h, norm_mix[i], diff_w_in[j], diff_q_norm[j], diff_k_norm[j], diff_lambda[j],
                           diff_subln[j], diff_w_out[j], diff_bias_tiles(rel_bias), i, B=B, S=S)
        else:
            h = swa_layer(h, norm_mix[i], swa_w_in[j], swa_q_norm[j], swa_k_norm[j], swa_sink[j],
                          swa_w_out[j], swa_bias(rel_bias), B=B, S=S)
        h = ec_moe(h, norm_ffn[i], moe_router[i], moe_w1[i], moe_w3[i], moe_w2[i], B=B, S=S)
    return h.reshape(B, S, D)
```
